```python
import jax, jax.numpy as jnp
from jax import lax
import numpy as np

D_MODEL = 1024
BATCH = 4
SEQ = 4096
DEPTH = 4
DEC_BATCH = 32
DEC_SEQ = 8
PAST_LEN = 8192
PAGE_SIZE = 128

N_MIXERS = 3
HEAD_DIM = 64
NSA_HEADS = D_MODEL // HEAD_DIM
NSA_KV_HEADS = 4
NSA_GROUP = NSA_HEADS // NSA_KV_HEADS
CMP_BLOCK = 32
SEL_BLOCK = 64
SEL_RATIO = SEL_BLOCK // CMP_BLOCK
N_SEL = 16
WINDOW = 512
CMP_HIDDEN = 2 * HEAD_DIM
NSA_Q_DIM = NSA_HEADS * HEAD_DIM
NSA_KV_DIM = 6 * NSA_KV_HEADS * HEAD_DIM
NSA_IN_DIM = NSA_Q_DIM + NSA_KV_DIM + 3 * NSA_HEADS
POOL_WINDOWS = (2, 4, 8, 16)
POOL_GROUP = D_MODEL // len(POOL_WINDOWS)
POOL_PAST = max(POOL_WINDOWS) - 1
MOBA_HEADS = D_MODEL // HEAD_DIM
MOBA_KV_HEADS = 4
MOBA_GROUP = MOBA_HEADS // MOBA_KV_HEADS
MOBA_BLOCK = 256
MOBA_TOPK = 3
MOBA_Q_DIM = MOBA_HEADS * HEAD_DIM
MOBA_IN_DIM = MOBA_Q_DIM + 2 * MOBA_KV_HEADS * HEAD_DIM
D_FF = 4 * D_MODEL
ALPHA = (2 * DEPTH) ** 0.25
BETA = (8 * DEPTH) ** -0.25
LN_EPS = 1e-5
NEG_INF = -1e30
NSA_Q_CHUNK = 64
MOBA_Q_CHUNK = 16
N_NSA_LAYERS = len(range(0, DEPTH, N_MIXERS))
N_POOL_LAYERS = len(range(1, DEPTH, N_MIXERS))
N_MOBA_LAYERS = len(range(2, DEPTH, N_MIXERS))

kernel_name = "nsa_pool_moba_hybrid_step"


def alibi_slopes(n_heads):
    return jnp.exp2(-8.0 * jnp.arange(1, n_heads + 1, dtype=jnp.float32) / n_heads)


def layer_norm(x, g, b):
    xf = x.astype(jnp.float32)
    mu = xf.mean(-1, keepdims=True)
    var = jnp.square(xf - mu).mean(-1, keepdims=True)
    return ((xf - mu) * lax.rsqrt(var + LN_EPS) * g + b).astype(x.dtype)


def masked_softmax(s, mask):
    p = jax.nn.softmax(jnp.where(mask, s, NEG_INF), axis=-1)
    return jnp.where(mask, p, 0.0)


def map_query_chunks(fn, chunk, *xs):
    b, t = xs[0].shape[:2]
    n = t // chunk
    xs_c = tuple(a.reshape((b, n, chunk) + a.shape[2:]).swapaxes(0, 1) for a in xs)
    out = lax.map(lambda args: fn(args[0], *args[1:]), (jnp.arange(n, dtype=jnp.int32),) + xs_c)
    return out.swapaxes(0, 1).reshape((b, t) + out.shape[3:])


def gather_blocks(blocks, idx):
    return jax.vmap(jax.vmap(lambda bg, ig: bg[ig]))(blocks, idx)


def compress_blocks(k, pe, w1, w2):
    b, l, g, hd = k.shape
    nc = l // CMP_BLOCK
    blk = k.reshape(b, nc, CMP_BLOCK, g, hd) + pe[:, None, :]
    blk = blk.transpose(0, 1, 3, 2, 4).reshape(b, nc, g, CMP_BLOCK * hd)
    return jax.nn.gelu(blk @ w1) @ w2


def nsa_mixer(x, kv_past, win_past, pos0, chunk, w_in, w_out, cmp_pe, cmp_w1, cmp_w2):
    b, t_len, _ = x.shape
    g, r, hd = NSA_KV_HEADS, NSA_GROUP, HEAD_DIM
    proj = x @ w_in
    q = proj[..., :NSA_Q_DIM].reshape(b, t_len, g, r, hd)
    kv = proj[..., NSA_Q_DIM:NSA_Q_DIM + NSA_KV_DIM].reshape(b, t_len, 6, g, hd)
    gate = jax.nn.sigmoid(proj[..., NSA_Q_DIM + NSA_KV_DIM:].astype(jnp.float32)).reshape(b, t_len, 3, g, r)
    kv_new = kv[:, :, :4]
    kv_all = jnp.concatenate([kv_past, kv_new], axis=1)
    l = kv_all.shape[1]
    nc = l // CMP_BLOCK
    kc = compress_blocks(kv_all[:, :nc * CMP_BLOCK, 0], cmp_pe[0], cmp_w1[0], cmp_w2[0])
    vc = compress_blocks(kv_all[:, :nc * CMP_BLOCK, 1], cmp_pe[1], cmp_w1[1], cmp_w2[1])
    c_end = jnp.arange(nc, dtype=jnp.int32) * CMP_BLOCK + (CMP_BLOCK - 1)
    c_mid = c_end.astype(jnp.float32) - 0.5 * (CMP_BLOCK - 1)
    ns = -(-l // SEL_BLOCK)
    sel = jnp.pad(kv_all[:, :, 2:4], ((0, 0), (0, ns * SEL_BLOCK - l), (0, 0), (0, 0), (0, 0)))
    sel = sel.reshape(b, ns, SEL_BLOCK, 2, g, hd).transpose(3, 0, 4, 1, 2, 5)
    ks, vs = sel[0], sel[1]
    k_sel = min(N_SEL - 1, ns)
    n_blk = k_sel + 1
    win_all = jnp.concatenate([win_past, kv[:, :, 4:6]], axis=1)
    wpos0 = pos0 - win_past.shape[1]
    win_pad = jnp.pad(win_all, ((0, 0), (WINDOW, 0), (0, 0), (0, 0), (0, 0)))
    wlen = WINDOW + chunk - 1
    slopes = alibi_slopes(NSA_HEADS).reshape(g, r)
    scale = HEAD_DIM ** -0.5

    def chunk_fn(i, qc, gc):
        t0 = pos0 + i * chunk
        t = t0 + jnp.arange(chunk, dtype=jnp.int32)
        s = jnp.einsum('bcgrd,bngd->bgrcn', qc, kc, preferred_element_type=jnp.float32) * scale
        s = s - slopes[:, :, None, None] * jnp.abs(t.astype(jnp.float32)[:, None] - c_mid[None, :])
        p_cmp = masked_softmax(s, c_end[None, :] <= t[:, None])
        o_cmp = jnp.einsum('bgrcn,bngd->bcgrd', p_cmp.astype(vc.dtype), vc)
        imp = p_cmp.sum(axis=2)
        imp = jnp.pad(imp, ((0, 0), (0, 0), (0, 0), (0, ns * SEL_RATIO - nc)))
        imp = imp.reshape(b, g, chunk, ns, SEL_RATIO).sum(-1)
        cur = t // SEL_BLOCK
        imp = jnp.where(jnp.arange(ns)[None, :] < cur[:, None], imp, -1.0)
        _, idx = lax.top_k(imp, k_sel)
        valid = idx < cur[:, None]
        idx = jnp.concatenate([idx, jnp.broadcast_to(cur[:, None], (b, g, chunk, 1))], axis=-1)
        valid = jnp.concatenate([valid, jnp.ones((b, g, chunk, 1), dtype=bool)], axis=-1)
        kg = gather_blocks(ks, idx)
        vg = gather_blocks(vs, idx)
        kpos = idx[..., None] * SEL_BLOCK + jnp.arange(SEL_BLOCK, dtype=jnp.int32)
        s = jnp.einsum('bcgrd,bgcnkd->bgrcnk', qc, kg, preferred_element_type=jnp.float32) * scale
        s = s - slopes[:, :, None, None, None] * jnp.abs(t[:, None, None] - kpos)[:, :, None].astype(jnp.float32)
        m = (valid[..., None] & (kpos <= t[:, None, None]))[:, :, None]
        p = masked_softmax(s.reshape(b, g, r, chunk, n_blk * SEL_BLOCK),
                           m.reshape(b, g, 1, chunk, n_blk * SEL_BLOCK))
        o_slc = jnp.einsum('bgrcm,bgcmd->bcgrd', p.astype(vg.dtype),
                           vg.reshape(b, g, chunk, n_blk * SEL_BLOCK, hd))
        kw = lax.dynamic_slice_in_dim(win_pad, t0 - wpos0 + 1, wlen, axis=1)
        kp = t0 + 1 - WINDOW + jnp.arange(wlen, dtype=jnp.int32)
        dist = t[:, None] - kp[None, :]
        s = jnp.einsum('bcgrd,bkgd->bgrck', qc, kw[:, :, 0], preferred_element_type=jnp.float32) * scale
        s = s - slopes[:, :, None, None] * jnp.abs(dist).astype(jnp.float32)
        m = (dist >= 0) & (dist < WINDOW) & (kp[None, :] >= wpos0)
        p = masked_softmax(s, m)
        o_win = jnp.einsum('bgrck,bkgd->bcgrd', p.astype(kw.dtype), kw[:, :, 1])
        o = (gc[:, :, 0, :, :, None] * o_cmp + gc[:, :, 1, :, :, None] * o_slc
             + gc[:, :, 2, :, :, None] * o_win)
        return o.astype(x.dtype)

    o = map_query_chunks(chunk_fn, chunk, q, gate)
    y = o.reshape(b, t_len, NSA_Q_DIM) @ w_out
    keep = min(WINDOW, win_all.shape[1])
    return y, kv_new, win_all[:, -keep:]


def pool_mixer(x, past, pos0, w_grp, scale):
    b, t_len, d = x.shape
    xe = jnp.concatenate([past, x], axis=1)
    cs = jnp.pad(jnp.cumsum(xe.astype(jnp.float32), axis=1), ((0, 0), (1, 0), (0, 0)))
    p = past.shape[1]
    n_avail = pos0 + jnp.arange(t_len, dtype=jnp.int32) + 1
    means = []
    for gi, w in enumerate(POOL_WINDOWS):
        lo, hi = gi * POOL_GROUP, (gi + 1) * POOL_GROUP
        win_sum = cs[:, p + 1:p + t_len + 1, lo:hi] - cs[:, p + 1 - w:p + t_len + 1 - w, lo:hi]
        cnt = jnp.minimum(n_avail, w).astype(jnp.float32)
        means.append(win_sum / cnt[None, :, None])
    pooled = jnp.concatenate(means, axis=-1) - x.astype(jnp.float32)
    pooled = pooled.reshape(b, t_len, len(POOL_WINDOWS), POOL_GROUP).astype(x.dtype)
    mixed = jnp.einsum('btgc,gce->btge', pooled, w_grp).reshape(b, t_len, d)
    return (mixed * scale).astype(x.dtype), xe[:, -POOL_PAST:]


def moba_mixer(x, kv_past, pos0, chunk, w_in, w_out):
    b, t_len, _ = x.shape
    g, r, hd = MOBA_KV_HEADS, MOBA_GROUP, HEAD_DIM
    proj = x @ w_in
    q = proj[..., :MOBA_Q_DIM].reshape(b, t_len, g, r, hd)
    kv_new = proj[..., MOBA_Q_DIM:].reshape(b, t_len, 2, g, hd)
    kv_all = jnp.concatenate([kv_past, kv_new], axis=1)
    l = kv_all.shape[1]
    nb = -(-l // MOBA_BLOCK)
    kvb = jnp.pad(kv_all, ((0, 0), (0, nb * MOBA_BLOCK - l), (0, 0), (0, 0), (0, 0)))
    kvb = kvb.reshape(b, nb, MOBA_BLOCK, 2, g, hd).transpose(3, 0, 4, 1, 2, 5)
    kb, vb = kvb[0], kvb[1]
    k_mean = kb.astype(jnp.float32).mean(axis=3)
    k_top = min(MOBA_TOPK, nb)
    nk = k_top * MOBA_BLOCK
    slopes = alibi_slopes(MOBA_HEADS).reshape(g, r)
    scale = HEAD_DIM ** -0.5

    def chunk_fn(i, qc):
        t = pos0 + i * chunk + jnp.arange(chunk, dtype=jnp.int32)
        cur = t // MOBA_BLOCK
        gs = jnp.einsum('bcgrd,bgnd->bgrcn', qc.astype(jnp.float32), k_mean)
        gs = jnp.where(jnp.arange(nb)[None, :] < cur[:, None], gs, NEG_INF)
        _, idx = lax.top_k(gs, k_top)
        valid = idx < cur[:, None]
        ksel = gather_blocks(kb, idx)
        vsel = gather_blocks(vb, idx)
        kpos = idx[..., None] * MOBA_BLOCK + jnp.arange(MOBA_BLOCK, dtype=jnp.int32)
        s_sel = jnp.einsum('bcgrd,bgrcnkd->bgrcnk', qc, ksel, preferred_element_type=jnp.float32) * scale
        s_sel = s_sel - slopes[:, :, None, None, None] * jnp.abs(t[:, None, None] - kpos).astype(jnp.float32)
        m_sel = jnp.broadcast_to(valid[..., None], kpos.shape)
        kown = kb[:, :, cur]
        vown = vb[:, :, cur]
        opos = cur[:, None] * MOBA_BLOCK + jnp.arange(MOBA_BLOCK, dtype=jnp.int32)
        s_own = jnp.einsum('bcgrd,bgckd->bgrck', qc, kown, preferred_element_type=jnp.float32) * scale
        s_own = s_own - slopes[:, :, None, None] * jnp.abs(t[:, None] - opos).astype(jnp.float32)
        m_own = jnp.broadcast_to(opos <= t[:, None], s_own.shape)
        s = jnp.concatenate([s_sel.reshape(b, g, r, chunk, nk), s_own], axis=-1)
        m = jnp.concatenate([m_sel.reshape(b, g, r, chunk, nk), m_own], axis=-1)
        p = masked_softmax(s, m).astype(vb.dtype)
        o = (jnp.einsum('bgrcm,bgrcmd->bcgrd', p[..., :nk], vsel.reshape(b, g, r, chunk, nk, hd))
             + jnp.einsum('bgrck,bgckd->bcgrd', p[..., nk:], vown))
        return o.astype(x.dtype)

    o = map_query_chunks(chunk_fn, chunk, q)
    return o.reshape(b, t_len, MOBA_Q_DIM) @ w_out, kv_new


def sq_relu_mlp(x, w1, w2):
    h = jax.nn.relu(x @ w1)
    return (h * h) @ w2


def setup_inputs(seed: int = 0) -> dict:
    key = jax.random.key(seed)
    ks = jax.random.split(key, 24)
    n_pages = PAST_LEN // PAGE_SIZE
    n_phys = (DEC_BATCH * n_pages * 5) // 4
    w_buf = min(WINDOW, PAST_LEN)

    def nrm(k, shape, s=1.0):
        return jax.random.normal(k, shape, jnp.float32) * s

    page_table = jax.random.permutation(ks[0], n_phys)[:DEC_BATCH * n_pages]
    page_table = page_table.reshape(DEC_BATCH, n_pages).astype(jnp.int32)
    return {
        "x_prompt": nrm(ks[1], (BATCH, SEQ, D_MODEL)),
        "x_sample": nrm(ks[2], (DEC_BATCH, DEC_SEQ, D_MODEL)),
        "cache_nsa_kv": nrm(ks[3], (N_NSA_LAYERS, n_phys, PAGE_SIZE, 4, NSA_KV_HEADS, HEAD_DIM)),
        "state_nsa_win": nrm(ks[4], (N_NSA_LAYERS, DEC_BATCH, w_buf, 2, NSA_KV_HEADS, HEAD_DIM)),
        "state_pool": nrm(ks[5], (N_POOL_LAYERS, DEC_BATCH, POOL_PAST, D_MODEL)),
        "cache_moba": nrm(ks[6], (N_MOBA_LAYERS, n_phys, PAGE_SIZE, 2, MOBA_KV_HEADS, HEAD_DIM)),
        "page_table": page_table,
        "ln_g": 1.0 + nrm(ks[7], (DEPTH, 2, D_MODEL), 0.05),
        "ln_b": nrm(ks[8], (DEPTH, 2, D_MODEL), 0.02),
        "mlp_w1": nrm(ks[9], (DEPTH, D_MODEL, D_FF), D_MODEL ** -0.5),
        "mlp_w2": nrm(ks[10], (DEPTH, D_FF, D_MODEL), BETA * D_FF ** -0.5),
        "nsa_w_in": nrm(ks[11], (N_NSA_LAYERS, D_MODEL, NSA_IN_DIM), D_MODEL ** -0.5),
        "nsa_w_out": nrm(ks[12], (N_NSA_LAYERS, NSA_Q_DIM, D_MODEL), BETA * NSA_Q_DIM ** -0.5),
        "nsa_cmp_pe": nrm(ks[13], (N_NSA_LAYERS, 2, CMP_BLOCK, HEAD_DIM), 0.5),
        "nsa_cmp_w1": nrm(ks[14], (N_NSA_LAYERS, 2, CMP_BLOCK * HEAD_DIM, CMP_HIDDEN), (CMP_BLOCK * HEAD_DIM) ** -0.5),
        "nsa_cmp_w2": nrm(ks[15], (N_NSA_LAYERS, 2, CMP_HIDDEN, HEAD_DIM), 2.0 * CMP_HIDDEN ** -0.5),
        "pool_w": nrm(ks[16], (N_POOL_LAYERS, len(POOL_WINDOWS), POOL_GROUP, POOL_GROUP), BETA * POOL_GROUP ** -0.5),
        "pool_scale": 1.0 + nrm(ks[17], (N_POOL_LAYERS, D_MODEL), 0.1),
        "moba_w_in": nrm(ks[18], (N_MOBA_LAYERS, D_MODEL, MOBA_IN_DIM), D_MODEL ** -0.5),
        "moba_w_out": nrm(ks[19], (N_MOBA_LAYERS, MOBA_Q_DIM, D_MODEL), BETA * MOBA_Q_DIM ** -0.5),
    }


def reference(x_prompt, x_sample, cache_nsa_kv, state_nsa_win, state_pool, cache_moba, page_table,
              ln_g, ln_b, mlp_w1, mlp_w2, nsa_w_in, nsa_w_out, nsa_cmp_pe, nsa_cmp_w1, nsa_cmp_w2,
              pool_w, pool_scale, moba_w_in, moba_w_out):
    bp, bs = x_prompt.shape[0], x_sample.shape[0]
    n_pages = page_table.shape[1]
    past_len = n_pages * PAGE_SIZE

    def gather_pages(pool):
        return pool[page_table].reshape((bs, past_len) + pool.shape[2:])

    xp, xs = x_prompt, x_sample
    nsa_kv_p, nsa_kv_s, nsa_win_p, nsa_win_s = [], [], [], []
    pool_p, pool_s, moba_p, moba_s = [], [], [], []
    for i in range(DEPTH):
        kind, j = i % N_MIXERS, i // N_MIXERS
        if kind == 0:
            wts = (nsa_w_in[j], nsa_w_out[j], nsa_cmp_pe[j], nsa_cmp_w1[j], nsa_cmp_w2[j])
            hp, kvp, wp = nsa_mixer(xp, jnp.zeros((bp, 0, 4, NSA_KV_HEADS, HEAD_DIM), xp.dtype),
                                    jnp.zeros((bp, 0, 2, NSA_KV_HEADS, HEAD_DIM), xp.dtype),
                                    0, NSA_Q_CHUNK, *wts)
            hs, kvs, ws = nsa_mixer(xs, gather_pages(cache_nsa_kv[j]), state_nsa_win[j],
                                    past_len, xs.shape[1], *wts)
            nsa_kv_p.append(kvp); nsa_kv_s.append(kvs); nsa_win_p.append(wp); nsa_win_s.append(ws)
        elif kind == 1:
            hp, sp = pool_mixer(xp, jnp.zeros((bp, POOL_PAST, xp.shape[2]), xp.dtype), 0, pool_w[j], pool_scale[j])
            hs, ss = pool_mixer(xs, state_pool[j], past_len, pool_w[j], pool_scale[j])
            pool_p.append(sp); pool_s.append(ss)
        else:
            hp, kvp = moba_mixer(xp, jnp.zeros((bp, 0, 2, MOBA_KV_HEADS, HEAD_DIM), xp.dtype),
                                 0, MOBA_Q_CHUNK, moba_w_in[j], moba_w_out[j])
            hs, kvs = moba_mixer(xs, gather_pages(cache_moba[j]), past_len, 1, moba_w_in[j], moba_w_out[j])
            moba_p.append(kvp); moba_s.append(kvs)
        xp = layer_norm(ALPHA * xp + hp, ln_g[i, 0], ln_b[i, 0])
        xs = layer_norm(ALPHA * xs + hs, ln_g[i, 0], ln_b[i, 0])
        xp = layer_norm(ALPHA * xp + sq_relu_mlp(xp, mlp_w1[i], mlp_w2[i]), ln_g[i, 1], ln_b[i, 1])
        xs = layer_norm(ALPHA * xs + sq_relu_mlp(xs, mlp_w1[i], mlp_w2[i]), ln_g[i, 1], ln_b[i, 1])

    return (xp, xs,
            jnp.stack(nsa_kv_p), jnp.stack(nsa_kv_s), jnp.stack(nsa_win_p), jnp.stack(nsa_win_s),
            jnp.stack(pool_p), jnp.stack(pool_s), jnp.stack(moba_p), jnp.stack(moba_s))
```

```python
import functools

import jax
import jax.numpy as jnp
from jax import lax
from jax.experimental import pallas as pl
from jax.experimental.pallas import tpu as pltpu

F32 = jnp.float32
BF16 = jnp.bfloat16

HEAD_DIM = 64
PAGE_SIZE = 128
N_MIXERS = 3
NSA_KV_HEADS = 4
NSA_GROUP = 4
CMP_BLOCK = 32
SEL_BLOCK = 64
N_SEL = 16
WINDOW = 512
CMP_HIDDEN = 2 * HEAD_DIM
POOL_WINDOWS = (2, 4, 8, 16)
POOL_PAST = max(POOL_WINDOWS) - 1
POOL_HALO = 16
MOBA_KV_HEADS = 4
MOBA_GROUP = 4
MOBA_BLOCK = 256
MOBA_TOPK = 3
LN_EPS = 1e-5
NEG_INF = -1e30
QK_SCALE = HEAD_DIM ** -0.5

LANE = 128
VMEM_LIMIT = 56 * 1024 * 1024
NT_DIMS = (((1,), (1,)), ((), ()))


def _params(*sem):
    return pltpu.CompilerParams(dimension_semantics=sem, vmem_limit_bytes=VMEM_LIMIT)


def _layer_norm(y, g, b):
    mu = jnp.mean(y, axis=-1, keepdims=True)
    d = y - mu
    var = jnp.mean(d * d, axis=-1, keepdims=True)
    return d * lax.rsqrt(var + LN_EPS) * g + b


def _row_tile(m, want):
    t = min(m, want)
    while m % t:
        t //= 2
    return t


def _proj_body(x_ref, w_ref, o_ref, ob_ref, *, lo, hi):
    acc = jnp.dot(x_ref[...].astype(BF16), w_ref[...], preferred_element_type=F32)
    o_ref[...] = acc
    ob_ref[...] = acc[:, lo:hi].astype(BF16)


def _proj(x, w_bf, lo, hi):
    m, d = x.shape
    n = w_bf.shape[1]
    tm = _row_tile(m, 512)
    return pl.pallas_call(
        functools.partial(_proj_body, lo=lo, hi=hi),
        grid=(m // tm,),
        in_specs=[pl.BlockSpec((tm, d), lambda i: (i, 0)),
                  pl.BlockSpec((d, n), lambda i: (0, 0))],
        out_specs=[pl.BlockSpec((tm, n), lambda i: (i, 0)),
                   pl.BlockSpec((tm, hi - lo), lambda i: (i, 0))],
        out_shape=[jax.ShapeDtypeStruct((m, n), F32),
                   jax.ShapeDtypeStruct((m, hi - lo), BF16)],
        compiler_params=_params("parallel"),
        name="proj",
    )(x, w_bf)


def _out_ln_body(o_ref, x_ref, w_ref, g_ref, b_ref, y_ref, *, alpha):
    h = jnp.dot(o_ref[...].astype(BF16), w_ref[...], preferred_element_type=F32)
    y_ref[...] = _layer_norm(alpha * x_ref[...] + h, g_ref[...], b_ref[...])


def _out_ln(o, x, w_bf, g, b, alpha):
    m, d = x.shape
    k = o.shape[1]
    tm = _row_tile(m, 512)
    return pl.pallas_call(
        functools.partial(_out_ln_body, alpha=alpha),
        grid=(m // tm,),
        in_specs=[pl.BlockSpec((tm, k), lambda i: (i, 0)),
                  pl.BlockSpec((tm, d), lambda i: (i, 0)),
                  pl.BlockSpec((k, d), lambda i: (0, 0)),
                  pl.BlockSpec((1, d), lambda i: (0, 0)),
                  pl.BlockSpec((1, d), lambda i: (0, 0))],
        out_specs=pl.BlockSpec((tm, d), lambda i: (i, 0)),
        out_shape=jax.ShapeDtypeStruct((m, d), F32),
        compiler_params=_params("parallel"),
        name="out_ln",
    )(o, x, w_bf, g, b)


def _mlp_ln_body(x_ref, w1_ref, w2_ref, g_ref, b_ref, y_ref, acc_ref, *, alpha):
    j = pl.program_id(1)

    @pl.when(j == 0)
    def _():
        acc_ref[...] = jnp.zeros_like(acc_ref)

    h = jnp.dot(x_ref[...].astype(BF16), w1_ref[...], preferred_element_type=F32)
    h = jnp.maximum(h, 0.0)
    acc_ref[...] += jnp.dot((h * h).astype(BF16), w2_ref[...], preferred_element_type=F32)

    @pl.when(j == pl.num_programs(1) - 1)
    def _():
        y_ref[...] = _layer_norm(alpha * x_ref[...] + acc_ref[...], g_ref[...], b_ref[...])


def _mlp_ln(x, w1_bf, w2_bf, g, b, alpha):
    m, d = x.shape
    f = w1_bf.shape[1]
    tm = _row_tile(m, 1024)
    tf = 512
    return pl.pallas_call(
        functools.partial(_mlp_ln_body, alpha=alpha),
        grid=(m // tm, f // tf),
        in_specs=[pl.BlockSpec((tm, d), lambda i, j: (i, 0)),
                  pl.BlockSpec((d, tf), lambda i, j: (0, j)),
                  pl.BlockSpec((tf, d), lambda i, j: (j, 0)),
                  pl.BlockSpec((1, d), lambda i, j: (0, 0)),
                  pl.BlockSpec((1, d), lambda i, j: (0, 0))],
        out_specs=pl.BlockSpec((tm, d), lambda i, j: (i, 0)),
        out_shape=jax.ShapeDtypeStruct((m, d), F32),
        scratch_shapes=[pltpu.VMEM((tm, d), F32)],
        compiler_params=_params("parallel", "arbitrary"),
        name="mlp_ln",
    )(x, w1_bf, w2_bf, g, b)


def _pool_ln_body(halo_ref, x_ref, w_ref, scale_ref, g_ref, b_ref, y_ref, *, alpha, pos0, tiles_per_seq):
    tm, d = x_ref.shape
    grp = d // len(POOL_WINDOWS)
    seq_tile = pl.program_id(0) % tiles_per_seq
    x = x_ref[...]
    halo = halo_ref[...]
    if pos0 == 0:
        halo = jnp.where(seq_tile == 0, 0.0, halo)
    xe = jnp.concatenate([halo, x], axis=0)
    n_avail = pos0 + seq_tile * tm + lax.broadcasted_iota(jnp.int32, (tm, 1), 0) + 1
    mixed = []
    for gi, w in enumerate(POOL_WINDOWS):
        s = xe[:, gi * grp:(gi + 1) * grp]
        step = 1
        while step < w:
            s = s + pltpu.roll(s, step, 0)
            step *= 2
        cnt = jnp.minimum(n_avail, w).astype(F32)
        pooled = s[POOL_HALO:] / cnt - x[:, gi * grp:(gi + 1) * grp]
        mixed.append(jnp.dot(pooled.astype(BF16), w_ref[gi], preferred_element_type=F32))
    h = jnp.concatenate(mixed, axis=-1) * scale_ref[...]
    y_ref[...] = _layer_norm(alpha * x + h, g_ref[...], b_ref[...])


def _pool_ln(rows, halo_spec, x_spec, n_tiles, tm, tiles_per_seq, pos0, w_bf, scale, g, b, alpha):
    d = rows.shape[-1]
    ngrp = len(POOL_WINDOWS)
    const = lambda i: (0, 0)
    return pl.pallas_call(
        functools.partial(_pool_ln_body, alpha=alpha, pos0=pos0, tiles_per_seq=tiles_per_seq),
        grid=(n_tiles,),
        in_specs=[halo_spec, x_spec,
                  pl.BlockSpec((ngrp, d // ngrp, d // ngrp), lambda i: (0, 0, 0)),
                  pl.BlockSpec((1, d), const), pl.BlockSpec((1, d), const), pl.BlockSpec((1, d), const)],
        out_specs=pl.BlockSpec((tm, d), lambda i: (i, 0)),
        out_shape=jax.ShapeDtypeStruct((n_tiles * tm, d), F32),
        compiler_params=_params("parallel"),
        name="pool_ln",
    )(rows, rows, w_bf, scale, g, b)


def _pool_ln_prompt(x2d, seq_len, w_bf, scale, g, b, alpha):
    m, d = x2d.shape
    tm = _row_tile(seq_len, 512)
    per = tm // POOL_HALO
    halo_spec = pl.BlockSpec((POOL_HALO, d), lambda i: (jnp.maximum(i * per - 1, 0), 0))
    x_spec = pl.BlockSpec((tm, d), lambda i: (i, 0))
    return _pool_ln(x2d, halo_spec, x_spec, m // tm, tm, seq_len // tm, 0, w_bf, scale, g, b, alpha)


def _pool_ln_sample(xe3, ts, pos0, w_bf, scale, g, b, alpha):
    bs, _, d = xe3.shape
    halo_spec = pl.BlockSpec((None, POOL_HALO, d), lambda i: (i, 0, 0))
    x_spec = pl.BlockSpec((None, ts, d), lambda i: (i, POOL_HALO // ts, 0))
    return _pool_ln(xe3, halo_spec, x_spec, bs, ts, 1, pos0, w_bf, scale, g, b, alpha)


def _topk_mask(vals, valid, k):
    n = vals.shape[-1]
    lane = lax.broadcasted_iota(jnp.int32, vals.shape, 1).astype(F32)
    work = jnp.where(valid, vals, -jnp.inf)
    sel = jnp.zeros(vals.shape, F32)
    for _ in range(k):
        m = jnp.max(work, axis=-1, keepdims=True)
        first = jnp.min(jnp.where(work == m, lane, float(n)), axis=-1, keepdims=True)
        pick = lane == first
        sel = jnp.where(pick, 1.0, sel)
        work = jnp.where(pick, -jnp.inf, work)
    return jnp.where(valid, sel, 0.0)


def _softmax_init(rows, vdim):
    return (jnp.full((rows, 1), NEG_INF, F32), jnp.zeros((rows, 1), F32), jnp.zeros((rows, vdim), F32))


def _softmax_step(carry, qb, k, v, t, slope, kpos, mask_fn):
    m, l, acc = carry
    s = lax.dot_general(qb, k, NT_DIMS, preferred_element_type=F32)
    dist = t - kpos
    s = s - slope * jnp.abs(dist).astype(F32)
    mask = mask_fn(dist)
    s = jnp.where(mask, s, NEG_INF)
    m_new = jnp.maximum(m, jnp.max(s, axis=-1, keepdims=True))
    a = jnp.exp(m - m_new)
    p = jnp.where(mask, jnp.exp(s - m_new), 0.0)
    l = a * l + jnp.sum(p, axis=-1, keepdims=True)
    acc = a * acc + jnp.dot(p.astype(BF16), v, preferred_element_type=F32)
    return m_new, l, acc


def _softmax_done(carry):
    _, l, acc = carry
    return acc / jnp.where(l > 0.0, l, 1.0)


def _block_mask(sel_bf, key0, tk, blk_shift):
    nblk = sel_bf.shape[1]
    kblk = (key0 + lax.broadcasted_iota(jnp.int32, (nblk, tk), 1)) >> blk_shift
    expand = jnp.where(kblk == lax.broadcasted_iota(jnp.int32, (nblk, tk), 0), 1.0, 0.0).astype(BF16)
    return jnp.dot(sel_bf, expand, preferred_element_type=F32) > 0.5


def _cmp_positions(nc):
    lane = lax.broadcasted_iota(jnp.int32, (1, nc), 1)
    half = nc // 2
    n = jnp.where(lane < half, 2 * lane, 2 * (lane - half) + 1)
    c_end = n * CMP_BLOCK + (CMP_BLOCK - 1)
    c_mid = c_end.astype(F32) - 0.5 * (CMP_BLOCK - 1)
    return c_end, c_mid


def _cmp_attention(q, kc, vc, t, slope):
    nc = kc.shape[0]
    c_end, c_mid = _cmp_positions(nc)
    s = lax.dot_general(q, kc, NT_DIMS, preferred_element_type=F32, precision=lax.Precision.HIGHEST)
    s = s - slope * jnp.abs(t.astype(F32) - c_mid)
    mask = c_end <= t
    s = jnp.where(mask, s, NEG_INF)
    m = jnp.max(s, axis=-1, keepdims=True)
    p = jnp.where(mask, jnp.exp(s - m), 0.0)
    l = jnp.sum(p, axis=-1, keepdims=True)
    p = p / jnp.where(l > 0.0, l, 1.0)
    o = jnp.dot(p.astype(BF16), vc.astype(BF16), preferred_element_type=F32)
    return p, o


HEADS_PER_LANE_TILE = LANE // HEAD_DIM


def _compress_rows(src_ref, pe_ref, w1_ref, w2_ref, nc):
    n_h, hd, half = HEADS_PER_LANE_TILE, HEAD_DIM, nc // 2
    acc = jnp.zeros((n_h * nc, CMP_HIDDEN), F32)
    for r in range(CMP_BLOCK):
        even = src_ref[pl.ds(r, half, stride=2 * CMP_BLOCK), :]
        odd = src_ref[pl.ds(CMP_BLOCK + r, half, stride=2 * CMP_BLOCK), :]
        rows = jnp.concatenate([even, odd], axis=0)
        rows = jnp.concatenate([rows[:, g * hd:(g + 1) * hd] for g in range(n_h)], axis=0)
        rows = rows + pe_ref[r:r + 1, :]
        acc = acc + jnp.dot(rows.astype(BF16), w1_ref[r * hd:(r + 1) * hd, :], preferred_element_type=F32)
    h = jax.nn.gelu(acc)
    out = jnp.dot(h.astype(BF16), w2_ref[...], preferred_element_type=F32)
    return jnp.concatenate([out[g * nc:(g + 1) * nc] for g in range(n_h)], axis=-1)


def _compress_prompt_body(src_ref, pe_ref, w1_ref, w2_ref, o_ref, *, nc):
    o_ref[...] = _compress_rows(src_ref, pe_ref, w1_ref, w2_ref, nc)


def _compress_prompt(proj3, col0, pe, w1_bf, w2_bf):
    b, t, _ = proj3.shape
    gd = NSA_KV_HEADS * HEAD_DIM
    nc = t // CMP_BLOCK
    tiles = gd // LANE
    return pl.pallas_call(
        functools.partial(_compress_prompt_body, nc=nc),
        grid=(2, b, tiles),
        in_specs=[pl.BlockSpec((None, t, LANE), lambda c, i, h: (i, 0, col0 // LANE + c * tiles + h)),
                  pl.BlockSpec((None, CMP_BLOCK, HEAD_DIM), lambda c, i, h: (c, 0, 0)),
                  pl.BlockSpec((None, CMP_BLOCK * HEAD_DIM, CMP_HIDDEN), lambda c, i, h: (c, 0, 0)),
                  pl.BlockSpec((None, CMP_HIDDEN, HEAD_DIM), lambda c, i, h: (c, 0, 0))],
        out_specs=pl.BlockSpec((None, None, nc, LANE), lambda c, i, h: (c, i, 0, h)),
        out_shape=jax.ShapeDtypeStruct((2, b, nc, gd), F32),
        compiler_params=_params("parallel", "parallel", "parallel"),
        name="nsa_compress_prompt",
    )(proj3, pe, w1_bf, w2_bf)


def _gather_pages(pt_ref, b, cache_hbm, col0, ncol, buf, sem, n_pages):
    def copy(p):
        return pltpu.make_async_copy(cache_hbm.at[pt_ref[b, p], :, pl.ds(col0, ncol)],
                                     buf.at[pl.ds(p * PAGE_SIZE, PAGE_SIZE), :], sem)

    def start():
        for p in range(n_pages):
            copy(p).start()

    def wait():
        for p in range(n_pages):
            copy(p).wait()

    return start, wait


def _compress_sample_body(pt_ref, cache_hbm, pe_ref, w1_ref, w2_ref, o_ref, buf, sem, *, n_pages, nc):
    tiles = NSA_KV_HEADS * HEAD_DIM // LANE
    b = pl.program_id(0)

    def copy(p, c):
        return pltpu.make_async_copy(cache_hbm.at[pt_ref[b, p], :, pl.ds(c * LANE, LANE)],
                                     buf.at[c, pl.ds(p * PAGE_SIZE, PAGE_SIZE), :], sem)

    for p in range(n_pages):
        for c in range(2 * tiles):
            copy(p, c).start()
    for p in range(n_pages):
        for c in range(2 * tiles):
            copy(p, c).wait()
    for c in range(2):
        for h in range(tiles):
            o_ref[c, :, h * LANE:(h + 1) * LANE] = _compress_rows(
                buf.at[c * tiles + h], pe_ref.at[c], w1_ref.at[c], w2_ref.at[c], nc)


def _compress_sample(cache3, page_table, pe, w1_bf, w2_bf):
    bs, n_pages = page_table.shape
    gd = NSA_KV_HEADS * HEAD_DIM
    past = n_pages * PAGE_SIZE
    nc = past // CMP_BLOCK
    return pl.pallas_call(
        functools.partial(_compress_sample_body, n_pages=n_pages, nc=nc),
        grid_spec=pltpu.PrefetchScalarGridSpec(
            num_scalar_prefetch=1,
            grid=(bs,),
            in_specs=[pl.BlockSpec(memory_space=pl.ANY),
                      pl.BlockSpec((2, CMP_BLOCK, HEAD_DIM), lambda i, pt: (0, 0, 0)),
                      pl.BlockSpec((2, CMP_BLOCK * HEAD_DIM, CMP_HIDDEN), lambda i, pt: (0, 0, 0)),
                      pl.BlockSpec((2, CMP_HIDDEN, HEAD_DIM), lambda i, pt: (0, 0, 0))],
            out_specs=pl.BlockSpec((None, 2, nc, gd), lambda i, pt: (i, 0, 0, 0)),
            scratch_shapes=[pltpu.VMEM((2 * gd // LANE, past, LANE), F32), pltpu.SemaphoreType.DMA(())]),
        out_shape=jax.ShapeDtypeStruct((bs, 2, nc, gd), F32),
        compiler_params=_params("arbitrary"),
        name="nsa_compress_sample",
    )(page_table, cache3, pe, w1_bf, w2_bf)


def _nsa_prompt_body(slope_ref, q_ref, gate_ref, kc_ref, vc_ref, ks_ref, vs_ref, kw_ref, vw_ref, o_ref,
                     *, tq, tk):
    i = pl.program_id(1)
    g_n, r_n, hd = NSA_KV_HEADS, NSA_GROUP, HEAD_DIM
    rows = r_n * tq
    nc = kc_ref.shape[0]
    ns = nc // 2
    qi = lax.broadcasted_iota(jnp.int32, (rows, 1), 0) % tq
    t = i * tq + qi
    t_q = i * tq + lax.broadcasted_iota(jnp.int32, (tq, 1), 0)
    cur = t_q // SEL_BLOCK
    blk = lax.broadcasted_iota(jnp.int32, (tq, ns), 1)
    gate = jax.nn.sigmoid(gate_ref[...])
    n_slc = ((i + 1) * tq + tk - 1) // tk
    w_first = jnp.maximum(i * tq - (WINDOW - 1), 0) // tk
    w_last = ((i + 1) * tq - 1) // tk

    for g in range(g_n):
        q = jnp.concatenate([q_ref[:, (g * r_n + r) * hd:(g * r_n + r + 1) * hd] for r in range(r_n)],
                            axis=0) * QK_SCALE
        qb = q.astype(BF16)
        slope = slope_ref[g]
        gsl = slice(g * hd, (g + 1) * hd)

        p_cmp, o_cmp = _cmp_attention(q, kc_ref[:, gsl], vc_ref[:, gsl], t, slope)
        imp = p_cmp[0:tq]
        for r in range(1, r_n):
            imp = imp + p_cmp[r * tq:(r + 1) * tq]
        imp = imp[:, :ns] + imp[:, ns:]
        sel = _topk_mask(imp, blk < cur, min(N_SEL - 1, ns))
        sel = jnp.where(blk == cur, 1.0, sel).astype(BF16)
        sel = jnp.concatenate([sel] * r_n, axis=0)

        def slc_step(j, carry):
            key0 = pl.multiple_of(j * tk, tk)
            kpos = key0 + lax.broadcasted_iota(jnp.int32, (1, tk), 1)
            in_sel = _block_mask(sel, key0, tk, SEL_BLOCK.bit_length() - 1)
            return _softmax_step(carry, qb, ks_ref[pl.ds(key0, tk), gsl], vs_ref[pl.ds(key0, tk), gsl],
                                 t, slope, kpos, lambda dist: in_sel & (dist >= 0))

        o_slc = _softmax_done(lax.fori_loop(0, n_slc, slc_step, _softmax_init(rows, hd)))

        def win_step(j, carry):
            key0 = pl.multiple_of(j * tk, tk)
            kpos = key0 + lax.broadcasted_iota(jnp.int32, (1, tk), 1)
            return _softmax_step(carry, qb, kw_ref[pl.ds(key0, tk), gsl], vw_ref[pl.ds(key0, tk), gsl],
                                 t, slope, kpos, lambda dist: (dist >= 0) & (dist < WINDOW))

        o_win = _softmax_done(lax.fori_loop(w_first, w_last + 1, win_step, _softmax_init(rows, hd)))

        for r in range(r_n):
            h = g * r_n + r
            rs = slice(r * tq, (r + 1) * tq)
            n_h = g_n * r_n
            o_ref[:, h * hd:(h + 1) * hd] = (gate[:, h:h + 1] * o_cmp[rs]
                                             + gate[:, n_h + h:n_h + h + 1] * o_slc[rs]
                                             + gate[:, 2 * n_h + h:2 * n_h + h + 1] * o_win[rs])


def _nsa_prompt_attn(proj3, kvb3, kcvc, slopes, tq=128, tk=128):
    b, t, _ = proj3.shape
    gd = NSA_KV_HEADS * HEAD_DIM
    qd = NSA_KV_HEADS * NSA_GROUP * HEAD_DIM
    nc = kcvc.shape[2]
    rows = NSA_GROUP * tq
    slope_rows = jnp.repeat(slopes.reshape(NSA_KV_HEADS, NSA_GROUP), tq, axis=1).reshape(NSA_KV_HEADS, rows, 1)
    kv_spec = lambda c: pl.BlockSpec((None, t, gd), lambda bi, i: (bi, 0, c))
    return pl.pallas_call(
        functools.partial(_nsa_prompt_body, tq=tq, tk=tk),
        grid=(b, t // tq),
        in_specs=[pl.BlockSpec((NSA_KV_HEADS, rows, 1), lambda bi, i: (0, 0, 0)),
                  pl.BlockSpec((None, tq, qd), lambda bi, i: (bi, i, 0)),
                  pl.BlockSpec((None, tq, LANE), lambda bi, i: (bi, i, (qd + 6 * gd) // LANE)),
                  pl.BlockSpec((None, None, nc, gd), lambda bi, i: (0, bi, 0, 0)),
                  pl.BlockSpec((None, None, nc, gd), lambda bi, i: (1, bi, 0, 0)),
                  kv_spec(2), kv_spec(3), kv_spec(4), kv_spec(5)],
        out_specs=pl.BlockSpec((None, tq, qd), lambda bi, i: (bi, i, 0)),
        out_shape=jax.ShapeDtypeStruct((b, t, qd), F32),
        compiler_params=_params("parallel", "arbitrary"),
        name="nsa_prompt_attn",
    )(slope_rows, proj3, proj3, kcvc, kcvc, kvb3, kvb3, kvb3, kvb3)


def _block_diag_q(q_ref, g_n, r_n):
    hd = HEAD_DIM
    ts = q_ref.shape[0]
    out = []
    for g in range(g_n):
        qg = jnp.concatenate([q_ref[:, (g * r_n + r) * hd:(g * r_n + r + 1) * hd] for r in range(r_n)], axis=0)
        parts = [jnp.zeros((r_n * ts, hd), F32)] * g_n
        parts[g] = qg
        out.append(jnp.concatenate(parts, axis=-1))
    return jnp.concatenate(out, axis=0)


def _nsa_sample_body(pt_ref, slope_ref, q_ref, gate_ref, kcvc_ref, snew_ref, wnew_ref, win_ref, cache_hbm,
                     o_ref, buf, sem, *, n_pages, tk):
    g_n, r_n, hd = NSA_KV_HEADS, NSA_GROUP, HEAD_DIM
    gd = g_n * hd
    ts = q_ref.shape[0]
    rows = g_n * r_n * ts
    past = n_pages * PAGE_SIZE
    start, wait = _gather_pages(pt_ref, pl.program_id(0), cache_hbm, 2 * gd, 2 * gd, buf, sem, n_pages)
    start()

    qf = _block_diag_q(q_ref, g_n, r_n) * QK_SCALE
    qb = qf.astype(BF16)
    slope = slope_ref[...]
    t = past + lax.broadcasted_iota(jnp.int32, (rows, 1), 0) % ts
    new_pos = past + lax.broadcasted_iota(jnp.int32, (1, ts), 1)
    causal = lambda dist: dist >= 0

    nc = kcvc_ref.shape[1]
    ns = nc // 2
    p_cmp, o_cmp = _cmp_attention(qf, kcvc_ref[0], kcvc_ref[1], t, slope)
    imp = []
    for g in range(g_n):
        base = g * r_n * ts
        acc = p_cmp[base:base + ts]
        for r in range(1, r_n):
            acc = acc + p_cmp[base + r * ts:base + (r + 1) * ts]
        imp.append(acc)
    imp = jnp.concatenate(imp, axis=0)
    imp = imp[:, :ns] + imp[:, ns:]
    sel = _topk_mask(imp, jnp.full(imp.shape, True), min(N_SEL - 1, ns)).astype(BF16)
    sel = jnp.concatenate([sel[g * ts:(g + 1) * ts] for g in range(g_n) for _ in range(r_n)], axis=0)

    wp = win_ref.shape[0]
    win_pos = (past - wp) + lax.broadcasted_iota(jnp.int32, (1, wp), 1)
    in_win = lambda dist: (dist >= 0) & (dist < WINDOW)
    carry = _softmax_step(_softmax_init(rows, gd), qb, win_ref[:, :gd].astype(BF16),
                          win_ref[:, gd:].astype(BF16), t, slope, win_pos, in_win)
    carry = _softmax_step(carry, qb, wnew_ref[:, :gd].astype(BF16), wnew_ref[:, gd:].astype(BF16),
                          t, slope, new_pos, in_win)
    o_win = _softmax_done(carry)

    wait()

    def slc_step(j, carry):
        key0 = pl.multiple_of(j * tk, tk)
        kpos = key0 + lax.broadcasted_iota(jnp.int32, (1, tk), 1)
        in_sel = _block_mask(sel, key0, tk, SEL_BLOCK.bit_length() - 1)
        return _softmax_step(carry, qb, buf[pl.ds(key0, tk), :gd].astype(BF16),
                             buf[pl.ds(key0, tk), gd:].astype(BF16), t, slope, kpos, lambda dist: in_sel)

    carry = lax.fori_loop(0, past // tk, slc_step, _softmax_init(rows, gd))
    carry = _softmax_step(carry, qb, snew_ref[:, :gd].astype(BF16), snew_ref[:, gd:].astype(BF16),
                          t, slope, new_pos, causal)
    o_slc = _softmax_done(carry)

    gate = jax.nn.sigmoid(gate_ref[...])
    n_h = g_n * r_n
    for h in range(n_h):
        g = h // r_n
        rs = slice(h * ts, (h + 1) * ts)
        cs = slice(g * hd, (g + 1) * hd)
        o_ref[:, h * hd:(h + 1) * hd] = (gate[:, h:h + 1] * o_cmp[rs, cs]
                                         + gate[:, n_h + h:n_h + h + 1] * o_slc[rs, cs]
                                         + gate[:, 2 * n_h + h:2 * n_h + h + 1] * o_win[rs, cs])


def _nsa_sample_attn(proj3, kcvc, win_state, cache3, page_table, slopes, tk=512):
    bs, ts, _ = proj3.shape
    n_pages = page_table.shape[1]
    gd = NSA_KV_HEADS * HEAD_DIM
    qd = NSA_KV_HEADS * NSA_GROUP * HEAD_DIM
    nc = kcvc.shape[2]
    wp = win_state.shape[1]
    slope_rows = jnp.repeat(slopes, ts).reshape(-1, 1)
    return pl.pallas_call(
        functools.partial(_nsa_sample_body, n_pages=n_pages, tk=tk),
        grid_spec=pltpu.PrefetchScalarGridSpec(
            num_scalar_prefetch=1,
            grid=(bs,),
            in_specs=[pl.BlockSpec((slope_rows.shape[0], 1), lambda i, pt: (0, 0)),
                      pl.BlockSpec((None, ts, qd), lambda i, pt: (i, 0, 0)),
                      pl.BlockSpec((None, ts, LANE), lambda i, pt: (i, 0, (qd + 6 * gd) // LANE)),
                      pl.BlockSpec((None, 2, nc, gd), lambda i, pt: (i, 0, 0, 0)),
                      pl.BlockSpec((None, ts, 2 * gd), lambda i, pt: (i, 0, (qd + 2 * gd) // (2 * gd))),
                      pl.BlockSpec((None, ts, 2 * gd), lambda i, pt: (i, 0, (qd + 4 * gd) // (2 * gd))),
                      pl.BlockSpec((None, wp, 2 * gd), lambda i, pt: (i, 0, 0)),
                      pl.BlockSpec(memory_space=pl.ANY)],
            out_specs=pl.BlockSpec((None, ts, qd), lambda i, pt: (i, 0, 0)),
            scratch_shapes=[pltpu.VMEM((n_pages * PAGE_SIZE, 2 * gd), F32), pltpu.SemaphoreType.DMA(())]),
        out_shape=jax.ShapeDtypeStruct((bs, ts, qd), F32),
        compiler_params=_params("arbitrary"),
        name="nsa_sample_attn",
    )(page_table, slope_rows, proj3, proj3, kcvc, proj3, proj3, win_state, cache3)


def _block_means(src_ref, ncol, nb):
    return jnp.concatenate(
        [jnp.mean(src_ref[n * MOBA_BLOCK:(n + 1) * MOBA_BLOCK, :ncol], axis=0, keepdims=True) for n in range(nb)],
        axis=0)


def _moba_prompt_body(slope_ref, q_ref, kf_ref, k_ref, v_ref, o_ref, kmean_ref, *, tq, tk):
    i = pl.program_id(1)
    g_n, r_n, hd = MOBA_KV_HEADS, MOBA_GROUP, HEAD_DIM
    rows = r_n * tq
    nb = kmean_ref.shape[0]

    @pl.when(i == 0)
    def _():
        kmean_ref[...] = _block_means(kf_ref, g_n * hd, nb)

    t = i * tq + lax.broadcasted_iota(jnp.int32, (rows, 1), 0) % tq
    cur = t // MOBA_BLOCK
    blk = lax.broadcasted_iota(jnp.int32, (rows, nb), 1)
    n_tiles = ((i + 1) * tq + tk - 1) // tk

    for g in range(g_n):
        q = jnp.concatenate([q_ref[:, (g * r_n + r) * hd:(g * r_n + r + 1) * hd] for r in range(r_n)], axis=0)
        qb = (q * QK_SCALE).astype(BF16)
        slope = slope_ref[g]
        gsl = slice(g * hd, (g + 1) * hd)
        gs = lax.dot_general(q, kmean_ref[:, gsl], NT_DIMS, preferred_element_type=F32,
                             precision=lax.Precision.HIGHEST)
        sel = _topk_mask(gs, blk < cur, min(MOBA_TOPK, nb))
        sel = jnp.where(blk == cur, 1.0, sel).astype(BF16)

        def step(j, carry):
            key0 = pl.multiple_of(j * tk, tk)
            kpos = key0 + lax.broadcasted_iota(jnp.int32, (1, tk), 1)
            in_sel = _block_mask(sel, key0, tk, MOBA_BLOCK.bit_length() - 1)
            return _softmax_step(carry, qb, k_ref[pl.ds(key0, tk), gsl], v_ref[pl.ds(key0, tk), gsl],
                                 t, slope, kpos, lambda dist: in_sel & (dist >= 0))

        o = _softmax_done(lax.fori_loop(0, n_tiles, step, _softmax_init(rows, hd)))
        for r in range(r_n):
            h = g * r_n + r
            o_ref[:, h * hd:(h + 1) * hd] = o[r * tq:(r + 1) * tq]


def _moba_prompt_attn(proj3, kvb3, slopes, tq=128, tk=256):
    b, t, _ = proj3.shape
    gd = MOBA_KV_HEADS * HEAD_DIM
    qd = MOBA_KV_HEADS * MOBA_GROUP * HEAD_DIM
    rows = MOBA_GROUP * tq
    tk = min(tk, t)
    slope_rows = jnp.repeat(slopes.reshape(MOBA_KV_HEADS, MOBA_GROUP), tq, axis=1).reshape(MOBA_KV_HEADS, rows, 1)
    return pl.pallas_call(
        functools.partial(_moba_prompt_body, tq=tq, tk=tk),
        grid=(b, t // tq),
        in_specs=[pl.BlockSpec((MOBA_KV_HEADS, rows, 1), lambda bi, i: (0, 0, 0)),
                  pl.BlockSpec((None, tq, qd), lambda bi, i: (bi, i, 0)),
                  pl.BlockSpec((None, t, gd), lambda bi, i: (bi, 0, qd // gd)),
                  pl.BlockSpec((None, t, gd), lambda bi, i: (bi, 0, 0)),
                  pl.BlockSpec((None, t, gd), lambda bi, i: (bi, 0, 1))],
        out_specs=pl.BlockSpec((None, tq, qd), lambda bi, i: (bi, i, 0)),
        out_shape=jax.ShapeDtypeStruct((b, t, qd), F32),
        scratch_shapes=[pltpu.VMEM((t // MOBA_BLOCK, gd), F32)],
        compiler_params=_params("parallel", "arbitrary"),
        name="moba_prompt_attn",
    )(slope_rows, proj3, proj3, kvb3, kvb3)


def _moba_sample_body(pt_ref, slope_ref, q_ref, new_ref, cache_hbm, o_ref, buf, sem, *, n_pages, tk):
    g_n, r_n, hd = MOBA_KV_HEADS, MOBA_GROUP, HEAD_DIM
    gd = g_n * hd
    ts = q_ref.shape[0]
    rows = g_n * r_n * ts
    past = n_pages * PAGE_SIZE
    nb = past // MOBA_BLOCK
    start, wait = _gather_pages(pt_ref, pl.program_id(0), cache_hbm, 0, 2 * gd, buf, sem, n_pages)
    start()
    qf = _block_diag_q(q_ref, g_n, r_n)
    qb = (qf * QK_SCALE).astype(BF16)
    slope = slope_ref[...]
    t = past + lax.broadcasted_iota(jnp.int32, (rows, 1), 0) % ts
    new_pos = past + lax.broadcasted_iota(jnp.int32, (1, ts), 1)
    wait()

    kmean = _block_means(buf, gd, nb)
    gs = lax.dot_general(qf, kmean, NT_DIMS, preferred_element_type=F32, precision=lax.Precision.HIGHEST)
    sel = _topk_mask(gs, jnp.full(gs.shape, True), min(MOBA_TOPK, nb)).astype(BF16)

    def step(j, carry):
        key0 = pl.multiple_of(j * tk, tk)
        kpos = key0 + lax.broadcasted_iota(jnp.int32, (1, tk), 1)
        in_sel = _block_mask(sel, key0, tk, MOBA_BLOCK.bit_length() - 1)
        return _softmax_step(carry, qb, buf[pl.ds(key0, tk), :gd].astype(BF16),
                             buf[pl.ds(key0, tk), gd:].astype(BF16), t, slope, kpos, lambda dist: in_sel)

    carry = lax.fori_loop(0, past // tk, step, _softmax_init(rows, gd))
    carry = _softmax_step(carry, qb, new_ref[:, :gd].astype(BF16), new_ref[:, gd:].astype(BF16),
                          t, slope, new_pos, lambda dist: dist >= 0)
    o = _softmax_done(carry)
    for h in range(g_n * r_n):
        g = h // r_n
        o_ref[:, h * hd:(h + 1) * hd] = o[h * ts:(h + 1) * ts, g * hd:(g + 1) * hd]


def _moba_sample_attn(proj3, cache3, page_table, slopes, tk=512):
    bs, ts, _ = proj3.shape
    n_pages = page_table.shape[1]
    gd = MOBA_KV_HEADS * HEAD_DIM
    qd = MOBA_KV_HEADS * MOBA_GROUP * HEAD_DIM
    slope_rows = jnp.repeat(slopes, ts).reshape(-1, 1)
    return pl.pallas_call(
        functools.partial(_moba_sample_body, n_pages=n_pages, tk=tk),
        grid_spec=pltpu.PrefetchScalarGridSpec(
            num_scalar_prefetch=1,
            grid=(bs,),
            in_specs=[pl.BlockSpec((slope_rows.shape[0], 1), lambda i, pt: (0, 0)),
                      pl.BlockSpec((None, ts, qd), lambda i, pt: (i, 0, 0)),
                      pl.BlockSpec((None, ts, 2 * gd), lambda i, pt: (i, 0, qd // (2 * gd))),
                      pl.BlockSpec(memory_space=pl.ANY)],
            out_specs=pl.BlockSpec((None, ts, qd), lambda i, pt: (i, 0, 0)),
            scratch_shapes=[pltpu.VMEM((n_pages * PAGE_SIZE, 2 * gd), F32), pltpu.SemaphoreType.DMA(())]),
        out_shape=jax.ShapeDtypeStruct((bs, ts, qd), F32),
        compiler_params=_params("arbitrary"),
        name="moba_sample_attn",
    )(page_table, slope_rows, proj3, proj3, cache3)


def _alibi_slopes(n_heads):
    return jnp.exp2(-8.0 * jnp.arange(1, n_heads + 1, dtype=F32) / n_heads)


def _nsa_layer(xp, xs, bp, bs, cache, win_state, page_table, w_in, w_out, pe, w1, w2, g, b, alpha):
    d = xp.shape[1]
    tp, ts = xp.shape[0] // bp, xs.shape[0] // bs
    gd = NSA_KV_HEADS * HEAD_DIM
    qd = NSA_KV_HEADS * NSA_GROUP * HEAD_DIM
    n_in = w_in.shape[1]
    n_pad = -(-n_in // LANE) * LANE
    w_in_bf = jnp.pad(w_in, ((0, 0), (0, n_pad - n_in))).astype(BF16)
    w1_bf, w2_bf, w_out_bf = w1.astype(BF16), w2.astype(BF16), w_out.astype(BF16)
    slopes = _alibi_slopes(NSA_KV_HEADS * NSA_GROUP)
    n_pages = page_table.shape[1]
    past = n_pages * PAGE_SIZE
    assert tp % LANE == 0 and past % (2 * CMP_BLOCK) == 0 and ts <= CMP_BLOCK

    proj_p, kvb_p = _proj(xp, w_in_bf, qd, qd + 6 * gd)
    proj_s, _ = _proj(xs, w_in_bf, qd, qd + 6 * gd)
    proj_p3 = proj_p.reshape(bp, tp, n_pad)
    proj_s3 = proj_s.reshape(bs, ts, n_pad)
    kcvc_p = _compress_prompt(proj_p3, qd, pe, w1_bf, w2_bf)
    o_p = _nsa_prompt_attn(proj_p3, kvb_p.reshape(bp, tp, 6 * gd), kcvc_p, slopes)
    cache3 = cache.reshape(cache.shape[0], PAGE_SIZE, 4 * gd)
    kcvc_s = _compress_sample(cache3, page_table, pe, w1_bf, w2_bf)
    wp = win_state.shape[1]
    o_s = _nsa_sample_attn(proj_s3, kcvc_s, win_state.reshape(bs, wp, 2 * gd), cache3, page_table, slopes)
    xp = _out_ln(o_p.reshape(bp * tp, qd), xp, w_out_bf, g, b, alpha)
    xs = _out_ln(o_s.reshape(bs * ts, qd), xs, w_out_bf, g, b, alpha)

    kv_shape = (4, NSA_KV_HEADS, HEAD_DIM)
    win_shape = (2, NSA_KV_HEADS, HEAD_DIM)
    kv_p = proj_p3[:, :, qd:qd + 4 * gd].reshape((bp, tp) + kv_shape)
    kv_s = proj_s3[:, :, qd:qd + 4 * gd].reshape((bs, ts) + kv_shape)
    win_p = proj_p3[:, tp - min(WINDOW, tp):, qd + 4 * gd:qd + 6 * gd].reshape((bp, min(WINDOW, tp)) + win_shape)
    win_s = jnp.concatenate([win_state, proj_s3[:, :, qd + 4 * gd:qd + 6 * gd].reshape((bs, ts) + win_shape)], axis=1)
    win_s = win_s[:, win_s.shape[1] - min(WINDOW, win_s.shape[1]):]
    return xp, xs, kv_p, kv_s, win_p, win_s


def _pool_layer(xp, xs, bp, bs, state, past, w, scale, g, b, alpha):
    d = xp.shape[1]
    tp, ts = xp.shape[0] // bp, xs.shape[0] // bs
    assert POOL_HALO % ts == 0 and tp % POOL_HALO == 0
    w_bf = w.astype(BF16)
    scale = scale.reshape(1, d)
    xs3 = xs.reshape(bs, ts, d)
    xe_s = jnp.concatenate([jnp.zeros((bs, POOL_HALO - POOL_PAST, d), F32), state, xs3], axis=1)
    pool_p = xp.reshape(bp, tp, d)[:, tp - POOL_PAST:]
    pool_s = xe_s[:, xe_s.shape[1] - POOL_PAST:]
    xp = _pool_ln_prompt(xp, tp, w_bf, scale, g, b, alpha)
    xs = _pool_ln_sample(xe_s, ts, past, w_bf, scale, g, b, alpha)
    return xp, xs, pool_p, pool_s


def _moba_layer(xp, xs, bp, bs, cache, page_table, w_in, w_out, g, b, alpha):
    tp, ts = xp.shape[0] // bp, xs.shape[0] // bs
    gd = MOBA_KV_HEADS * HEAD_DIM
    qd = MOBA_KV_HEADS * MOBA_GROUP * HEAD_DIM
    n_in = w_in.shape[1]
    w_in_bf, w_out_bf = w_in.astype(BF16), w_out.astype(BF16)
    slopes = _alibi_slopes(MOBA_KV_HEADS * MOBA_GROUP)
    past = page_table.shape[1] * PAGE_SIZE
    assert tp % MOBA_BLOCK == 0 and past % MOBA_BLOCK == 0 and ts <= MOBA_BLOCK

    proj_p, kvb_p = _proj(xp, w_in_bf, qd, qd + 2 * gd)
    proj_s, _ = _proj(xs, w_in_bf, qd, qd + 2 * gd)
    proj_p3 = proj_p.reshape(bp, tp, n_in)
    proj_s3 = proj_s.reshape(bs, ts, n_in)
    o_p = _moba_prompt_attn(proj_p3, kvb_p.reshape(bp, tp, 2 * gd), slopes)
    cache3 = cache.reshape(cache.shape[0], PAGE_SIZE, 2 * gd)
    o_s = _moba_sample_attn(proj_s3, cache3, page_table, slopes)
    xp = _out_ln(o_p.reshape(bp * tp, qd), xp, w_out_bf, g, b, alpha)
    xs = _out_ln(o_s.reshape(bs * ts, qd), xs, w_out_bf, g, b, alpha)
    kv_shape = (2, MOBA_KV_HEADS, HEAD_DIM)
    kv_p = proj_p3[:, :, qd:].reshape((bp, tp) + kv_shape)
    kv_s = proj_s3[:, :, qd:].reshape((bs, ts) + kv_shape)
    return xp, xs, kv_p, kv_s


def kernel(x_prompt, x_sample, cache_nsa_kv, state_nsa_win, state_pool, cache_moba, page_table, ln_g, ln_b, mlp_w1, mlp_w2, nsa_w_in, nsa_w_out, nsa_cmp_pe, nsa_cmp_w1, nsa_cmp_w2, pool_w, pool_scale, moba_w_in, moba_w_out):
    bp, tp, d = x_prompt.shape
    bs, ts, _ = x_sample.shape
    depth = ln_g.shape[0]
    alpha = (2 * depth) ** 0.25
    past = page_table.shape[1] * PAGE_SIZE
    xp = x_prompt.reshape(bp * tp, d)
    xs = x_sample.reshape(bs * ts, d)
    outs = {k: [] for k in ("nsa_kv_p", "nsa_kv_s", "nsa_win_p", "nsa_win_s", "pool_p", "pool_s", "moba_p", "moba_s")}
    for i in range(depth):
        kind, j = i % N_MIXERS, i // N_MIXERS
        g0, b0 = ln_g[i, 0].reshape(1, d), ln_b[i, 0].reshape(1, d)
        g1, b1 = ln_g[i, 1].reshape(1, d), ln_b[i, 1].reshape(1, d)
        if kind == 0:
            xp, xs, kv_p, kv_s, win_p, win_s = _nsa_layer(
                xp, xs, bp, bs, cache_nsa_kv[j], state_nsa_win[j], page_table, nsa_w_in[j], nsa_w_out[j],
                nsa_cmp_pe[j], nsa_cmp_w1[j], nsa_cmp_w2[j], g0, b0, alpha)
            outs["nsa_kv_p"].append(kv_p); outs["nsa_kv_s"].append(kv_s)
            outs["nsa_win_p"].append(win_p); outs["nsa_win_s"].append(win_s)
        elif kind == 1:
            xp, xs, pool_p, pool_s = _pool_layer(xp, xs, bp, bs, state_pool[j], past, pool_w[j], pool_scale[j],
                                                 g0, b0, alpha)
            outs["pool_p"].append(pool_p); outs["pool_s"].append(pool_s)
        else:
            xp, xs, kv_p, kv_s = _moba_layer(xp, xs, bp, bs, cache_moba[j], page_table, moba_w_in[j],
                                             moba_w_out[j], g0, b0, alpha)
            outs["moba_p"].append(kv_p); outs["moba_s"].append(kv_s)
        w1_bf, w2_bf = mlp_w1[i].astype(BF16), mlp_w2[i].astype(BF16)
        xp = _mlp_ln(xp, w1_bf, w2_bf, g1, b1, alpha)
        xs = _mlp_ln(xs, w1_bf, w2_bf, g1, b1, alpha)
    return (xp.reshape(bp, tp, d), xs.reshape(bs, ts, d),
            jnp.stack(outs["nsa_kv_p"]), jnp.stack(outs["nsa_kv_s"]),
            jnp.stack(outs["nsa_win_p"]), jnp.stack(outs["nsa_win_s"]),
            jnp.stack(outs["pool_p"]), jnp.stack(outs["pool_s"]),
            jnp.stack(outs["moba_p"]), jnp.stack(outs["moba_s"]))
```

```python
import functools

import jax
import jax.numpy as jnp
from jax import lax
from jax.experimental import pallas as pl
from jax.experimental.pallas import tpu as pltpu

F32 = jnp.float32
BF16 = jnp.bfloat16

HEAD_DIM = 64
PAGE_SIZE = 128
N_MIXERS = 3
NSA_KV_HEADS = 4
NSA_GROUP = 4
CMP_BLOCK = 32
SEL_BLOCK = 64
N_SEL = 16
WINDOW = 512
CMP_HIDDEN = 2 * HEAD_DIM
POOL_WINDOWS = (2, 4, 8, 16)
POOL_PAST = max(POOL_WINDOWS) - 1
POOL_HALO = 16
MOBA_KV_HEADS = 4
MOBA_GROUP = 4
MOBA_BLOCK = 256
MOBA_TOPK = 3
LN_EPS = 1e-5
NEG_INF = -1e30
QK_SCALE = HEAD_DIM ** -0.5
LOG2E = 1.4426950408889634

LANE = 128
VMEM_LIMIT = 56 * 1024 * 1024
NT_DIMS = (((1,), (1,)), ((), ()))
HIGHEST = lax.Precision.HIGHEST

AUG = LANE
AUG_OH = 16
AUG_AL = 8
POS_DIGIT = 16
NSA_PAST_TILE = 8 * SEL_BLOCK


def _params(*sem):
    return pltpu.CompilerParams(dimension_semantics=sem, vmem_limit_bytes=VMEM_LIMIT)


def _layer_norm(y, g, b):
    mu = jnp.mean(y, axis=-1, keepdims=True)
    d = y - mu
    var = jnp.mean(d * d, axis=-1, keepdims=True)
    return d * lax.rsqrt(var + LN_EPS) * g + b


def _row_tile(m, want):
    t = min(m, want)
    while m % t:
        t //= 2
    return t


def _proj_body(x_ref, w_ref, o_ref, ob_ref, *, lo, hi):
    acc = jnp.dot(x_ref[...].astype(BF16), w_ref[...], preferred_element_type=F32)
    o_ref[...] = acc
    ob_ref[...] = acc[:, lo:hi].astype(BF16)


def _proj(x, w_bf, lo, hi):
    m, d = x.shape
    n = w_bf.shape[1]
    tm = _row_tile(m, 512)
    return pl.pallas_call(
        functools.partial(_proj_body, lo=lo, hi=hi),
        grid=(m // tm,),
        in_specs=[pl.BlockSpec((tm, d), lambda i: (i, 0)),
                  pl.BlockSpec((d, n), lambda i: (0, 0))],
        out_specs=[pl.BlockSpec((tm, n), lambda i: (i, 0)),
                   pl.BlockSpec((tm, hi - lo), lambda i: (i, 0))],
        out_shape=[jax.ShapeDtypeStruct((m, n), F32),
                   jax.ShapeDtypeStruct((m, hi - lo), BF16)],
        compiler_params=_params("parallel"),
        name="proj",
    )(x, w_bf)


def _out_ln_body(o_ref, x_ref, w_ref, g_ref, b_ref, y_ref, *, alpha):
    h = jnp.dot(o_ref[...].astype(BF16), w_ref[...], preferred_element_type=F32)
    y_ref[...] = _layer_norm(alpha * x_ref[...] + h, g_ref[...], b_ref[...])


def _out_ln(o, x, w_bf, g, b, alpha):
    m, d = x.shape
    k = o.shape[1]
    tm = _row_tile(m, 512)
    return pl.pallas_call(
        functools.partial(_out_ln_body, alpha=alpha),
        grid=(m // tm,),
        in_specs=[pl.BlockSpec((tm, k), lambda i: (i, 0)),
                  pl.BlockSpec((tm, d), lambda i: (i, 0)),
                  pl.BlockSpec((k, d), lambda i: (0, 0)),
                  pl.BlockSpec((1, d), lambda i: (0, 0)),
                  pl.BlockSpec((1, d), lambda i: (0, 0))],
        out_specs=pl.BlockSpec((tm, d), lambda i: (i, 0)),
        out_shape=jax.ShapeDtypeStruct((m, d), F32),
        compiler_params=_params("parallel"),
        name="out_ln",
    )(o, x, w_bf, g, b)


def _mlp_ln_body(x_ref, w1_ref, w2_ref, g_ref, b_ref, y_ref, acc_ref, *, alpha):
    j = pl.program_id(1)

    @pl.when(j == 0)
    def _():
        acc_ref[...] = jnp.zeros_like(acc_ref)

    h = jnp.dot(x_ref[...].astype(BF16), w1_ref[...], preferred_element_type=F32)
    h = jnp.maximum(h, 0.0)
    acc_ref[...] += jnp.dot((h * h).astype(BF16), w2_ref[...], preferred_element_type=F32)

    @pl.when(j == pl.num_programs(1) - 1)
    def _():
        y_ref[...] = _layer_norm(alpha * x_ref[...] + acc_ref[...], g_ref[...], b_ref[...])


def _mlp_ln(x, w1_bf, w2_bf, g, b, alpha):
    m, d = x.shape
    f = w1_bf.shape[1]
    tm = _row_tile(m, 1024)
    tf = 512
    return pl.pallas_call(
        functools.partial(_mlp_ln_body, alpha=alpha),
        grid=(m // tm, f // tf),
        in_specs=[pl.BlockSpec((tm, d), lambda i, j: (i, 0)),
                  pl.BlockSpec((d, tf), lambda i, j: (0, j)),
                  pl.BlockSpec((tf, d), lambda i, j: (j, 0)),
                  pl.BlockSpec((1, d), lambda i, j: (0, 0)),
                  pl.BlockSpec((1, d), lambda i, j: (0, 0))],
        out_specs=pl.BlockSpec((tm, d), lambda i, j: (i, 0)),
        out_shape=jax.ShapeDtypeStruct((m, d), F32),
        scratch_shapes=[pltpu.VMEM((tm, d), F32)],
        compiler_params=_params("parallel", "arbitrary"),
        name="mlp_ln",
    )(x, w1_bf, w2_bf, g, b)


def _pool_ln_body(halo_ref, x_ref, w_ref, scale_ref, g_ref, b_ref, y_ref, *, alpha, pos0, tiles_per_seq):
    tm, d = x_ref.shape
    grp = d // len(POOL_WINDOWS)
    seq_tile = pl.program_id(0) % tiles_per_seq
    x = x_ref[...]
    halo = halo_ref[...]
    if pos0 == 0:
        halo = jnp.where(seq_tile == 0, 0.0, halo)
    xe = jnp.concatenate([halo, x], axis=0)
    n_avail = pos0 + seq_tile * tm + lax.broadcasted_iota(jnp.int32, (tm, 1), 0) + 1
    mixed = []
    for gi, w in enumerate(POOL_WINDOWS):
        s = xe[:, gi * grp:(gi + 1) * grp]
        step = 1
        while step < w:
            s = s + pltpu.roll(s, step, 0)
            step *= 2
        cnt = jnp.minimum(n_avail, w).astype(F32)
        pooled = s[POOL_HALO:] / cnt - x[:, gi * grp:(gi + 1) * grp]
        mixed.append(jnp.dot(pooled.astype(BF16), w_ref[gi], preferred_element_type=F32))
    h = jnp.concatenate(mixed, axis=-1) * scale_ref[...]
    y_ref[...] = _layer_norm(alpha * x + h, g_ref[...], b_ref[...])


def _pool_ln(rows, halo_spec, x_spec, n_tiles, tm, tiles_per_seq, pos0, w_bf, scale, g, b, alpha):
    d = rows.shape[-1]
    ngrp = len(POOL_WINDOWS)
    const = lambda i: (0, 0)
    return pl.pallas_call(
        functools.partial(_pool_ln_body, alpha=alpha, pos0=pos0, tiles_per_seq=tiles_per_seq),
        grid=(n_tiles,),
        in_specs=[halo_spec, x_spec,
                  pl.BlockSpec((ngrp, d // ngrp, d // ngrp), lambda i: (0, 0, 0)),
                  pl.BlockSpec((1, d), const), pl.BlockSpec((1, d), const), pl.BlockSpec((1, d), const)],
        out_specs=pl.BlockSpec((tm, d), lambda i: (i, 0)),
        out_shape=jax.ShapeDtypeStruct((n_tiles * tm, d), F32),
        compiler_params=_params("parallel"),
        name="pool_ln",
    )(rows, rows, w_bf, scale, g, b)


def _pool_ln_prompt(x2d, seq_len, w_bf, scale, g, b, alpha):
    m, d = x2d.shape
    tm = _row_tile(seq_len, 512)
    per = tm // POOL_HALO
    halo_spec = pl.BlockSpec((POOL_HALO, d), lambda i: (jnp.maximum(i * per - 1, 0), 0))
    x_spec = pl.BlockSpec((tm, d), lambda i: (i, 0))
    return _pool_ln(x2d, halo_spec, x_spec, m // tm, tm, seq_len // tm, 0, w_bf, scale, g, b, alpha)


def _pool_ln_sample(xe3, ts, pos0, w_bf, scale, g, b, alpha):
    bs, _, d = xe3.shape
    halo_spec = pl.BlockSpec((None, POOL_HALO, d), lambda i: (i, 0, 0))
    x_spec = pl.BlockSpec((None, ts, d), lambda i: (i, POOL_HALO // ts, 0))
    return _pool_ln(xe3, halo_spec, x_spec, bs, ts, 1, pos0, w_bf, scale, g, b, alpha)


def _topk_mask(vals, valid, k, axis):
    n = vals.shape[axis]
    idx = lax.broadcasted_iota(jnp.int32, vals.shape, axis).astype(F32)
    work = jnp.where(valid, vals, -jnp.inf)
    sel = jnp.zeros(vals.shape, F32)
    for _ in range(k):
        m = jnp.max(work, axis=axis, keepdims=True)
        first = jnp.min(jnp.where(work == m, idx, float(n)), axis=axis, keepdims=True)
        pick = idx == first
        sel = jnp.where(pick, 1.0, sel)
        work = jnp.where(pick, -jnp.inf, work)
    return jnp.where(valid, sel, 0.0)


def _cmp_positions(nc, axis):
    shape = (nc, 1) if axis == 0 else (1, nc)
    i = lax.broadcasted_iota(jnp.int32, shape, axis)
    half = nc // 2
    n = jnp.where(i < half, 2 * i, 2 * (i - half) + 1)
    c_end = n * CMP_BLOCK + (CMP_BLOCK - 1)
    c_mid = c_end.astype(F32) - 0.5 * (CMP_BLOCK - 1)
    return c_end, c_mid


def _softmax_init(rows, vdim):
    return (jnp.full((rows, 1), NEG_INF, F32), jnp.zeros((rows, 1), F32), jnp.zeros((rows, vdim), F32))


def _softmax_step(carry, qb, k, v, t, slope, kpos, mask_fn, kv_t):
    m, l, acc = carry
    if kv_t:
        s = jnp.dot(qb, k, preferred_element_type=F32)
    else:
        s = lax.dot_general(qb, k, NT_DIMS, preferred_element_type=F32)
    dist = t - kpos
    s = s - slope * jnp.abs(dist).astype(F32)
    mask = mask_fn(dist)
    s = jnp.where(mask, s, NEG_INF)
    m_new = jnp.maximum(m, jnp.max(s, axis=-1, keepdims=True))
    a = jnp.exp(m - m_new)
    p = jnp.where(mask, jnp.exp(s - m_new), 0.0).astype(BF16)
    l = a * l + jnp.sum(p.astype(F32), axis=-1, keepdims=True)
    if kv_t:
        pv = lax.dot_general(p, v, NT_DIMS, preferred_element_type=F32)
    else:
        pv = jnp.dot(p, v, preferred_element_type=F32)
    return m_new, l, a * acc + pv


def _softmax_done(carry):
    _, l, acc = carry
    return acc / jnp.where(l > 0.0, l, 1.0)


def _block_mask(sel_bf, key0, tk, blk_shift):
    nblk = sel_bf.shape[1]
    kblk = (key0 + lax.broadcasted_iota(jnp.int32, (nblk, tk), 1)) >> blk_shift
    expand = jnp.where(kblk == lax.broadcasted_iota(jnp.int32, (nblk, tk), 0), 1.0, 0.0).astype(BF16)
    return jnp.dot(sel_bf, expand, preferred_element_type=F32) > 0.5


def _cmp_attention(q, kc, vc, t, slope):
    nc = kc.shape[0]
    c_end, c_mid = _cmp_positions(nc, 1)
    s = lax.dot_general(q, kc, NT_DIMS, preferred_element_type=F32, precision=HIGHEST)
    s = s - slope * jnp.abs(t.astype(F32) - c_mid)
    mask = c_end <= t
    s = jnp.where(mask, s, NEG_INF)
    m = jnp.max(s, axis=-1, keepdims=True)
    p = jnp.where(mask, jnp.exp(s - m), 0.0)
    l = jnp.sum(p, axis=-1, keepdims=True)
    p = p / jnp.where(l > 0.0, l, 1.0)
    o = jnp.dot(p.astype(BF16), vc.astype(BF16), preferred_element_type=F32)
    return p, o


def _tile_lanes(x, n):
    return jnp.concatenate([x] * n, axis=1)


def _group_q_t(q_t, g, r_n):
    hd = HEAD_DIM
    return jnp.concatenate([q_t[(g * r_n + r) * hd:(g * r_n + r + 1) * hd] for r in range(r_n)], axis=1)


def _aug_tail(q_t, alibi):
    pad = jnp.zeros((AUG - AUG_OH - HEAD_DIM - AUG_AL, q_t.shape[1]), F32)
    return jnp.concatenate([q_t, alibi, pad], axis=0).astype(BF16)


def _aug_rhs(not_sel, tail):
    pad_oh = AUG_OH - not_sel.shape[0]
    if pad_oh:
        not_sel = jnp.concatenate([not_sel, jnp.zeros((pad_oh, not_sel.shape[1]), F32)], axis=0)
    return jnp.concatenate([not_sel.astype(BF16), tail], axis=0)


def _soft_first(s, v_t):
    m = jnp.max(s, axis=0, keepdims=True)
    p = jnp.exp2(s - m)
    return m, jnp.sum(p, axis=0, keepdims=True), jnp.dot(v_t, p.astype(BF16), preferred_element_type=F32)


def _soft_more(carry, s, v_t):
    m, l, acc = carry
    m_new = jnp.maximum(m, jnp.max(s, axis=0, keepdims=True))
    a = jnp.exp2(m - m_new)
    p = jnp.exp2(s - m_new)
    return (m_new, a * l + jnp.sum(p, axis=0, keepdims=True),
            a * acc + jnp.dot(v_t, p.astype(BF16), preferred_element_type=F32))


def _soft_done(carry):
    _, l, acc = carry
    return acc / l


HEADS_PER_LANE_TILE = LANE // HEAD_DIM


def _compress_rows(src_ref, pe_ref, w1_ref, w2_ref, nc):
    n_h, hd, half = HEADS_PER_LANE_TILE, HEAD_DIM, nc // 2
    acc = jnp.zeros((n_h * nc, CMP_HIDDEN), F32)
    for r in range(CMP_BLOCK):
        even = src_ref[pl.ds(r, half, stride=2 * CMP_BLOCK), :]
        odd = src_ref[pl.ds(CMP_BLOCK + r, half, stride=2 * CMP_BLOCK), :]
        rows = jnp.concatenate([even, odd], axis=0)
        rows = jnp.concatenate([rows[:, g * hd:(g + 1) * hd] for g in range(n_h)], axis=0)
        rows = rows + pe_ref[r:r + 1, :]
        acc = acc + jnp.dot(rows.astype(BF16), w1_ref[r * hd:(r + 1) * hd, :], preferred_element_type=F32)
    h = jax.nn.gelu(acc)
    out = jnp.dot(h.astype(BF16), w2_ref[...], preferred_element_type=F32)
    return jnp.concatenate([out[g * nc:(g + 1) * nc] for g in range(n_h)], axis=-1)


def _compress_prompt_body(src_ref, pe_ref, w1_ref, w2_ref, o_ref, *, nc):
    o_ref[...] = _compress_rows(src_ref, pe_ref, w1_ref, w2_ref, nc)


def _compress_prompt(proj3, col0, pe, w1_bf, w2_bf):
    b, t, _ = proj3.shape
    gd = NSA_KV_HEADS * HEAD_DIM
    nc = t // CMP_BLOCK
    tiles = gd // LANE
    return pl.pallas_call(
        functools.partial(_compress_prompt_body, nc=nc),
        grid=(2, b, tiles),
        in_specs=[pl.BlockSpec((None, t, LANE), lambda c, i, h: (i, 0, col0 // LANE + c * tiles + h)),
                  pl.BlockSpec((None, CMP_BLOCK, HEAD_DIM), lambda c, i, h: (c, 0, 0)),
                  pl.BlockSpec((None, CMP_BLOCK * HEAD_DIM, CMP_HIDDEN), lambda c, i, h: (c, 0, 0)),
                  pl.BlockSpec((None, CMP_HIDDEN, HEAD_DIM), lambda c, i, h: (c, 0, 0))],
        out_specs=pl.BlockSpec((None, None, nc, LANE), lambda c, i, h: (c, i, 0, h)),
        out_shape=jax.ShapeDtypeStruct((2, b, nc, gd), F32),
        compiler_params=_params("parallel", "parallel", "parallel"),
        name="nsa_compress_prompt",
    )(proj3, pe, w1_bf, w2_bf)


def _gather_pages(pt_ref, b, cache_hbm, layer, row0, nrows, buf, sem, n_pages):
    def copy(p):
        return pltpu.make_async_copy(cache_hbm.at[layer, pt_ref[b, p], pl.ds(row0, nrows), :],
                                     buf.at[:, pl.ds(p * PAGE_SIZE, PAGE_SIZE)], sem)

    def start():
        for p in range(n_pages):
            copy(p).start()

    def wait():
        for p in range(n_pages):
            copy(p).wait()

    return start, wait


def _compress_sample_body(pt_ref, cache_hbm, pe_ref, w1_ref, w2_ref, o_ref, buf, rowbuf, sem,
                          *, layer, n_pages, nc):
    gd = NSA_KV_HEADS * HEAD_DIM
    tiles = gd // LANE
    start, wait = _gather_pages(pt_ref, pl.program_id(0), cache_hbm, layer, 0, 2 * gd, buf, sem, n_pages)
    start()
    wait()

    def to_rows(p, carry):
        k0 = pl.multiple_of(p * PAGE_SIZE, PAGE_SIZE)
        for c in range(2 * tiles):
            rowbuf[c, pl.ds(k0, PAGE_SIZE), :] = buf[c * LANE:(c + 1) * LANE, pl.ds(k0, PAGE_SIZE)].T
        return carry

    lax.fori_loop(0, n_pages, to_rows, 0)
    for c in range(2):
        for h in range(tiles):
            o_ref[c, :, h * LANE:(h + 1) * LANE] = _compress_rows(
                rowbuf.at[c * tiles + h], pe_ref.at[c], w1_ref.at[c], w2_ref.at[c], nc)


def _compress_sample(cache_t, layer, page_table, pe, w1_bf, w2_bf):
    bs, n_pages = page_table.shape
    gd = NSA_KV_HEADS * HEAD_DIM
    past = n_pages * PAGE_SIZE
    nc = past // CMP_BLOCK
    return pl.pallas_call(
        functools.partial(_compress_sample_body, layer=layer, n_pages=n_pages, nc=nc),
        grid_spec=pltpu.PrefetchScalarGridSpec(
            num_scalar_prefetch=1,
            grid=(bs,),
            in_specs=[pl.BlockSpec(memory_space=pl.ANY),
                      pl.BlockSpec((2, CMP_BLOCK, HEAD_DIM), lambda i, pt: (0, 0, 0)),
                      pl.BlockSpec((2, CMP_BLOCK * HEAD_DIM, CMP_HIDDEN), lambda i, pt: (0, 0, 0)),
                      pl.BlockSpec((2, CMP_HIDDEN, HEAD_DIM), lambda i, pt: (0, 0, 0))],
            out_specs=pl.BlockSpec((None, 2, nc, gd), lambda i, pt: (i, 0, 0, 0)),
            scratch_shapes=[pltpu.VMEM((2 * gd, past), F32),
                            pltpu.VMEM((2 * gd // LANE, past, LANE), F32),
                            pltpu.SemaphoreType.DMA(())]),
        out_shape=jax.ShapeDtypeStruct((bs, 2, nc, gd), F32),
        compiler_params=_params("arbitrary"),
        name="nsa_compress_sample",
    )(page_table, cache_t, pe, w1_bf, w2_bf)


def _nsa_prompt_body(slope_ref, alibi_ref, q_ref, gate_ref, kc_ref, vc_ref, ks_ref, vs_ref, kw_ref, vw_ref,
                     o_ref, ns_sc, o_sc, *, tq, tk, wk):
    i = pl.program_id(1)
    g_n, r_n, hd = NSA_KV_HEADS, NSA_GROUP, HEAD_DIM
    n_h = g_n * r_n
    rows = r_n * tq
    nc = kc_ref.shape[0]
    ns = nc // 2
    t_q = i * tq + lax.broadcasted_iota(jnp.int32, (1, tq), 1)
    t_row = _tile_lanes(t_q, r_n)
    cur = t_q // SEL_BLOCK
    blk = lax.broadcasted_iota(jnp.int32, (ns, 1), 0)
    q_t = q_ref[...].T * (QK_SCALE * LOG2E)
    gate_t = jax.nn.sigmoid(gate_ref[...]).T
    vc_t = vc_ref[...].T
    c_end, c_mid = _cmp_positions(nc, 0)
    key_i = lax.broadcasted_iota(jnp.int32, (tq, 1), 0)
    causal = _tile_lanes(jnp.where(key_i <= lax.broadcasted_iota(jnp.int32, (1, tq), 1), 0.0, NEG_INF), r_n)
    key_d = pl.multiple_of(i * tq, tq)
    blk_d = pl.multiple_of((i * tq // SEL_BLOCK) // 8 * 8, 8)
    n_past = (i * tq + tk - 1) // tk
    w0 = pl.multiple_of(jnp.maximum((i + 1) * tq - wk, 0), tq)
    w_dist = t_q - (w0 + lax.broadcasted_iota(jnp.int32, (wk, 1), 0))
    in_win = _tile_lanes(jnp.where((w_dist >= 0) & (w_dist < WINDOW), 0.0, NEG_INF), r_n)
    no_sel = jnp.zeros((AUG_OH, rows), F32)

    for g in range(g_n):
        gsl = slice(g * hd, (g + 1) * hd)
        asl = slice(g * AUG, (g + 1) * AUG)
        qg = _group_q_t(q_t, g, r_n)
        slope = slope_ref[g]

        s = jnp.dot(kc_ref[:, gsl], qg, preferred_element_type=F32, precision=HIGHEST)
        s = s - slope * jnp.abs(t_row.astype(F32) - c_mid)
        valid = c_end <= t_row
        s = jnp.where(valid, s, NEG_INF)
        p = jnp.where(valid, jnp.exp2(s - jnp.max(s, axis=0, keepdims=True)), 0.0)
        l = jnp.sum(p, axis=0, keepdims=True)
        p = p / jnp.where(l > 0.0, l, 1.0)
        o_cmp = jnp.dot(vc_t[gsl].astype(BF16), p.astype(BF16), preferred_element_type=F32)
        imp = p[:, 0:tq]
        for r in range(1, r_n):
            imp = imp + p[:, r * tq:(r + 1) * tq]
        imp = imp[:ns] + imp[ns:]
        sel = _topk_mask(imp, blk < cur, min(N_SEL - 1, ns), 0)
        ns_sc[0] = _tile_lanes(1.0 - jnp.where(blk < i * tq // SEL_BLOCK, sel, 0.0), r_n)
        ns_sc[1] = _tile_lanes(1.0 - jnp.where(blk == cur, 1.0, sel), r_n)

        tail = _aug_tail(qg, alibi_ref[g])

        def rhs(which, blk0):
            return _aug_rhs(ns_sc[which, pl.ds(blk0, 8), :], tail)

        s = jnp.dot(ks_ref[pl.ds(key_d, tq), asl], rhs(1, blk_d), preferred_element_type=F32) + causal
        carry = _soft_first(s, vs_ref[gsl, pl.ds(key_d, tq)])

        def past(j, carry):
            key0 = pl.multiple_of(j * tk, tk)
            s = jnp.dot(ks_ref[pl.ds(key0, tk), asl], rhs(0, pl.multiple_of(j * 8, 8)),
                        preferred_element_type=F32)
            return _soft_more(carry, s, vs_ref[gsl, pl.ds(key0, tk)])

        o_slc = _soft_done(lax.fori_loop(0, n_past, past, carry))

        s = jnp.dot(kw_ref[pl.ds(w0, wk), asl], _aug_rhs(no_sel, tail), preferred_element_type=F32) + in_win
        o_win = _soft_done(_soft_first(s, vw_ref[gsl, pl.ds(w0, wk)]))

        for r in range(r_n):
            h = g * r_n + r
            ls = slice(r * tq, (r + 1) * tq)
            o_sc[h * hd:(h + 1) * hd, :] = (gate_t[h:h + 1] * o_cmp[:, ls]
                                            + gate_t[n_h + h:n_h + h + 1] * o_slc[:, ls]
                                            + gate_t[2 * n_h + h:2 * n_h + h + 1] * o_win[:, ls])
    o_ref[...] = o_sc[...].T


def _nsa_prompt_attn(proj3, k_slc, v_slc_t, k_win, v_win_t, kcvc, slope_rows, alibi_rows, tq):
    b, t, _ = proj3.shape
    g_n = NSA_KV_HEADS
    gd = g_n * HEAD_DIM
    qd = g_n * NSA_GROUP * HEAD_DIM
    nc = kcvc.shape[2]
    rows = NSA_GROUP * tq
    tk = NSA_PAST_TILE
    wk = WINDOW + tq
    assert t % tk == 0 and t >= wk and (nc // 2) % 8 == 0
    k_spec = pl.BlockSpec((None, t, g_n * AUG), lambda bi, i: (bi, 0, 0))
    v_spec = pl.BlockSpec((None, gd, t), lambda bi, i: (bi, 0, 0))
    return pl.pallas_call(
        functools.partial(_nsa_prompt_body, tq=tq, tk=tk, wk=wk),
        grid=(b, t // tq),
        in_specs=[pl.BlockSpec((g_n, 1, rows), lambda bi, i: (0, 0, 0)),
                  pl.BlockSpec((g_n, AUG_AL, rows), lambda bi, i: (0, 0, 0)),
                  pl.BlockSpec((None, tq, qd), lambda bi, i: (bi, i, 0)),
                  pl.BlockSpec((None, tq, LANE), lambda bi, i: (bi, i, (qd + 6 * gd) // LANE)),
                  pl.BlockSpec((None, None, nc, gd), lambda bi, i: (0, bi, 0, 0)),
                  pl.BlockSpec((None, None, nc, gd), lambda bi, i: (1, bi, 0, 0)),
                  k_spec, v_spec, k_spec, v_spec],
        out_specs=pl.BlockSpec((None, tq, qd), lambda bi, i: (bi, i, 0)),
        out_shape=jax.ShapeDtypeStruct((b, t, qd), F32),
        scratch_shapes=[pltpu.VMEM((2, nc // 2, rows), F32), pltpu.VMEM((qd, tq), F32)],
        compiler_params=_params("parallel", "arbitrary"),
        name="nsa_prompt_attn",
    )(slope_rows, alibi_rows, proj3, proj3, kcvc, kcvc, k_slc, v_slc_t, k_win, v_win_t)


def _block_diag_q(q_ref, g_n, r_n):
    hd = HEAD_DIM
    ts = q_ref.shape[0]
    out = []
    for g in range(g_n):
        qg = jnp.concatenate([q_ref[:, (g * r_n + r) * hd:(g * r_n + r + 1) * hd] for r in range(r_n)], axis=0)
        parts = [jnp.zeros((r_n * ts, hd), F32)] * g_n
        parts[g] = qg
        out.append(jnp.concatenate(parts, axis=-1))
    return jnp.concatenate(out, axis=0)


def _nsa_sample_body(pt_ref, slope_ref, q_ref, gate_ref, kcvc_ref, snew_ref, wnew_ref, win_ref, cache_hbm,
                     o_ref, buf, sem, *, layer, n_pages, tk):
    g_n, r_n, hd = NSA_KV_HEADS, NSA_GROUP, HEAD_DIM
    gd = g_n * hd
    ts = q_ref.shape[0]
    rows = g_n * r_n * ts
    past = n_pages * PAGE_SIZE
    start, wait = _gather_pages(pt_ref, pl.program_id(0), cache_hbm, layer, 2 * gd, 2 * gd, buf, sem, n_pages)
    start()

    qf = _block_diag_q(q_ref, g_n, r_n) * QK_SCALE
    qb = qf.astype(BF16)
    slope = slope_ref[...]
    t = past + lax.broadcasted_iota(jnp.int32, (rows, 1), 0) % ts
    new_pos = past + lax.broadcasted_iota(jnp.int32, (1, ts), 1)
    causal = lambda dist: dist >= 0

    nc = kcvc_ref.shape[1]
    ns = nc // 2
    p_cmp, o_cmp = _cmp_attention(qf, kcvc_ref[0], kcvc_ref[1], t, slope)
    imp = []
    for g in range(g_n):
        base = g * r_n * ts
        acc = p_cmp[base:base + ts]
        for r in range(1, r_n):
            acc = acc + p_cmp[base + r * ts:base + (r + 1) * ts]
        imp.append(acc)
    imp = jnp.concatenate(imp, axis=0)
    imp = imp[:, :ns] + imp[:, ns:]
    sel = _topk_mask(imp, jnp.full(imp.shape, True), min(N_SEL - 1, ns), 1).astype(BF16)
    sel = jnp.concatenate([sel[g * ts:(g + 1) * ts] for g in range(g_n) for _ in range(r_n)], axis=0)

    wp = win_ref.shape[1]
    win_pos = (past - wp) + lax.broadcasted_iota(jnp.int32, (1, wp), 1)
    in_win = lambda dist: (dist >= 0) & (dist < WINDOW)
    carry = _softmax_step(_softmax_init(rows, gd), qb, win_ref[:gd, :].astype(BF16),
                          win_ref[gd:, :].astype(BF16), t, slope, win_pos, in_win, True)
    carry = _softmax_step(carry, qb, wnew_ref[:, :gd].astype(BF16), wnew_ref[:, gd:].astype(BF16),
                          t, slope, new_pos, in_win, False)
    o_win = _softmax_done(carry)

    wait()

    def slc_step(j, carry):
        key0 = pl.multiple_of(j * tk, tk)
        kpos = key0 + lax.broadcasted_iota(jnp.int32, (1, tk), 1)
        in_sel = _block_mask(sel, key0, tk, SEL_BLOCK.bit_length() - 1)
        return _softmax_step(carry, qb, buf[:gd, pl.ds(key0, tk)].astype(BF16),
                             buf[gd:, pl.ds(key0, tk)].astype(BF16), t, slope, kpos, lambda dist: in_sel, True)

    carry = lax.fori_loop(0, past // tk, slc_step, _softmax_init(rows, gd))
    carry = _softmax_step(carry, qb, snew_ref[:, :gd].astype(BF16), snew_ref[:, gd:].astype(BF16),
                          t, slope, new_pos, causal, False)
    o_slc = _softmax_done(carry)

    gate = jax.nn.sigmoid(gate_ref[...])
    n_h = g_n * r_n
    for h in range(n_h):
        g = h // r_n
        rs = slice(h * ts, (h + 1) * ts)
        cs = slice(g * hd, (g + 1) * hd)
        o_ref[:, h * hd:(h + 1) * hd] = (gate[:, h:h + 1] * o_cmp[rs, cs]
                                         + gate[:, n_h + h:n_h + h + 1] * o_slc[rs, cs]
                                         + gate[:, 2 * n_h + h:2 * n_h + h + 1] * o_win[rs, cs])


def _nsa_sample_attn(proj3, kcvc, win_t, cache_t, layer, page_table, slopes, tk=512):
    bs, ts, _ = proj3.shape
    n_pages = page_table.shape[1]
    gd = NSA_KV_HEADS * HEAD_DIM
    qd = NSA_KV_HEADS * NSA_GROUP * HEAD_DIM
    nc = kcvc.shape[2]
    wp = win_t.shape[3]
    slope_rows = jnp.repeat(slopes, ts).reshape(-1, 1)
    return pl.pallas_call(
        functools.partial(_nsa_sample_body, layer=layer, n_pages=n_pages, tk=tk),
        grid_spec=pltpu.PrefetchScalarGridSpec(
            num_scalar_prefetch=1,
            grid=(bs,),
            in_specs=[pl.BlockSpec((slope_rows.shape[0], 1), lambda i, pt: (0, 0)),
                      pl.BlockSpec((None, ts, qd), lambda i, pt: (i, 0, 0)),
                      pl.BlockSpec((None, ts, LANE), lambda i, pt: (i, 0, (qd + 6 * gd) // LANE)),
                      pl.BlockSpec((None, 2, nc, gd), lambda i, pt: (i, 0, 0, 0)),
                      pl.BlockSpec((None, ts, 2 * gd), lambda i, pt: (i, 0, (qd + 2 * gd) // (2 * gd))),
                      pl.BlockSpec((None, ts, 2 * gd), lambda i, pt: (i, 0, (qd + 4 * gd) // (2 * gd))),
                      pl.BlockSpec((None, None, 2 * gd, wp), lambda i, pt: (layer, i, 0, 0)),
                      pl.BlockSpec(memory_space=pl.ANY)],
            out_specs=pl.BlockSpec((None, ts, qd), lambda i, pt: (i, 0, 0)),
            scratch_shapes=[pltpu.VMEM((2 * gd, n_pages * PAGE_SIZE), F32), pltpu.SemaphoreType.DMA(())]),
        out_shape=jax.ShapeDtypeStruct((bs, ts, qd), F32),
        compiler_params=_params("arbitrary"),
        name="nsa_sample_attn",
    )(page_table, slope_rows, proj3, proj3, kcvc, proj3, proj3, win_t, cache_t)


def _moba_prompt_body(alibi_ref, q_ref, kf_ref, k_ref, v_ref, o_ref, kmean_sc, o_sc, *, tq, tk):
    i = pl.program_id(1)
    g_n, r_n, hd = MOBA_KV_HEADS, MOBA_GROUP, HEAD_DIM
    rows = r_n * tq
    nb = kmean_sc.shape[0]

    @pl.when(i == 0)
    def _():
        kmean_sc[...] = jnp.concatenate(
            [jnp.mean(kf_ref[n * MOBA_BLOCK:(n + 1) * MOBA_BLOCK, :], axis=0, keepdims=True) for n in range(nb)],
            axis=0)

    t_q = i * tq + lax.broadcasted_iota(jnp.int32, (1, tq), 1)
    cur = i * tq // MOBA_BLOCK
    blk = lax.broadcasted_iota(jnp.int32, (nb, 1), 0)
    q_t = q_ref[...].T
    own0 = pl.multiple_of(cur * MOBA_BLOCK, MOBA_BLOCK)
    own_pos = own0 + lax.broadcasted_iota(jnp.int32, (MOBA_BLOCK, 1), 0)
    causal = _tile_lanes(jnp.where(own_pos <= t_q, 0.0, NEG_INF), r_n)
    n_past = (cur * MOBA_BLOCK + tk - 1) // tk
    no_sel = jnp.zeros((AUG_OH, rows), F32)

    for g in range(g_n):
        gsl = slice(g * hd, (g + 1) * hd)
        asl = slice(g * AUG, (g + 1) * AUG)
        qg = _group_q_t(q_t, g, r_n)
        gs = jnp.dot(kmean_sc[:, gsl], qg, preferred_element_type=F32, precision=HIGHEST)
        sel = _topk_mask(gs, blk < cur, min(MOBA_TOPK, nb), 0)
        tail = _aug_tail(qg * (QK_SCALE * LOG2E), alibi_ref[g])
        rhs_past = _aug_rhs(1.0 - sel, tail)

        s = jnp.dot(k_ref[pl.ds(own0, MOBA_BLOCK), asl], _aug_rhs(no_sel, tail),
                    preferred_element_type=F32) + causal
        carry = _soft_first(s, v_ref[gsl, pl.ds(own0, MOBA_BLOCK)])

        def past(j, carry):
            key0 = pl.multiple_of(j * tk, tk)
            s = jnp.dot(k_ref[pl.ds(key0, tk), asl], rhs_past, preferred_element_type=F32)
            return _soft_more(carry, s, v_ref[gsl, pl.ds(key0, tk)])

        o = _soft_done(lax.fori_loop(0, n_past, past, carry))
        for r in range(r_n):
            h = g * r_n + r
            o_sc[h * hd:(h + 1) * hd, :] = o[:, r * tq:(r + 1) * tq]
    o_ref[...] = o_sc[...].T


def _moba_prompt_attn(proj3, k_aug, v_t, alibi_rows, tq, tk=512):
    b, t, _ = proj3.shape
    g_n = MOBA_KV_HEADS
    gd = g_n * HEAD_DIM
    qd = g_n * MOBA_GROUP * HEAD_DIM
    rows = MOBA_GROUP * tq
    assert t % tk == 0 and tk % MOBA_BLOCK == 0 and MOBA_BLOCK % tq == 0 and t // MOBA_BLOCK <= AUG_OH
    return pl.pallas_call(
        functools.partial(_moba_prompt_body, tq=tq, tk=tk),
        grid=(b, t // tq),
        in_specs=[pl.BlockSpec((g_n, AUG_AL, rows), lambda bi, i: (0, 0, 0)),
                  pl.BlockSpec((None, tq, qd), lambda bi, i: (bi, i, 0)),
                  pl.BlockSpec((None, t, gd), lambda bi, i: (bi, 0, qd // gd)),
                  pl.BlockSpec((None, t, g_n * AUG), lambda bi, i: (bi, 0, 0)),
                  pl.BlockSpec((None, gd, t), lambda bi, i: (bi, 0, 0))],
        out_specs=pl.BlockSpec((None, tq, qd), lambda bi, i: (bi, i, 0)),
        out_shape=jax.ShapeDtypeStruct((b, t, qd), F32),
        scratch_shapes=[pltpu.VMEM((t // MOBA_BLOCK, gd), F32), pltpu.VMEM((qd, tq), F32)],
        compiler_params=_params("parallel", "arbitrary"),
        name="moba_prompt_attn",
    )(alibi_rows, proj3, proj3, k_aug, v_t)


def _moba_sample_body(pt_ref, slope_ref, q_ref, new_ref, cache_hbm, o_ref, buf, sem, *, layer, n_pages, tk):
    g_n, r_n, hd = MOBA_KV_HEADS, MOBA_GROUP, HEAD_DIM
    gd = g_n * hd
    ts = q_ref.shape[0]
    rows = g_n * r_n * ts
    past = n_pages * PAGE_SIZE
    nb = past // MOBA_BLOCK
    start, wait = _gather_pages(pt_ref, pl.program_id(0), cache_hbm, layer, 0, 2 * gd, buf, sem, n_pages)
    start()
    qf = _block_diag_q(q_ref, g_n, r_n)
    qb = (qf * QK_SCALE).astype(BF16)
    slope = slope_ref[...]
    t = past + lax.broadcasted_iota(jnp.int32, (rows, 1), 0) % ts
    new_pos = past + lax.broadcasted_iota(jnp.int32, (1, ts), 1)
    wait()

    blk = lax.broadcasted_iota(jnp.int32, (gd, nb), 1)
    kmean_t = jnp.zeros((gd, nb), F32)
    for n in range(nb):
        col = jnp.mean(buf[:gd, n * MOBA_BLOCK:(n + 1) * MOBA_BLOCK], axis=1, keepdims=True)
        kmean_t = jnp.where(blk == n, col, kmean_t)
    gs = jnp.dot(qf, kmean_t, preferred_element_type=F32, precision=HIGHEST)
    sel = _topk_mask(gs, jnp.full(gs.shape, True), min(MOBA_TOPK, nb), 1).astype(BF16)

    def step(j, carry):
        key0 = pl.multiple_of(j * tk, tk)
        kpos = key0 + lax.broadcasted_iota(jnp.int32, (1, tk), 1)
        in_sel = _block_mask(sel, key0, tk, MOBA_BLOCK.bit_length() - 1)
        return _softmax_step(carry, qb, buf[:gd, pl.ds(key0, tk)].astype(BF16),
                             buf[gd:, pl.ds(key0, tk)].astype(BF16), t, slope, kpos, lambda dist: in_sel, True)

    carry = lax.fori_loop(0, past // tk, step, _softmax_init(rows, gd))
    carry = _softmax_step(carry, qb, new_ref[:, :gd].astype(BF16), new_ref[:, gd:].astype(BF16),
                          t, slope, new_pos, lambda dist: dist >= 0, False)
    o = _softmax_done(carry)
    for h in range(g_n * r_n):
        g = h // r_n
        o_ref[:, h * hd:(h + 1) * hd] = o[h * ts:(h + 1) * ts, g * hd:(g + 1) * hd]


def _moba_sample_attn(proj3, cache_t, layer, page_table, slopes, tk=512):
    bs, ts, _ = proj3.shape
    n_pages = page_table.shape[1]
    gd = MOBA_KV_HEADS * HEAD_DIM
    qd = MOBA_KV_HEADS * MOBA_GROUP * HEAD_DIM
    slope_rows = jnp.repeat(slopes, ts).reshape(-1, 1)
    return pl.pallas_call(
        functools.partial(_moba_sample_body, layer=layer, n_pages=n_pages, tk=tk),
        grid_spec=pltpu.PrefetchScalarGridSpec(
            num_scalar_prefetch=1,
            grid=(bs,),
            in_specs=[pl.BlockSpec((slope_rows.shape[0], 1), lambda i, pt: (0, 0)),
                      pl.BlockSpec((None, ts, qd), lambda i, pt: (i, 0, 0)),
                      pl.BlockSpec((None, ts, 2 * gd), lambda i, pt: (i, 0, qd // (2 * gd))),
                      pl.BlockSpec(memory_space=pl.ANY)],
            out_specs=pl.BlockSpec((None, ts, qd), lambda i, pt: (i, 0, 0)),
            scratch_shapes=[pltpu.VMEM((2 * gd, n_pages * PAGE_SIZE), F32), pltpu.SemaphoreType.DMA(())]),
        out_shape=jax.ShapeDtypeStruct((bs, ts, qd), F32),
        compiler_params=_params("arbitrary"),
        name="moba_sample_attn",
    )(page_table, slope_rows, proj3, proj3, cache_t)


PROMPT_Q_TILE = 128


def _alibi_slopes(n_heads):
    return jnp.exp2(-8.0 * jnp.arange(1, n_heads + 1, dtype=F32) / n_heads)


def _alibi_query_rows(slopes, g_n, r_n, tq):
    s2 = slopes * LOG2E
    d0 = s2.astype(BF16).astype(F32)
    d1 = (s2 - d0).astype(BF16).astype(F32)
    d2 = (s2 - d0 - d1).astype(BF16).astype(F32)
    zero = jnp.zeros_like(s2)
    digits = jnp.stack([d0, d1, d2, POS_DIGIT * d0, POS_DIGIT * d1, POS_DIGIT * d2, zero, zero], axis=0)
    lanes = lambda a: jnp.repeat(a.reshape(-1, g_n, r_n), tq, axis=2).transpose(1, 0, 2)
    return lanes(s2[None]), lanes(digits)


def _augment_keys(k, blk_len, n_onehot):
    b, t, g, hd = k.shape
    assert t <= POS_DIGIT * 256
    pos = jnp.arange(t, dtype=jnp.int32)
    slot = (pos // blk_len) % n_onehot if n_onehot else jnp.full((t,), -1, jnp.int32)
    onehot = jnp.where(slot[:, None] == jnp.arange(AUG_OH)[None, :], NEG_INF, 0.0).astype(BF16)
    lo, hi = (pos % POS_DIGIT).astype(BF16), (pos // POS_DIGIT).astype(BF16)
    zero = jnp.zeros((t,), BF16)
    tail = jnp.stack([lo, lo, lo, hi, hi, hi, zero, zero] + [zero] * (AUG - AUG_OH - hd - AUG_AL), axis=1)
    wide = lambda a: jnp.broadcast_to(a[None, :, None, :], (b, t, g, a.shape[1]))
    return jnp.concatenate([wide(onehot), k, wide(tail)], axis=-1).reshape(b, t, g * AUG)


def _positions_minor(a, n_lead):
    nd = a.ndim
    a = jnp.transpose(a, tuple(range(n_lead)) + tuple(range(n_lead + 1, nd)) + (n_lead,))
    return a.reshape(a.shape[:n_lead] + (-1, a.shape[-1]))


def _nsa_layer(xp, xs, bp, bs, cache_t, win_t, layer, win_state, page_table, w_in, w_out, pe, w1, w2, g, b,
               alpha):
    tp, ts = xp.shape[0] // bp, xs.shape[0] // bs
    g_n, hd = NSA_KV_HEADS, HEAD_DIM
    gd = g_n * hd
    qd = g_n * NSA_GROUP * hd
    n_in = w_in.shape[1]
    n_pad = -(-n_in // LANE) * LANE
    w_in_bf = jnp.pad(w_in, ((0, 0), (0, n_pad - n_in))).astype(BF16)
    w1_bf, w2_bf, w_out_bf = w1.astype(BF16), w2.astype(BF16), w_out.astype(BF16)
    slopes = _alibi_slopes(g_n * NSA_GROUP)
    past = page_table.shape[1] * PAGE_SIZE
    assert PAGE_SIZE == LANE and past % (2 * CMP_BLOCK) == 0 and ts <= CMP_BLOCK

    proj_p, kvb_p = _proj(xp, w_in_bf, qd, qd + 6 * gd)
    proj_s, _ = _proj(xs, w_in_bf, qd, qd + 6 * gd)
    proj_p3 = proj_p.reshape(bp, tp, n_pad)
    proj_s3 = proj_s.reshape(bs, ts, n_pad)
    kcvc_p = _compress_prompt(proj_p3, qd, pe, w1_bf, w2_bf)
    kvb = kvb_p.reshape(bp, tp, 6, g_n, hd)
    v_t = lambda c: kvb[:, :, c].reshape(bp, tp, gd).transpose(0, 2, 1)
    slope_rows, alibi_rows = _alibi_query_rows(slopes, g_n, NSA_GROUP, PROMPT_Q_TILE)
    o_p = _nsa_prompt_attn(proj_p3, _augment_keys(kvb[:, :, 2], SEL_BLOCK, NSA_PAST_TILE // SEL_BLOCK), v_t(3),
                           _augment_keys(kvb[:, :, 4], SEL_BLOCK, 0), v_t(5), kcvc_p, slope_rows, alibi_rows,
                           PROMPT_Q_TILE)
    kcvc_s = _compress_sample(cache_t, layer, page_table, pe, w1_bf, w2_bf)
    o_s = _nsa_sample_attn(proj_s3, kcvc_s, win_t, cache_t, layer, page_table, slopes)
    xp = _out_ln(o_p.reshape(bp * tp, qd), xp, w_out_bf, g, b, alpha)
    xs = _out_ln(o_s.reshape(bs * ts, qd), xs, w_out_bf, g, b, alpha)

    kv_shape = (4, g_n, hd)
    win_shape = (2, g_n, hd)
    kv_p = proj_p3[:, :, qd:qd + 4 * gd].reshape((bp, tp) + kv_shape)
    kv_s = proj_s3[:, :, qd:qd + 4 * gd].reshape((bs, ts) + kv_shape)
    win_p = proj_p3[:, tp - min(WINDOW, tp):, qd + 4 * gd:qd + 6 * gd].reshape((bp, min(WINDOW, tp)) + win_shape)
    win_s = jnp.concatenate([win_state, proj_s3[:, :, qd + 4 * gd:qd + 6 * gd].reshape((bs, ts) + win_shape)], axis=1)
    win_s = win_s[:, win_s.shape[1] - min(WINDOW, win_s.shape[1]):]
    return xp, xs, kv_p, kv_s, win_p, win_s


def _pool_layer(xp, xs, bp, bs, state, past, w, scale, g, b, alpha):
    d = xp.shape[1]
    tp, ts = xp.shape[0] // bp, xs.shape[0] // bs
    assert POOL_HALO % ts == 0 and tp % POOL_HALO == 0
    w_bf = w.astype(BF16)
    scale = scale.reshape(1, d)
    xs3 = xs.reshape(bs, ts, d)
    xe_s = jnp.concatenate([jnp.zeros((bs, POOL_HALO - POOL_PAST, d), F32), state, xs3], axis=1)
    pool_p = xp.reshape(bp, tp, d)[:, tp - POOL_PAST:]
    pool_s = xe_s[:, xe_s.shape[1] - POOL_PAST:]
    xp = _pool_ln_prompt(xp, tp, w_bf, scale, g, b, alpha)
    xs = _pool_ln_sample(xe_s, ts, past, w_bf, scale, g, b, alpha)
    return xp, xs, pool_p, pool_s


def _moba_layer(xp, xs, bp, bs, cache_t, layer, page_table, w_in, w_out, g, b, alpha):
    tp, ts = xp.shape[0] // bp, xs.shape[0] // bs
    g_n, hd = MOBA_KV_HEADS, HEAD_DIM
    gd = g_n * hd
    qd = g_n * MOBA_GROUP * hd
    n_in = w_in.shape[1]
    w_in_bf, w_out_bf = w_in.astype(BF16), w_out.astype(BF16)
    slopes = _alibi_slopes(g_n * MOBA_GROUP)
    past = page_table.shape[1] * PAGE_SIZE
    assert PAGE_SIZE == LANE and past % MOBA_BLOCK == 0 and ts <= MOBA_BLOCK

    proj_p, kvb_p = _proj(xp, w_in_bf, qd, qd + 2 * gd)
    proj_s, _ = _proj(xs, w_in_bf, qd, qd + 2 * gd)
    proj_p3 = proj_p.reshape(bp, tp, n_in)
    proj_s3 = proj_s.reshape(bs, ts, n_in)
    kvb = kvb_p.reshape(bp, tp, 2, g_n, hd)
    _, alibi_rows = _alibi_query_rows(slopes, g_n, MOBA_GROUP, PROMPT_Q_TILE)
    o_p = _moba_prompt_attn(proj_p3, _augment_keys(kvb[:, :, 0], MOBA_BLOCK, AUG_OH),
                            kvb[:, :, 1].reshape(bp, tp, gd).transpose(0, 2, 1), alibi_rows, PROMPT_Q_TILE)
    o_s = _moba_sample_attn(proj_s3, cache_t, layer, page_table, slopes)
    xp = _out_ln(o_p.reshape(bp * tp, qd), xp, w_out_bf, g, b, alpha)
    xs = _out_ln(o_s.reshape(bs * ts, qd), xs, w_out_bf, g, b, alpha)
    kv_shape = (2, g_n, hd)
    kv_p = proj_p3[:, :, qd:].reshape((bp, tp) + kv_shape)
    kv_s = proj_s3[:, :, qd:].reshape((bs, ts) + kv_shape)
    return xp, xs, kv_p, kv_s


def kernel(x_prompt, x_sample, cache_nsa_kv, state_nsa_win, state_pool, cache_moba, page_table, ln_g, ln_b, mlp_w1, mlp_w2, nsa_w_in, nsa_w_out, nsa_cmp_pe, nsa_cmp_w1, nsa_cmp_w2, pool_w, pool_scale, moba_w_in, moba_w_out):
    bp, tp, d = x_prompt.shape
    bs, ts, _ = x_sample.shape
    depth = ln_g.shape[0]
    alpha = (2 * depth) ** 0.25
    past = page_table.shape[1] * PAGE_SIZE
    xp = x_prompt.reshape(bp * tp, d)
    xs = x_sample.reshape(bs * ts, d)
    nsa_cache_t = _positions_minor(cache_nsa_kv, 2)
    nsa_win_t = _positions_minor(state_nsa_win, 2)
    moba_cache_t = _positions_minor(cache_moba, 2)
    outs = {k: [] for k in ("nsa_kv_p", "nsa_kv_s", "nsa_win_p", "nsa_win_s", "pool_p", "pool_s", "moba_p", "moba_s")}
    for i in range(depth):
        kind, j = i % N_MIXERS, i // N_MIXERS
        g0, b0 = ln_g[i, 0].reshape(1, d), ln_b[i, 0].reshape(1, d)
        g1, b1 = ln_g[i, 1].reshape(1, d), ln_b[i, 1].reshape(1, d)
        if kind == 0:
            xp, xs, kv_p, kv_s, win_p, win_s = _nsa_layer(
                xp, xs, bp, bs, nsa_cache_t, nsa_win_t, j, state_nsa_win[j], page_table, nsa_w_in[j], nsa_w_out[j],
                nsa_cmp_pe[j], nsa_cmp_w1[j], nsa_cmp_w2[j], g0, b0, alpha)
            outs["nsa_kv_p"].append(kv_p); outs["nsa_kv_s"].append(kv_s)
            outs["nsa_win_p"].append(win_p); outs["nsa_win_s"].append(win_s)
        elif kind == 1:
            xp, xs, pool_p, pool_s = _pool_layer(xp, xs, bp, bs, state_pool[j], past, pool_w[j], pool_scale[j],
                                                 g0, b0, alpha)
            outs["pool_p"].append(pool_p); outs["pool_s"].append(pool_s)
        else:
            xp, xs, kv_p, kv_s = _moba_layer(xp, xs, bp, bs, moba_cache_t, j, page_table, moba_w_in[j],
                                             moba_w_out[j], g0, b0, alpha)
            outs["moba_p"].append(kv_p); outs["moba_s"].append(kv_s)
        w1_bf, w2_bf = mlp_w1[i].astype(BF16), mlp_w2[i].astype(BF16)
        xp = _mlp_ln(xp, w1_bf, w2_bf, g1, b1, alpha)
        xs = _mlp_ln(xs, w1_bf, w2_bf, g1, b1, alpha)
    return (xp.reshape(bp, tp, d), xs.reshape(bs, ts, d),
            jnp.stack(outs["nsa_kv_p"]), jnp.stack(outs["nsa_kv_s"]),
            jnp.stack(outs["nsa_win_p"]), jnp.stack(outs["nsa_win_s"]),
            jnp.stack(outs["pool_p"]), jnp.stack(outs["pool_s"]),
            jnp.stack(outs["moba_p"]), jnp.stack(outs["moba_s"]))
```

```python
import functools

import jax
import jax.numpy as jnp
import numpy as np
from jax import lax
from jax.experimental import pallas as pl
from jax.experimental.pallas import tpu as pltpu

F32 = jnp.float32
BF16 = jnp.bfloat16

HEAD_DIM = 64
PAGE_SIZE = 128
N_MIXERS = 3
NSA_KV_HEADS = 4
NSA_GROUP = 4
CMP_BLOCK = 32
SEL_BLOCK = 64
N_SEL = 16
WINDOW = 512
CMP_HIDDEN = 2 * HEAD_DIM
POOL_WINDOWS = (2, 4, 8, 16)
POOL_PAST = max(POOL_WINDOWS) - 1
POOL_HALO = 16
MOBA_KV_HEADS = 4
MOBA_GROUP = 4
MOBA_BLOCK = 256
MOBA_TOPK = 3
LN_EPS = 1e-5
NEG_INF = -1e30
QK_SCALE = HEAD_DIM ** -0.5
LOG2E = 1.4426950408889634

LANE = 128
VMEM_LIMIT = 56 * 1024 * 1024
NT_DIMS = (((1,), (1,)), ((), ()))
HIGHEST = lax.Precision.HIGHEST

AUG = LANE
AUG_OH = 16
AUG_AL = 8
POS_DIGIT = 16
NSA_PAST_TILE = 8 * SEL_BLOCK
PAST_SPLIT = 2


def _params(*sem):
    return pltpu.CompilerParams(dimension_semantics=sem, vmem_limit_bytes=VMEM_LIMIT)


def _layer_norm(y, g, b):
    mu = jnp.mean(y, axis=-1, keepdims=True)
    d = y - mu
    var = jnp.mean(d * d, axis=-1, keepdims=True)
    return d * lax.rsqrt(var + LN_EPS) * g + b


def _row_tile(m, want):
    t = min(m, want)
    while m % t:
        t //= 2
    return t


def _proj_body(x_ref, w_ref, o_ref, ob_ref, *, lo, hi):
    acc = jnp.dot(x_ref[...].astype(BF16), w_ref[...], preferred_element_type=F32)
    o_ref[...] = acc
    ob_ref[...] = acc[:, lo:hi].astype(BF16)


def _proj(x, w_bf, lo, hi):
    m, d = x.shape
    n = w_bf.shape[1]
    tm = _row_tile(m, 512)
    return pl.pallas_call(
        functools.partial(_proj_body, lo=lo, hi=hi),
        grid=(m // tm,),
        in_specs=[pl.BlockSpec((tm, d), lambda i: (i, 0)),
                  pl.BlockSpec((d, n), lambda i: (0, 0))],
        out_specs=[pl.BlockSpec((tm, n), lambda i: (i, 0)),
                   pl.BlockSpec((tm, hi - lo), lambda i: (i, 0))],
        out_shape=[jax.ShapeDtypeStruct((m, n), F32),
                   jax.ShapeDtypeStruct((m, hi - lo), BF16)],
        compiler_params=_params("parallel"),
        name="proj",
    )(x, w_bf)


def _out_ln_body(o_ref, x_ref, w_ref, g_ref, b_ref, y_ref, *, alpha):
    h = jnp.dot(o_ref[...].astype(BF16), w_ref[...], preferred_element_type=F32)
    y_ref[...] = _layer_norm(alpha * x_ref[...] + h, g_ref[...], b_ref[...])


def _out_ln(o, x, w_bf, g, b, alpha):
    m, d = x.shape
    k = o.shape[1]
    tm = _row_tile(m, 512)
    return pl.pallas_call(
        functools.partial(_out_ln_body, alpha=alpha),
        grid=(m // tm,),
        in_specs=[pl.BlockSpec((tm, k), lambda i: (i, 0)),
                  pl.BlockSpec((tm, d), lambda i: (i, 0)),
                  pl.BlockSpec((k, d), lambda i: (0, 0)),
                  pl.BlockSpec((1, d), lambda i: (0, 0)),
                  pl.BlockSpec((1, d), lambda i: (0, 0))],
        out_specs=pl.BlockSpec((tm, d), lambda i: (i, 0)),
        out_shape=jax.ShapeDtypeStruct((m, d), F32),
        compiler_params=_params("parallel"),
        name="out_ln",
    )(o, x, w_bf, g, b)


def _mlp_ln_body(x_ref, w1_ref, w2_ref, g_ref, b_ref, y_ref, acc_ref, *, alpha):
    j = pl.program_id(1)

    @pl.when(j == 0)
    def _():
        acc_ref[...] = jnp.zeros_like(acc_ref)

    h = jnp.dot(x_ref[...].astype(BF16), w1_ref[...], preferred_element_type=F32)
    h = jnp.maximum(h, 0.0)
    acc_ref[...] += jnp.dot((h * h).astype(BF16), w2_ref[...], preferred_element_type=F32)

    @pl.when(j == pl.num_programs(1) - 1)
    def _():
        y_ref[...] = _layer_norm(alpha * x_ref[...] + acc_ref[...], g_ref[...], b_ref[...])


def _mlp_ln(x, w1_bf, w2_bf, g, b, alpha):
    m, d = x.shape
    f = w1_bf.shape[1]
    tm = _row_tile(m, 1024)
    tf = 512
    return pl.pallas_call(
        functools.partial(_mlp_ln_body, alpha=alpha),
        grid=(m // tm, f // tf),
        in_specs=[pl.BlockSpec((tm, d), lambda i, j: (i, 0)),
                  pl.BlockSpec((d, tf), lambda i, j: (0, j)),
                  pl.BlockSpec((tf, d), lambda i, j: (j, 0)),
                  pl.BlockSpec((1, d), lambda i, j: (0, 0)),
                  pl.BlockSpec((1, d), lambda i, j: (0, 0))],
        out_specs=pl.BlockSpec((tm, d), lambda i, j: (i, 0)),
        out_shape=jax.ShapeDtypeStruct((m, d), F32),
        scratch_shapes=[pltpu.VMEM((tm, d), F32)],
        compiler_params=_params("parallel", "arbitrary"),
        name="mlp_ln",
    )(x, w1_bf, w2_bf, g, b)


def _pool_ln_body(halo_ref, x_ref, w_ref, scale_ref, g_ref, b_ref, y_ref, *, alpha, pos0, tiles_per_seq):
    tm, d = x_ref.shape
    grp = d // len(POOL_WINDOWS)
    seq_tile = pl.program_id(0) % tiles_per_seq
    x = x_ref[...]
    halo = halo_ref[...]
    if pos0 == 0:
        halo = jnp.where(seq_tile == 0, 0.0, halo)
    xe = jnp.concatenate([halo, x], axis=0)
    n_avail = pos0 + seq_tile * tm + lax.broadcasted_iota(jnp.int32, (tm, 1), 0) + 1
    mixed = []
    for gi, w in enumerate(POOL_WINDOWS):
        s = xe[:, gi * grp:(gi + 1) * grp]
        step = 1
        while step < w:
            s = s + pltpu.roll(s, step, 0)
            step *= 2
        cnt = jnp.minimum(n_avail, w).astype(F32)
        pooled = s[POOL_HALO:] / cnt - x[:, gi * grp:(gi + 1) * grp]
        mixed.append(jnp.dot(pooled.astype(BF16), w_ref[gi], preferred_element_type=F32))
    h = jnp.concatenate(mixed, axis=-1) * scale_ref[...]
    y_ref[...] = _layer_norm(alpha * x + h, g_ref[...], b_ref[...])


def _pool_ln(rows, halo_spec, x_spec, n_tiles, tm, tiles_per_seq, pos0, w_bf, scale, g, b, alpha):
    d = rows.shape[-1]
    ngrp = len(POOL_WINDOWS)
    const = lambda i: (0, 0)
    return pl.pallas_call(
        functools.partial(_pool_ln_body, alpha=alpha, pos0=pos0, tiles_per_seq=tiles_per_seq),
        grid=(n_tiles,),
        in_specs=[halo_spec, x_spec,
                  pl.BlockSpec((ngrp, d // ngrp, d // ngrp), lambda i: (0, 0, 0)),
                  pl.BlockSpec((1, d), const), pl.BlockSpec((1, d), const), pl.BlockSpec((1, d), const)],
        out_specs=pl.BlockSpec((tm, d), lambda i: (i, 0)),
        out_shape=jax.ShapeDtypeStruct((n_tiles * tm, d), F32),
        compiler_params=_params("parallel"),
        name="pool_ln",
    )(rows, rows, w_bf, scale, g, b)


def _pool_ln_prompt(x2d, seq_len, w_bf, scale, g, b, alpha):
    m, d = x2d.shape
    tm = _row_tile(seq_len, 512)
    per = tm // POOL_HALO
    halo_spec = pl.BlockSpec((POOL_HALO, d), lambda i: (jnp.maximum(i * per - 1, 0), 0))
    x_spec = pl.BlockSpec((tm, d), lambda i: (i, 0))
    return _pool_ln(x2d, halo_spec, x_spec, m // tm, tm, seq_len // tm, 0, w_bf, scale, g, b, alpha)


def _pool_ln_sample(xe3, ts, pos0, w_bf, scale, g, b, alpha):
    bs, _, d = xe3.shape
    halo_spec = pl.BlockSpec((None, POOL_HALO, d), lambda i: (i, 0, 0))
    x_spec = pl.BlockSpec((None, ts, d), lambda i: (i, POOL_HALO // ts, 0))
    return _pool_ln(xe3, halo_spec, x_spec, bs, ts, 1, pos0, w_bf, scale, g, b, alpha)


def _topk_mask(vals, valid, k, axis):
    n = vals.shape[axis]
    idx = lax.broadcasted_iota(jnp.int32, vals.shape, axis).astype(F32)
    work = jnp.where(valid, vals, -jnp.inf)
    sel = jnp.zeros(vals.shape, F32)
    for _ in range(k):
        m = jnp.max(work, axis=axis, keepdims=True)
        first = jnp.min(jnp.where(work == m, idx, float(n)), axis=axis, keepdims=True)
        pick = idx == first
        sel = jnp.where(pick, 1.0, sel)
        work = jnp.where(pick, -jnp.inf, work)
    return jnp.where(valid, sel, 0.0)


def _cmp_positions(nc, axis):
    shape = (nc, 1) if axis == 0 else (1, nc)
    i = lax.broadcasted_iota(jnp.int32, shape, axis)
    half = nc // 2
    n = jnp.where(i < half, 2 * i, 2 * (i - half) + 1)
    c_end = n * CMP_BLOCK + (CMP_BLOCK - 1)
    c_mid = c_end.astype(F32) - 0.5 * (CMP_BLOCK - 1)
    return c_end, c_mid


def _softmax_init(rows, vdim):
    return (jnp.full((rows, 1), NEG_INF, F32), jnp.zeros((rows, 1), F32), jnp.zeros((rows, vdim), F32))


def _softmax_step(carry, qb, k, v, t, slope, kpos, mask_fn, kv_t):
    m, l, acc = carry
    if kv_t:
        s = jnp.dot(qb, k, preferred_element_type=F32)
    else:
        s = lax.dot_general(qb, k, NT_DIMS, preferred_element_type=F32)
    dist = t - kpos
    s = s - slope * jnp.abs(dist).astype(F32)
    mask = mask_fn(dist)
    s = jnp.where(mask, s, NEG_INF)
    m_new = jnp.maximum(m, jnp.max(s, axis=-1, keepdims=True))
    a = jnp.exp(m - m_new)
    p = jnp.where(mask, jnp.exp(s - m_new), 0.0).astype(BF16)
    l = a * l + jnp.sum(p.astype(F32), axis=-1, keepdims=True)
    if kv_t:
        pv = lax.dot_general(p, v, NT_DIMS, preferred_element_type=F32)
    else:
        pv = jnp.dot(p, v, preferred_element_type=F32)
    return m_new, l, a * acc + pv


def _softmax_done(carry):
    _, l, acc = carry
    return acc / jnp.where(l > 0.0, l, 1.0)


def _block_mask(sel_bf, key0, tk, blk_shift):
    nblk = sel_bf.shape[1]
    kblk = (key0 + lax.broadcasted_iota(jnp.int32, (nblk, tk), 1)) >> blk_shift
    expand = jnp.where(kblk == lax.broadcasted_iota(jnp.int32, (nblk, tk), 0), 1.0, 0.0).astype(BF16)
    return jnp.dot(sel_bf, expand, preferred_element_type=F32) > 0.5


def _cmp_attention(q, kc, vc, t, slope):
    nc = kc.shape[0]
    c_end, c_mid = _cmp_positions(nc, 1)
    s = lax.dot_general(q, kc, NT_DIMS, preferred_element_type=F32, precision=HIGHEST)
    s = s - slope * jnp.abs(t.astype(F32) - c_mid)
    mask = c_end <= t
    s = jnp.where(mask, s, NEG_INF)
    m = jnp.max(s, axis=-1, keepdims=True)
    p = jnp.where(mask, jnp.exp(s - m), 0.0)
    l = jnp.sum(p, axis=-1, keepdims=True)
    p = p / jnp.where(l > 0.0, l, 1.0)
    o = jnp.dot(p.astype(BF16), vc.astype(BF16), preferred_element_type=F32)
    return p, o


def _tile_lanes(x, n):
    return jnp.concatenate([x] * n, axis=1)


def _group_q_t(q_t, g, r_n):
    hd = HEAD_DIM
    return jnp.concatenate([q_t[(g * r_n + r) * hd:(g * r_n + r + 1) * hd] for r in range(r_n)], axis=1)


def _aug_tail(q_t, alibi):
    pad = jnp.zeros((AUG - AUG_OH - HEAD_DIM - AUG_AL, q_t.shape[1]), F32)
    return jnp.concatenate([q_t, alibi, pad], axis=0).astype(BF16)


def _aug_rhs(not_sel, tail):
    pad_oh = AUG_OH - not_sel.shape[0]
    if pad_oh:
        not_sel = jnp.concatenate([not_sel, jnp.zeros((pad_oh, not_sel.shape[1]), F32)], axis=0)
    return jnp.concatenate([not_sel.astype(BF16), tail], axis=0)


def _soft_first(s, v_t):
    m = jnp.max(s, axis=0, keepdims=True)
    p = jnp.exp2(s - m)
    return m, jnp.sum(p, axis=0, keepdims=True), jnp.dot(v_t, p.astype(BF16), preferred_element_type=F32)


def _soft_more(carry, k_ref, v_ref, ksl, vsl, key0, tk, rhs):
    m, l, acc = carry
    part = tk // PAST_SPLIT
    keys = [pl.ds(pl.multiple_of(key0 + u * part, part), part) for u in range(PAST_SPLIT)]
    s = [jnp.dot(k_ref[kk, ksl], rhs, preferred_element_type=F32) for kk in keys]
    m_new = m
    for su in s:
        m_new = jnp.maximum(m_new, jnp.max(su, axis=0, keepdims=True))
    a = jnp.exp2(m - m_new)
    l, acc = a * l, a * acc
    for su, kk in zip(s, keys):
        p = jnp.exp2(su - m_new)
        l = l + jnp.sum(p, axis=0, keepdims=True)
        acc = acc + jnp.dot(v_ref[vsl, kk], p.astype(BF16), preferred_element_type=F32)
    return m_new, l, acc


def _soft_done(carry):
    _, l, acc = carry
    return acc / l


HEADS_PER_LANE_TILE = LANE // HEAD_DIM


def _compress_rows(src_ref, pe_ref, w1_ref, w2_ref, nc):
    half = nc // 2
    acc = jnp.zeros((nc, HEADS_PER_LANE_TILE * CMP_HIDDEN), F32)
    for r in range(CMP_BLOCK):
        even = src_ref[pl.ds(r, half, stride=2 * CMP_BLOCK), :]
        odd = src_ref[pl.ds(CMP_BLOCK + r, half, stride=2 * CMP_BLOCK), :]
        rows = jnp.concatenate([even, odd], axis=0) + pe_ref[r:r + 1, :]
        acc = acc + jnp.dot(rows.astype(BF16), w1_ref[r], preferred_element_type=F32)
    return jnp.dot(jax.nn.gelu(acc).astype(BF16), w2_ref[...], preferred_element_type=F32)


def _compress_weights(pe, w1, w2):
    n_h = HEADS_PER_LANE_TILE
    eye = jnp.eye(n_h, dtype=w1.dtype)
    w1r = w1.reshape(2, CMP_BLOCK, HEAD_DIM, CMP_HIDDEN)
    w1_bd = jnp.einsum('ab,crdh->cradbh', eye, w1r).reshape(2, CMP_BLOCK, n_h * HEAD_DIM, n_h * CMP_HIDDEN)
    w2_bd = jnp.einsum('ab,chd->cahbd', eye, w2).reshape(2, n_h * CMP_HIDDEN, n_h * HEAD_DIM)
    return jnp.tile(pe, (1, 1, n_h)), w1_bd.astype(BF16), w2_bd.astype(BF16)


def _compress_prompt_body(src_ref, pe_ref, w1_ref, w2_ref, o_ref, *, nc):
    o_ref[...] = _compress_rows(src_ref, pe_ref, w1_ref, w2_ref, nc)


def _compress_prompt(proj3, col0, pe, w1_bf, w2_bf):
    b, t, _ = proj3.shape
    gd = NSA_KV_HEADS * HEAD_DIM
    nc = t // CMP_BLOCK
    tiles = gd // LANE
    return pl.pallas_call(
        functools.partial(_compress_prompt_body, nc=nc),
        grid=(2, b, tiles),
        in_specs=[pl.BlockSpec((None, t, LANE), lambda c, i, h: (i, 0, col0 // LANE + c * tiles + h)),
                  pl.BlockSpec((None,) + pe.shape[1:], lambda c, i, h: (c, 0, 0)),
                  pl.BlockSpec((None,) + w1_bf.shape[1:], lambda c, i, h: (c, 0, 0, 0)),
                  pl.BlockSpec((None,) + w2_bf.shape[1:], lambda c, i, h: (c, 0, 0))],
        out_specs=pl.BlockSpec((None, None, nc, LANE), lambda c, i, h: (c, i, 0, h)),
        out_shape=jax.ShapeDtypeStruct((2, b, nc, gd), F32),
        compiler_params=_params("parallel", "parallel", "parallel"),
        name="nsa_compress_prompt",
    )(proj3, pe, w1_bf, w2_bf)


def _gather_pages(pt_ref, b, cache_hbm, layer, row0, nrows, buf, sem, n_pages):
    def copy(p):
        return pltpu.make_async_copy(cache_hbm.at[layer, pt_ref[b, p], pl.ds(row0, nrows), :], buf.at[p], sem)

    def start():
        for p in range(n_pages):
            copy(p).start()

    def wait():
        for p in range(n_pages):
            copy(p).wait()

    return start, wait


def _compress_sample_body(pt_ref, cache_hbm, pe_ref, w1_ref, w2_ref, o_ref, buf, rowbuf, sem,
                          *, layer, n_pages, nc):
    gd = NSA_KV_HEADS * HEAD_DIM
    tiles = gd // LANE
    b = pl.program_id(0)
    gather = lambda seq: _gather_pages(pt_ref, seq, cache_hbm, layer, 0, 2 * gd, buf, sem, n_pages)

    @pl.when(b == 0)
    def _():
        gather(b)[0]()

    gather(b)[1]()

    def to_rows(p, carry):
        k0 = pl.multiple_of(p * PAGE_SIZE, PAGE_SIZE)
        for c in range(2 * tiles):
            rowbuf[c, pl.ds(k0, PAGE_SIZE), :] = buf[p, c * LANE:(c + 1) * LANE, :].T
        return carry

    lax.fori_loop(0, n_pages, to_rows, 0)

    @pl.when(b + 1 < pl.num_programs(0))
    def _():
        gather(b + 1)[0]()

    for c in range(2):
        for h in range(tiles):
            o_ref[c, :, h * LANE:(h + 1) * LANE] = _compress_rows(
                rowbuf.at[c * tiles + h], pe_ref.at[c], w1_ref.at[c], w2_ref.at[c], nc)


def _compress_sample(cache_t, layer, page_table, pe, w1_bf, w2_bf):
    bs, n_pages = page_table.shape
    gd = NSA_KV_HEADS * HEAD_DIM
    past = n_pages * PAGE_SIZE
    nc = past // CMP_BLOCK
    return pl.pallas_call(
        functools.partial(_compress_sample_body, layer=layer, n_pages=n_pages, nc=nc),
        grid_spec=pltpu.PrefetchScalarGridSpec(
            num_scalar_prefetch=1,
            grid=(bs,),
            in_specs=[pl.BlockSpec(memory_space=pl.ANY),
                      pl.BlockSpec(pe.shape, lambda i, pt: (0, 0, 0)),
                      pl.BlockSpec(w1_bf.shape, lambda i, pt: (0, 0, 0, 0)),
                      pl.BlockSpec(w2_bf.shape, lambda i, pt: (0, 0, 0))],
            out_specs=pl.BlockSpec((None, 2, nc, gd), lambda i, pt: (i, 0, 0, 0)),
            scratch_shapes=[pltpu.VMEM((n_pages, 2 * gd, PAGE_SIZE), F32),
                            pltpu.VMEM((2 * gd // LANE, past, LANE), F32),
                            pltpu.SemaphoreType.DMA(())]),
        out_shape=jax.ShapeDtypeStruct((bs, 2, nc, gd), F32),
        compiler_params=_params("arbitrary"),
        name="nsa_compress_sample",
    )(page_table, cache_t, pe, w1_bf, w2_bf)


def _nsa_prompt_body(slope_ref, alibi_ref, q_ref, gate_ref, kc_ref, vc_ref, ks_ref, vs_ref, kw_ref, vw_ref,
                     o_ref, ns_sc, own_sc, tail_sc, m_sc, l_sc, acc_sc, o_sc, *, tq, tk, wk):
    i = pl.program_id(1)
    g_n, r_n, hd = NSA_KV_HEADS, NSA_GROUP, HEAD_DIM
    n_h = g_n * r_n
    rows = r_n * tq
    nc = kc_ref.shape[0]
    ns = nc // 2
    t_q = i * tq + lax.broadcasted_iota(jnp.int32, (1, tq), 1)
    t_row = _tile_lanes(t_q, r_n)
    cur = t_q // SEL_BLOCK
    blk = lax.broadcasted_iota(jnp.int32, (ns, 1), 0)
    q_t = q_ref[...].T * (QK_SCALE * LOG2E)
    gate_t = jax.nn.sigmoid(gate_ref[...]).T
    vc_t = vc_ref[...].T
    c_end, c_mid = _cmp_positions(nc, 0)
    key_i = lax.broadcasted_iota(jnp.int32, (tq, 1), 0)
    causal = _tile_lanes(jnp.where(key_i <= lax.broadcasted_iota(jnp.int32, (1, tq), 1), 0.0, NEG_INF), r_n)
    key_d = pl.multiple_of(i * tq, tq)
    blk_d = pl.multiple_of((i * tq // SEL_BLOCK) // 8 * 8, 8)
    n_past = (i * tq + tk - 1) // tk
    w0 = pl.multiple_of(jnp.maximum((i + 1) * tq - wk, 0), tq)
    w_dist = t_q - (w0 + lax.broadcasted_iota(jnp.int32, (wk, 1), 0))
    in_win = _tile_lanes(jnp.where((w_dist >= 0) & (w_dist < WINDOW), 0.0, NEG_INF), r_n)
    no_sel = jnp.zeros((AUG_OH, rows), F32)

    for g in range(g_n):
        gsl = slice(g * hd, (g + 1) * hd)
        asl = slice(g * AUG, (g + 1) * AUG)
        qg = _group_q_t(q_t, g, r_n)
        slope = slope_ref[g]

        s = jnp.dot(kc_ref[:, gsl], qg, preferred_element_type=F32, precision=HIGHEST)
        s = s - slope * jnp.abs(t_row.astype(F32) - c_mid)
        valid = c_end <= t_row
        s = jnp.where(valid, s, NEG_INF)
        p = jnp.where(valid, jnp.exp2(s - jnp.max(s, axis=0, keepdims=True)), 0.0)
        l = jnp.sum(p, axis=0, keepdims=True)
        p = p / jnp.where(l > 0.0, l, 1.0)
        o_cmp = jnp.dot(vc_t[gsl].astype(BF16), p.astype(BF16), preferred_element_type=F32)
        imp = p[:, 0:tq]
        for r in range(1, r_n):
            imp = imp + p[:, r * tq:(r + 1) * tq]
        imp = imp[:ns] + imp[ns:]
        sel = _topk_mask(imp, blk < cur, min(N_SEL - 1, ns), 0)
        ns_sc[g] = _tile_lanes(1.0 - jnp.where(blk < i * tq // SEL_BLOCK, sel, 0.0), r_n)
        own_sc[g] = _tile_lanes(1.0 - jnp.where(blk == cur, 1.0, sel), r_n)

        tail = _aug_tail(qg, alibi_ref[g])
        tail_sc[g] = tail

        s = jnp.dot(ks_ref[pl.ds(key_d, tq), asl], _aug_rhs(own_sc[g, pl.ds(blk_d, 8), :], tail),
                    preferred_element_type=F32) + causal
        m_sc[g], l_sc[g], acc_sc[g] = _soft_first(s, vs_ref[gsl, pl.ds(key_d, tq)])

        s = jnp.dot(kw_ref[pl.ds(w0, wk), asl], _aug_rhs(no_sel, tail), preferred_element_type=F32) + in_win
        o_win = _soft_done(_soft_first(s, vw_ref[gsl, pl.ds(w0, wk)]))

        for r in range(r_n):
            h = g * r_n + r
            ls = slice(r * tq, (r + 1) * tq)
            o_sc[h * hd:(h + 1) * hd, :] = (gate_t[h:h + 1] * o_cmp[:, ls]
                                            + gate_t[2 * n_h + h:2 * n_h + h + 1] * o_win[:, ls])

    def past(j, carry):
        key0 = pl.multiple_of(j * tk, tk)
        blk0 = pl.multiple_of(j * 8, 8)
        for g in range(g_n):
            m_sc[g], l_sc[g], acc_sc[g] = _soft_more(
                (m_sc[g], l_sc[g], acc_sc[g]), ks_ref, vs_ref, slice(g * AUG, (g + 1) * AUG),
                slice(g * hd, (g + 1) * hd), key0, tk, _aug_rhs(ns_sc[g, pl.ds(blk0, 8), :], tail_sc[g]))
        return carry

    lax.fori_loop(0, n_past, past, 0)

    for g in range(g_n):
        o_slc = acc_sc[g] / l_sc[g]
        for r in range(r_n):
            h = g * r_n + r
            o_sc[h * hd:(h + 1) * hd, :] += gate_t[n_h + h:n_h + h + 1] * o_slc[:, r * tq:(r + 1) * tq]
    o_ref[...] = o_sc[...].T


def _nsa_prompt_attn(proj3, k_slc, v_slc_t, k_win, v_win_t, kcvc, slope_rows, alibi_rows, tq):
    b, t, _ = proj3.shape
    g_n = NSA_KV_HEADS
    gd = g_n * HEAD_DIM
    qd = g_n * NSA_GROUP * HEAD_DIM
    nc = kcvc.shape[2]
    rows = NSA_GROUP * tq
    tk = NSA_PAST_TILE
    wk = WINDOW + tq
    assert t % tk == 0 and t >= wk and (nc // 2) % 8 == 0
    k_spec = pl.BlockSpec((None, t, g_n * AUG), lambda bi, i: (bi, 0, 0))
    v_spec = pl.BlockSpec((None, gd, t), lambda bi, i: (bi, 0, 0))
    return pl.pallas_call(
        functools.partial(_nsa_prompt_body, tq=tq, tk=tk, wk=wk),
        grid=(b, t // tq),
        in_specs=[pl.BlockSpec((g_n, 1, rows), lambda bi, i: (0, 0, 0)),
                  pl.BlockSpec((g_n, AUG_AL, rows), lambda bi, i: (0, 0, 0)),
                  pl.BlockSpec((None, tq, qd), lambda bi, i: (bi, i, 0)),
                  pl.BlockSpec((None, tq, LANE), lambda bi, i: (bi, i, (qd + 6 * gd) // LANE)),
                  pl.BlockSpec((None, None, nc, gd), lambda bi, i: (0, bi, 0, 0)),
                  pl.BlockSpec((None, None, nc, gd), lambda bi, i: (1, bi, 0, 0)),
                  k_spec, v_spec, k_spec, v_spec],
        out_specs=pl.BlockSpec((None, tq, qd), lambda bi, i: (bi, i, 0)),
        out_shape=jax.ShapeDtypeStruct((b, t, qd), F32),
        scratch_shapes=[pltpu.VMEM((g_n, nc // 2, rows), F32), pltpu.VMEM((g_n, nc // 2, rows), F32),
                        pltpu.VMEM((g_n, AUG - AUG_OH, rows), BF16),
                        pltpu.VMEM((g_n, 1, rows), F32), pltpu.VMEM((g_n, 1, rows), F32),
                        pltpu.VMEM((g_n, HEAD_DIM, rows), F32), pltpu.VMEM((qd, tq), F32)],
        compiler_params=_params("parallel", "arbitrary"),
        name="nsa_prompt_attn",
    )(slope_rows, alibi_rows, proj3, proj3, kcvc, kcvc, k_slc, v_slc_t, k_win, v_win_t)


def _page_tile(buf, slot, j, tk, lo, hi):
    pages = tk // PAGE_SIZE
    return jnp.concatenate([buf[slot, j * pages + u, lo:hi, :] for u in range(pages)], axis=1).astype(BF16)


def _block_diag_q(q_ref, g_n, r_n):
    hd = HEAD_DIM
    ts = q_ref.shape[0]
    out = []
    for g in range(g_n):
        qg = jnp.concatenate([q_ref[:, (g * r_n + r) * hd:(g * r_n + r + 1) * hd] for r in range(r_n)], axis=0)
        parts = [jnp.zeros((r_n * ts, hd), F32)] * g_n
        parts[g] = qg
        out.append(jnp.concatenate(parts, axis=-1))
    return jnp.concatenate(out, axis=0)


def _nsa_sample_body(pt_ref, slope_ref, q_ref, gate_ref, kcvc_ref, snew_ref, wnew_ref, win_ref, cache_hbm,
                     o_ref, buf, sem, *, layer, n_pages, tk):
    g_n, r_n, hd = NSA_KV_HEADS, NSA_GROUP, HEAD_DIM
    gd = g_n * hd
    ts = q_ref.shape[0]
    rows = g_n * r_n * ts
    past = n_pages * PAGE_SIZE
    b = pl.program_id(0)
    slot = b % 2
    gather = lambda seq, sl: _gather_pages(pt_ref, seq, cache_hbm, layer, 2 * gd, 2 * gd, buf.at[sl],
                                           sem.at[sl], n_pages)

    @pl.when(b == 0)
    def _():
        gather(b, slot)[0]()

    @pl.when(b + 1 < pl.num_programs(0))
    def _():
        gather(b + 1, 1 - slot)[0]()

    qf = _block_diag_q(q_ref, g_n, r_n) * QK_SCALE
    qb = qf.astype(BF16)
    slope = slope_ref[...]
    t = past + lax.broadcasted_iota(jnp.int32, (rows, 1), 0) % ts
    new_pos = past + lax.broadcasted_iota(jnp.int32, (1, ts), 1)
    causal = lambda dist: dist >= 0

    nc = kcvc_ref.shape[1]
    ns = nc // 2
    p_cmp, o_cmp = _cmp_attention(qf, kcvc_ref[0], kcvc_ref[1], t, slope)
    imp = []
    for g in range(g_n):
        base = g * r_n * ts
        acc = p_cmp[base:base + ts]
        for r in range(1, r_n):
            acc = acc + p_cmp[base + r * ts:base + (r + 1) * ts]
        imp.append(acc)
    imp = jnp.concatenate(imp, axis=0)
    imp = imp[:, :ns] + imp[:, ns:]
    sel = _topk_mask(imp, jnp.full(imp.shape, True), min(N_SEL - 1, ns), 1).astype(BF16)
    sel = jnp.concatenate([sel[g * ts:(g + 1) * ts] for g in range(g_n) for _ in range(r_n)], axis=0)

    wp = win_ref.shape[1]
    win_pos = (past - wp) + lax.broadcasted_iota(jnp.int32, (1, wp), 1)
    in_win = lambda dist: (dist >= 0) & (dist < WINDOW)
    carry = _softmax_step(_softmax_init(rows, gd), qb, win_ref[:gd, :].astype(BF16),
                          win_ref[gd:, :].astype(BF16), t, slope, win_pos, in_win, True)
    carry = _softmax_step(carry, qb, wnew_ref[:, :gd].astype(BF16), wnew_ref[:, gd:].astype(BF16),
                          t, slope, new_pos, in_win, False)
    o_win = _softmax_done(carry)

    gather(b, slot)[1]()

    def slc_step(j, carry):
        key0 = pl.multiple_of(j * tk, tk)
        kpos = key0 + lax.broadcasted_iota(jnp.int32, (1, tk), 1)
        in_sel = _block_mask(sel, key0, tk, SEL_BLOCK.bit_length() - 1)
        return _softmax_step(carry, qb, _page_tile(buf, slot, j, tk, 0, gd), _page_tile(buf, slot, j, tk, gd, 2 * gd),
                             t, slope, kpos, lambda dist: in_sel, True)

    carry = lax.fori_loop(0, past // tk, slc_step, _softmax_init(rows, gd))
    carry = _softmax_step(carry, qb, snew_ref[:, :gd].astype(BF16), snew_ref[:, gd:].astype(BF16),
                          t, slope, new_pos, causal, False)
    o_slc = _softmax_done(carry)

    gate = jax.nn.sigmoid(gate_ref[...])
    n_h = g_n * r_n
    for h in range(n_h):
        g = h // r_n
        rs = slice(h * ts, (h + 1) * ts)
        cs = slice(g * hd, (g + 1) * hd)
        o_ref[:, h * hd:(h + 1) * hd] = (gate[:, h:h + 1] * o_cmp[rs, cs]
                                         + gate[:, n_h + h:n_h + h + 1] * o_slc[rs, cs]
                                         + gate[:, 2 * n_h + h:2 * n_h + h + 1] * o_win[rs, cs])


def _nsa_sample_attn(proj3, kcvc, win_t, cache_t, layer, page_table, slopes, tk=512):
    bs, ts, _ = proj3.shape
    n_pages = page_table.shape[1]
    gd = NSA_KV_HEADS * HEAD_DIM
    qd = NSA_KV_HEADS * NSA_GROUP * HEAD_DIM
    nc = kcvc.shape[2]
    wp = win_t.shape[3]
    slope_rows = jnp.repeat(slopes, ts).reshape(-1, 1)
    return pl.pallas_call(
        functools.partial(_nsa_sample_body, layer=layer, n_pages=n_pages, tk=tk),
        grid_spec=pltpu.PrefetchScalarGridSpec(
            num_scalar_prefetch=1,
            grid=(bs,),
            in_specs=[pl.BlockSpec((slope_rows.shape[0], 1), lambda i, pt: (0, 0)),
                      pl.BlockSpec((None, ts, qd), lambda i, pt: (i, 0, 0)),
                      pl.BlockSpec((None, ts, LANE), lambda i, pt: (i, 0, (qd + 6 * gd) // LANE)),
                      pl.BlockSpec((None, 2, nc, gd), lambda i, pt: (i, 0, 0, 0)),
                      pl.BlockSpec((None, ts, 2 * gd), lambda i, pt: (i, 0, (qd + 2 * gd) // (2 * gd))),
                      pl.BlockSpec((None, ts, 2 * gd), lambda i, pt: (i, 0, (qd + 4 * gd) // (2 * gd))),
                      pl.BlockSpec((None, None, 2 * gd, wp), lambda i, pt: (layer, i, 0, 0)),
                      pl.BlockSpec(memory_space=pl.ANY)],
            out_specs=pl.BlockSpec((None, ts, qd), lambda i, pt: (i, 0, 0)),
            scratch_shapes=[pltpu.VMEM((2, n_pages, 2 * gd, PAGE_SIZE), F32), pltpu.SemaphoreType.DMA((2,))]),
        out_shape=jax.ShapeDtypeStruct((bs, ts, qd), F32),
        compiler_params=_params("arbitrary"),
        name="nsa_sample_attn",
    )(page_table, slope_rows, proj3, proj3, kcvc, proj3, proj3, win_t, cache_t)


def _moba_prompt_body(alibi_ref, q_ref, kf_ref, k_ref, v_ref, o_ref, kmean_sc, rhs_sc, m_sc, l_sc, acc_sc, o_sc,
                      *, tq, tk):
    i = pl.program_id(1)
    g_n, r_n, hd = MOBA_KV_HEADS, MOBA_GROUP, HEAD_DIM
    rows = r_n * tq
    nb = kmean_sc.shape[0]

    @pl.when(i == 0)
    def _():
        kmean_sc[...] = jnp.concatenate(
            [jnp.mean(kf_ref[n * MOBA_BLOCK:(n + 1) * MOBA_BLOCK, :], axis=0, keepdims=True) for n in range(nb)],
            axis=0)

    t_q = i * tq + lax.broadcasted_iota(jnp.int32, (1, tq), 1)
    cur = i * tq // MOBA_BLOCK
    blk = lax.broadcasted_iota(jnp.int32, (nb, 1), 0)
    q_t = q_ref[...].T
    own0 = pl.multiple_of(cur * MOBA_BLOCK, MOBA_BLOCK)
    own_pos = own0 + lax.broadcasted_iota(jnp.int32, (MOBA_BLOCK, 1), 0)
    causal = _tile_lanes(jnp.where(own_pos <= t_q, 0.0, NEG_INF), r_n)
    n_past = (cur * MOBA_BLOCK + tk - 1) // tk
    no_sel = jnp.zeros((AUG_OH, rows), F32)

    for g in range(g_n):
        gsl = slice(g * hd, (g + 1) * hd)
        asl = slice(g * AUG, (g + 1) * AUG)
        qg = _group_q_t(q_t, g, r_n)
        gs = jnp.dot(kmean_sc[:, gsl], qg, preferred_element_type=F32, precision=HIGHEST)
        sel = _topk_mask(gs, blk < cur, min(MOBA_TOPK, nb), 0)
        tail = _aug_tail(qg * (QK_SCALE * LOG2E), alibi_ref[g])
        rhs_sc[g] = _aug_rhs(1.0 - sel, tail)

        s = jnp.dot(k_ref[pl.ds(own0, MOBA_BLOCK), asl], _aug_rhs(no_sel, tail),
                    preferred_element_type=F32) + causal
        m_sc[g], l_sc[g], acc_sc[g] = _soft_first(s, v_ref[gsl, pl.ds(own0, MOBA_BLOCK)])

    def past(j, carry):
        key0 = pl.multiple_of(j * tk, tk)
        for g in range(g_n):
            m_sc[g], l_sc[g], acc_sc[g] = _soft_more(
                (m_sc[g], l_sc[g], acc_sc[g]), k_ref, v_ref, slice(g * AUG, (g + 1) * AUG),
                slice(g * hd, (g + 1) * hd), key0, tk, rhs_sc[g])
        return carry

    lax.fori_loop(0, n_past, past, 0)

    for g in range(g_n):
        o = acc_sc[g] / l_sc[g]
        for r in range(r_n):
            h = g * r_n + r
            o_sc[h * hd:(h + 1) * hd, :] = o[:, r * tq:(r + 1) * tq]
    o_ref[...] = o_sc[...].T


def _moba_prompt_attn(proj3, k_aug, v_t, alibi_rows, tq, tk=512):
    b, t, _ = proj3.shape
    g_n = MOBA_KV_HEADS
    gd = g_n * HEAD_DIM
    qd = g_n * MOBA_GROUP * HEAD_DIM
    rows = MOBA_GROUP * tq
    assert t % tk == 0 and tk % MOBA_BLOCK == 0 and MOBA_BLOCK % tq == 0 and t // MOBA_BLOCK <= AUG_OH
    return pl.pallas_call(
        functools.partial(_moba_prompt_body, tq=tq, tk=tk),
        grid=(b, t // tq),
        in_specs=[pl.BlockSpec((g_n, AUG_AL, rows), lambda bi, i: (0, 0, 0)),
                  pl.BlockSpec((None, tq, qd), lambda bi, i: (bi, i, 0)),
                  pl.BlockSpec((None, t, gd), lambda bi, i: (bi, 0, qd // gd)),
                  pl.BlockSpec((None, t, g_n * AUG), lambda bi, i: (bi, 0, 0)),
                  pl.BlockSpec((None, gd, t), lambda bi, i: (bi, 0, 0))],
        out_specs=pl.BlockSpec((None, tq, qd), lambda bi, i: (bi, i, 0)),
        out_shape=jax.ShapeDtypeStruct((b, t, qd), F32),
        scratch_shapes=[pltpu.VMEM((t // MOBA_BLOCK, gd), F32), pltpu.VMEM((g_n, AUG, rows), BF16),
                        pltpu.VMEM((g_n, 1, rows), F32), pltpu.VMEM((g_n, 1, rows), F32),
                        pltpu.VMEM((g_n, HEAD_DIM, rows), F32), pltpu.VMEM((qd, tq), F32)],
        compiler_params=_params("parallel", "arbitrary"),
        name="moba_prompt_attn",
    )(alibi_rows, proj3, proj3, k_aug, v_t)


def _moba_sample_body(pt_ref, slope_ref, q_ref, new_ref, cache_hbm, o_ref, buf, sem, *, layer, n_pages, tk):
    g_n, r_n, hd = MOBA_KV_HEADS, MOBA_GROUP, HEAD_DIM
    gd = g_n * hd
    ts = q_ref.shape[0]
    rows = g_n * r_n * ts
    past = n_pages * PAGE_SIZE
    nb = past // MOBA_BLOCK
    b = pl.program_id(0)
    slot = b % 2
    gather = lambda seq, sl: _gather_pages(pt_ref, seq, cache_hbm, layer, 0, 2 * gd, buf.at[sl], sem.at[sl],
                                           n_pages)

    @pl.when(b == 0)
    def _():
        gather(b, slot)[0]()

    @pl.when(b + 1 < pl.num_programs(0))
    def _():
        gather(b + 1, 1 - slot)[0]()

    qf = _block_diag_q(q_ref, g_n, r_n)
    qb = (qf * QK_SCALE).astype(BF16)
    slope = slope_ref[...]
    t = past + lax.broadcasted_iota(jnp.int32, (rows, 1), 0) % ts
    new_pos = past + lax.broadcasted_iota(jnp.int32, (1, ts), 1)
    gather(b, slot)[1]()

    blk = lax.broadcasted_iota(jnp.int32, (gd, nb), 1)
    kmean_t = jnp.zeros((gd, nb), F32)
    pages = MOBA_BLOCK // PAGE_SIZE
    for n in range(nb):
        col = sum(jnp.sum(buf[slot, n * pages + u, :gd, :], axis=1, keepdims=True) for u in range(pages))
        kmean_t = jnp.where(blk == n, col / MOBA_BLOCK, kmean_t)
    gs = jnp.dot(qf, kmean_t, preferred_element_type=F32, precision=HIGHEST)
    sel = _topk_mask(gs, jnp.full(gs.shape, True), min(MOBA_TOPK, nb), 1).astype(BF16)

    def step(j, carry):
        key0 = pl.multiple_of(j * tk, tk)
        kpos = key0 + lax.broadcasted_iota(jnp.int32, (1, tk), 1)
        in_sel = _block_mask(sel, key0, tk, MOBA_BLOCK.bit_length() - 1)
        return _softmax_step(carry, qb, _page_tile(buf, slot, j, tk, 0, gd), _page_tile(buf, slot, j, tk, gd, 2 * gd),
                             t, slope, kpos, lambda dist: in_sel, True)

    carry = lax.fori_loop(0, past // tk, step, _softmax_init(rows, gd))
    carry = _softmax_step(carry, qb, new_ref[:, :gd].astype(BF16), new_ref[:, gd:].astype(BF16),
                          t, slope, new_pos, lambda dist: dist >= 0, False)
    o = _softmax_done(carry)
    for h in range(g_n * r_n):
        g = h // r_n
        o_ref[:, h * hd:(h + 1) * hd] = o[h * ts:(h + 1) * ts, g * hd:(g + 1) * hd]


def _moba_sample_attn(proj3, cache_t, layer, page_table, slopes, tk=512):
    bs, ts, _ = proj3.shape
    n_pages = page_table.shape[1]
    gd = MOBA_KV_HEADS * HEAD_DIM
    qd = MOBA_KV_HEADS * MOBA_GROUP * HEAD_DIM
    slope_rows = jnp.repeat(slopes, ts).reshape(-1, 1)
    return pl.pallas_call(
        functools.partial(_moba_sample_body, layer=layer, n_pages=n_pages, tk=tk),
        grid_spec=pltpu.PrefetchScalarGridSpec(
            num_scalar_prefetch=1,
            grid=(bs,),
            in_specs=[pl.BlockSpec((slope_rows.shape[0], 1), lambda i, pt: (0, 0)),
                      pl.BlockSpec((None, ts, qd), lambda i, pt: (i, 0, 0)),
                      pl.BlockSpec((None, ts, 2 * gd), lambda i, pt: (i, 0, qd // (2 * gd))),
                      pl.BlockSpec(memory_space=pl.ANY)],
            out_specs=pl.BlockSpec((None, ts, qd), lambda i, pt: (i, 0, 0)),
            scratch_shapes=[pltpu.VMEM((2, n_pages, 2 * gd, PAGE_SIZE), F32), pltpu.SemaphoreType.DMA((2,))]),
        out_shape=jax.ShapeDtypeStruct((bs, ts, qd), F32),
        compiler_params=_params("arbitrary"),
        name="moba_sample_attn",
    )(page_table, slope_rows, proj3, proj3, cache_t)


PROMPT_Q_TILE = 128


def _alibi_slopes(n_heads):
    return jnp.exp2(-8.0 * jnp.arange(1, n_heads + 1, dtype=F32) / n_heads)


def _alibi_query_rows(slopes, g_n, r_n, tq):
    s2 = slopes * LOG2E
    d0 = s2.astype(BF16).astype(F32)
    d1 = (s2 - d0).astype(BF16).astype(F32)
    d2 = (s2 - d0 - d1).astype(BF16).astype(F32)
    zero = jnp.zeros_like(s2)
    digits = jnp.stack([d0, d1, d2, POS_DIGIT * d0, POS_DIGIT * d1, POS_DIGIT * d2, zero, zero], axis=0)
    lanes = lambda a: jnp.repeat(a.reshape(-1, g_n, r_n), tq, axis=2).transpose(1, 0, 2)
    return lanes(s2[None]), lanes(digits)


def _augment_keys(k, blk_len, n_onehot):
    b, t, g, hd = k.shape
    assert t <= POS_DIGIT * 256
    pos = np.arange(t)
    slot = (pos // blk_len) % n_onehot if n_onehot else np.full((t,), -1)
    onehot = np.where(slot[:, None] == np.arange(AUG_OH)[None, :], NEG_INF, 0.0)
    lo, hi, zero = pos % POS_DIGIT, pos // POS_DIGIT, np.zeros((t,))
    tail = np.stack([lo, lo, lo, hi, hi, hi, zero, zero] + [zero] * (AUG - AUG_OH - hd - AUG_AL), axis=1)
    wide = lambda a: jnp.broadcast_to(jnp.asarray(a, BF16)[None, :, None, :], (b, t, g, a.shape[1]))
    return jnp.concatenate([wide(onehot), k, wide(tail)], axis=-1).reshape(b, t, g * AUG)


def _positions_minor(a, n_lead):
    nd = a.ndim
    a = jnp.transpose(a, tuple(range(n_lead)) + tuple(range(n_lead + 1, nd)) + (n_lead,))
    return a.reshape(a.shape[:n_lead] + (-1, a.shape[-1]))


def _nsa_layer(xp, xs, bp, bs, cache_t, win_t, layer, win_state, page_table, w_in, w_out, pe, w1, w2, g, b,
               alpha):
    tp, ts = xp.shape[0] // bp, xs.shape[0] // bs
    g_n, hd = NSA_KV_HEADS, HEAD_DIM
    gd = g_n * hd
    qd = g_n * NSA_GROUP * hd
    n_in = w_in.shape[1]
    n_pad = -(-n_in // LANE) * LANE
    w_in_bf = jnp.pad(w_in, ((0, 0), (0, n_pad - n_in))).astype(BF16)
    pe, w1_bf, w2_bf = _compress_weights(pe, w1, w2)
    w_out_bf = w_out.astype(BF16)
    slopes = _alibi_slopes(g_n * NSA_GROUP)
    past = page_table.shape[1] * PAGE_SIZE
    assert PAGE_SIZE == LANE and past % (2 * CMP_BLOCK) == 0 and ts <= CMP_BLOCK

    proj_p, kvb_p = _proj(xp, w_in_bf, qd, qd + 6 * gd)
    proj_s, _ = _proj(xs, w_in_bf, qd, qd + 6 * gd)
    proj_p3 = proj_p.reshape(bp, tp, n_pad)
    proj_s3 = proj_s.reshape(bs, ts, n_pad)
    kcvc_p = _compress_prompt(proj_p3, qd, pe, w1_bf, w2_bf)
    kvb = kvb_p.reshape(bp, tp, 6, g_n, hd)
    v_t = lambda c: kvb[:, :, c].reshape(bp, tp, gd).transpose(0, 2, 1)
    slope_rows, alibi_rows = _alibi_query_rows(slopes, g_n, NSA_GROUP, PROMPT_Q_TILE)
    o_p = _nsa_prompt_attn(proj_p3, _augment_keys(kvb[:, :, 2], SEL_BLOCK, NSA_PAST_TILE // SEL_BLOCK), v_t(3),
                           _augment_keys(kvb[:, :, 4], SEL_BLOCK, 0), v_t(5), kcvc_p, slope_rows, alibi_rows,
                           PROMPT_Q_TILE)
    kcvc_s = _compress_sample(cache_t, layer, page_table, pe, w1_bf, w2_bf)
    o_s = _nsa_sample_attn(proj_s3, kcvc_s, win_t, cache_t, layer, page_table, slopes)
    xp = _out_ln(o_p.reshape(bp * tp, qd), xp, w_out_bf, g, b, alpha)
    xs = _out_ln(o_s.reshape(bs * ts, qd), xs, w_out_bf, g, b, alpha)

    kv_shape = (4, g_n, hd)
    win_shape = (2, g_n, hd)
    kv_p = proj_p3[:, :, qd:qd + 4 * gd].reshape((bp, tp) + kv_shape)
    kv_s = proj_s3[:, :, qd:qd + 4 * gd].reshape((bs, ts) + kv_shape)
    win_p = proj_p3[:, tp - min(WINDOW, tp):, qd + 4 * gd:qd + 6 * gd].reshape((bp, min(WINDOW, tp)) + win_shape)
    win_s = jnp.concatenate([win_state, proj_s3[:, :, qd + 4 * gd:qd + 6 * gd].reshape((bs, ts) + win_shape)], axis=1)
    win_s = win_s[:, win_s.shape[1] - min(WINDOW, win_s.shape[1]):]
    return xp, xs, kv_p, kv_s, win_p, win_s


def _pool_layer(xp, xs, bp, bs, state, past, w, scale, g, b, alpha):
    d = xp.shape[1]
    tp, ts = xp.shape[0] // bp, xs.shape[0] // bs
    assert POOL_HALO % ts == 0 and tp % POOL_HALO == 0
    w_bf = w.astype(BF16)
    scale = scale.reshape(1, d)
    xs3 = xs.reshape(bs, ts, d)
    xe_s = jnp.concatenate([jnp.zeros((bs, POOL_HALO - POOL_PAST, d), F32), state, xs3], axis=1)
    pool_p = xp.reshape(bp, tp, d)[:, tp - POOL_PAST:]
    pool_s = xe_s[:, xe_s.shape[1] - POOL_PAST:]
    xp = _pool_ln_prompt(xp, tp, w_bf, scale, g, b, alpha)
    xs = _pool_ln_sample(xe_s, ts, past, w_bf, scale, g, b, alpha)
    return xp, xs, pool_p, pool_s


def _moba_layer(xp, xs, bp, bs, cache_t, layer, page_table, w_in, w_out, g, b, alpha):
    tp, ts = xp.shape[0] // bp, xs.shape[0] // bs
    g_n, hd = MOBA_KV_HEADS, HEAD_DIM
    gd = g_n * hd
    qd = g_n * MOBA_GROUP * hd
    n_in = w_in.shape[1]
    w_in_bf, w_out_bf = w_in.astype(BF16), w_out.astype(BF16)
    slopes = _alibi_slopes(g_n * MOBA_GROUP)
    past = page_table.shape[1] * PAGE_SIZE
    assert PAGE_SIZE == LANE and past % MOBA_BLOCK == 0 and ts <= MOBA_BLOCK

    proj_p, kvb_p = _proj(xp, w_in_bf, qd, qd + 2 * gd)
    proj_s, _ = _proj(xs, w_in_bf, qd, qd + 2 * gd)
    proj_p3 = proj_p.reshape(bp, tp, n_in)
    proj_s3 = proj_s.reshape(bs, ts, n_in)
    kvb = kvb_p.reshape(bp, tp, 2, g_n, hd)
    _, alibi_rows = _alibi_query_rows(slopes, g_n, MOBA_GROUP, PROMPT_Q_TILE)
    o_p = _moba_prompt_attn(proj_p3, _augment_keys(kvb[:, :, 0], MOBA_BLOCK, AUG_OH),
                            kvb[:, :, 1].reshape(bp, tp, gd).transpose(0, 2, 1), alibi_rows, PROMPT_Q_TILE)
    o_s = _moba_sample_attn(proj_s3, cache_t, layer, page_table, slopes)
    xp = _out_ln(o_p.reshape(bp * tp, qd), xp, w_out_bf, g, b, alpha)
    xs = _out_ln(o_s.reshape(bs * ts, qd), xs, w_out_bf, g, b, alpha)
    kv_shape = (2, g_n, hd)
    kv_p = proj_p3[:, :, qd:].reshape((bp, tp) + kv_shape)
    kv_s = proj_s3[:, :, qd:].reshape((bs, ts) + kv_shape)
    return xp, xs, kv_p, kv_s


def kernel(x_prompt, x_sample, cache_nsa_kv, state_nsa_win, state_pool, cache_moba, page_table, ln_g, ln_b, mlp_w1, mlp_w2, nsa_w_in, nsa_w_out, nsa_cmp_pe, nsa_cmp_w1, nsa_cmp_w2, pool_w, pool_scale, moba_w_in, moba_w_out):
    bp, tp, d = x_prompt.shape
    bs, ts, _ = x_sample.shape
    depth = ln_g.shape[0]
    alpha = (2 * depth) ** 0.25
    past = page_table.shape[1] * PAGE_SIZE
    xp = x_prompt.reshape(bp * tp, d)
    xs = x_sample.reshape(bs * ts, d)
    nsa_cache_t = _positions_minor(cache_nsa_kv, 2)
    nsa_win_t = _positions_minor(state_nsa_win, 2)
    moba_cache_t = _positions_minor(cache_moba, 2)
    outs = {k: [] for k in ("nsa_kv_p", "nsa_kv_s", "nsa_win_p", "nsa_win_s", "pool_p", "pool_s", "moba_p", "moba_s")}
    for i in range(depth):
        kind, j = i % N_MIXERS, i // N_MIXERS
        g0, b0 = ln_g[i, 0].reshape(1, d), ln_b[i, 0].reshape(1, d)
        g1, b1 = ln_g[i, 1].reshape(1, d), ln_b[i, 1].reshape(1, d)
        if kind == 0:
            xp, xs, kv_p, kv_s, win_p, win_s = _nsa_layer(
                xp, xs, bp, bs, nsa_cache_t, nsa_win_t, j, state_nsa_win[j], page_table, nsa_w_in[j], nsa_w_out[j],
                nsa_cmp_pe[j], nsa_cmp_w1[j], nsa_cmp_w2[j], g0, b0, alpha)
            outs["nsa_kv_p"].append(kv_p); outs["nsa_kv_s"].append(kv_s)
            outs["nsa_win_p"].append(win_p); outs["nsa_win_s"].append(win_s)
        elif kind == 1:
            xp, xs, pool_p, pool_s = _pool_layer(xp, xs, bp, bs, state_pool[j], past, pool_w[j], pool_scale[j],
                                                 g0, b0, alpha)
            outs["pool_p"].append(pool_p); outs["pool_s"].append(pool_s)
        else:
            xp, xs, kv_p, kv_s = _moba_layer(xp, xs, bp, bs, moba_cache_t, j, page_table, moba_w_in[j],
                                             moba_w_out[j], g0, b0, alpha)
            outs["moba_p"].append(kv_p); outs["moba_s"].append(kv_s)
        w1_bf, w2_bf = mlp_w1[i].astype(BF16), mlp_w2[i].astype(BF16)
        xp = _mlp_ln(xp, w1_bf, w2_bf, g1, b1, alpha)
        xs = _mlp_ln(xs, w1_bf, w2_bf, g1, b1, alpha)
    return (xp.reshape(bp, tp, d), xs.reshape(bs, ts, d),
            jnp.stack(outs["nsa_kv_p"]), jnp.stack(outs["nsa_kv_s"]),
            jnp.stack(outs["nsa_win_p"]), jnp.stack(outs["nsa_win_s"]),
            jnp.stack(outs["pool_p"]), jnp.stack(outs["pool_s"]),
            jnp.stack(outs["moba_p"]), jnp.stack(outs["moba_s"]))
```

```python
import functools

import jax
import jax.numpy as jnp
import numpy as np
from jax import lax
from jax.experimental import pallas as pl
from jax.experimental.pallas import tpu as pltpu

F32 = jnp.float32
BF16 = jnp.bfloat16

HEAD_DIM = 64
PAGE_SIZE = 128
N_MIXERS = 3
NSA_KV_HEADS = 4
NSA_GROUP = 4
CMP_BLOCK = 32
SEL_BLOCK = 64
N_SEL = 16
WINDOW = 512
CMP_HIDDEN = 2 * HEAD_DIM
POOL_WINDOWS = (2, 4, 8, 16)
POOL_PAST = max(POOL_WINDOWS) - 1
POOL_HALO = 16
MOBA_KV_HEADS = 4
MOBA_GROUP = 4
MOBA_BLOCK = 256
MOBA_TOPK = 3
LN_EPS = 1e-5
NEG_INF = -1e30
QK_SCALE = HEAD_DIM ** -0.5
LOG2E = 1.4426950408889634

LANE = 128
VMEM_LIMIT = 56 * 1024 * 1024
NT_DIMS = (((1,), (1,)), ((), ()))
HIGHEST = lax.Precision.HIGHEST

AUG = LANE
AUG_OH = 16
AUG_AL = 8
AUG_AL0 = HEAD_DIM + AUG_OH
POS_DIGIT = 16
NSA_PAST_TILE = 8 * SEL_BLOCK
PAST_SPLIT = 2


def _params(*sem):
    return pltpu.CompilerParams(dimension_semantics=sem, vmem_limit_bytes=VMEM_LIMIT)


def _layer_norm(y, g, b):
    mu = jnp.mean(y, axis=-1, keepdims=True)
    d = y - mu
    var = jnp.mean(d * d, axis=-1, keepdims=True)
    return d * lax.rsqrt(var + LN_EPS) * g + b


def _row_tile(m, want):
    t = min(m, want)
    while m % t:
        t //= 2
    return t


def _proj_body(x_ref, w_ref, o_ref, *aug_refs, keys, values, tm, seq_len, g_n):
    acc = jnp.dot(x_ref[...].astype(BF16), w_ref[...], preferred_element_type=F32)
    o_ref[...] = acc
    if not aug_refs:
        return
    pos = (pl.program_id(0) * tm) % seq_len + lax.broadcasted_iota(jnp.int32, (tm, AUG), 0)
    lane = lax.broadcasted_iota(jnp.int32, (tm, AUG), 1)
    lo_digit = (lane >= AUG_AL0) & (lane < AUG_AL0 + 3)
    hi_digit = (lane >= AUG_AL0 + 3) & (lane < AUG_AL0 + 6)
    digits = jnp.where(lo_digit, pos % POS_DIGIT, jnp.where(hi_digit, pos // POS_DIGIT, 0)).astype(F32)
    for k_ref, (col0, blk_len, n_onehot) in zip(aug_refs, keys):
        extra = digits
        if n_onehot:
            extra = jnp.where(lane - HEAD_DIM == (pos // blk_len) % n_onehot, NEG_INF, digits)
        for g in range(g_n):
            col = col0 + g * HEAD_DIM
            piece = acc[:, col // LANE * LANE:(col // LANE + 1) * LANE]
            if col % LANE:
                piece = pltpu.roll(piece, LANE - col % LANE, 1)
            k_ref[:, g * AUG:(g + 1) * AUG] = jnp.where(lane < HEAD_DIM, piece, extra).astype(BF16)
    for v_ref, col0 in zip(aug_refs[len(keys):], values):
        v_ref[...] = acc[:, col0:col0 + g_n * HEAD_DIM].T.astype(BF16)


def _proj(x, w_bf, seq_len=None, keys=(), values=(), g_n=0):
    m, d = x.shape
    n = w_bf.shape[1]
    tm = _row_tile(seq_len or m, 512)
    per_seq = (seq_len or m) // tm
    gd = g_n * HEAD_DIM
    out_specs = [pl.BlockSpec((tm, n), lambda i: (i, 0))]
    out_shape = [jax.ShapeDtypeStruct((m, n), F32)]
    out_specs += [pl.BlockSpec((tm, g_n * AUG), lambda i: (i, 0)) for _ in keys]
    out_shape += [jax.ShapeDtypeStruct((m, g_n * AUG), BF16) for _ in keys]
    out_specs += [pl.BlockSpec((None, gd, tm), lambda i: (i // per_seq, 0, i % per_seq)) for _ in values]
    out_shape += [jax.ShapeDtypeStruct((m // (seq_len or m), gd, seq_len or m), BF16) for _ in values]
    return pl.pallas_call(
        functools.partial(_proj_body, keys=keys, values=values, tm=tm, seq_len=seq_len, g_n=g_n),
        grid=(m // tm,),
        in_specs=[pl.BlockSpec((tm, d), lambda i: (i, 0)),
                  pl.BlockSpec((d, n), lambda i: (0, 0))],
        out_specs=out_specs,
        out_shape=out_shape,
        compiler_params=_params("parallel"),
        name="proj",
    )(x, w_bf)


def _out_ln_body(o_ref, x_ref, w_ref, g_ref, b_ref, y_ref, *, alpha):
    h = jnp.dot(o_ref[...].astype(BF16), w_ref[...], preferred_element_type=F32)
    y_ref[...] = _layer_norm(alpha * x_ref[...] + h, g_ref[...], b_ref[...])


def _out_ln(o, x, w_bf, g, b, alpha):
    m, d = x.shape
    k = o.shape[1]
    tm = _row_tile(m, 512)
    return pl.pallas_call(
        functools.partial(_out_ln_body, alpha=alpha),
        grid=(m // tm,),
        in_specs=[pl.BlockSpec((tm, k), lambda i: (i, 0)),
                  pl.BlockSpec((tm, d), lambda i: (i, 0)),
                  pl.BlockSpec((k, d), lambda i: (0, 0)),
                  pl.BlockSpec((1, d), lambda i: (0, 0)),
                  pl.BlockSpec((1, d), lambda i: (0, 0))],
        out_specs=pl.BlockSpec((tm, d), lambda i: (i, 0)),
        out_shape=jax.ShapeDtypeStruct((m, d), F32),
        compiler_params=_params("parallel"),
        name="out_ln",
    )(o, x, w_bf, g, b)


MLP_F_CHUNK = 1024


def _mlp_ln_body(x_ref, w1_ref, w2_ref, g_ref, b_ref, y_ref, *, alpha):
    x = x_ref[...]
    xb = x.astype(BF16)
    acc = None
    for c in range(w1_ref.shape[1] // MLP_F_CHUNK):
        fs = slice(c * MLP_F_CHUNK, (c + 1) * MLP_F_CHUNK)
        h = jnp.maximum(jnp.dot(xb, w1_ref[:, fs], preferred_element_type=F32), 0.0)
        part = jnp.dot((h * h).astype(BF16), w2_ref[fs, :], preferred_element_type=F32)
        acc = part if acc is None else acc + part
    y_ref[...] = _layer_norm(alpha * x + acc, g_ref[...], b_ref[...])


def _mlp_ln(x, w1_bf, w2_bf, g, b, alpha):
    m, d = x.shape
    f = w1_bf.shape[1]
    tm = _row_tile(m, 512)
    once = pl.Buffered(1)
    return pl.pallas_call(
        functools.partial(_mlp_ln_body, alpha=alpha),
        grid=(m // tm,),
        in_specs=[pl.BlockSpec((tm, d), lambda i: (i, 0)),
                  pl.BlockSpec((d, f), lambda i: (0, 0), pipeline_mode=once),
                  pl.BlockSpec((f, d), lambda i: (0, 0), pipeline_mode=once),
                  pl.BlockSpec((1, d), lambda i: (0, 0)),
                  pl.BlockSpec((1, d), lambda i: (0, 0))],
        out_specs=pl.BlockSpec((tm, d), lambda i: (i, 0)),
        out_shape=jax.ShapeDtypeStruct((m, d), F32),
        compiler_params=_params("parallel"),
        name="mlp_ln",
    )(x, w1_bf, w2_bf, g, b)


def _pool_ln_body(halo_ref, x_ref, w_ref, scale_ref, g_ref, b_ref, y_ref, *, alpha, pos0, tiles_per_seq):
    tm, d = x_ref.shape
    grp = d // len(POOL_WINDOWS)
    seq_tile = pl.program_id(0) % tiles_per_seq
    x = x_ref[...]
    halo = halo_ref[...]
    if pos0 == 0:
        halo = jnp.where(seq_tile == 0, 0.0, halo)
    xe = jnp.concatenate([halo, x], axis=0)
    n_avail = pos0 + seq_tile * tm + lax.broadcasted_iota(jnp.int32, (tm, 1), 0) + 1
    mixed = []
    for gi, w in enumerate(POOL_WINDOWS):
        s = xe[:, gi * grp:(gi + 1) * grp]
        step = 1
        while step < w:
            s = s + pltpu.roll(s, step, 0)
            step *= 2
        cnt = jnp.minimum(n_avail, w).astype(F32)
        pooled = s[POOL_HALO:] / cnt - x[:, gi * grp:(gi + 1) * grp]
        mixed.append(jnp.dot(pooled.astype(BF16), w_ref[gi], preferred_element_type=F32))
    h = jnp.concatenate(mixed, axis=-1) * scale_ref[...]
    y_ref[...] = _layer_norm(alpha * x + h, g_ref[...], b_ref[...])


def _pool_ln(rows, halo_spec, x_spec, n_tiles, tm, tiles_per_seq, pos0, w_bf, scale, g, b, alpha):
    d = rows.shape[-1]
    ngrp = len(POOL_WINDOWS)
    const = lambda i: (0, 0)
    return pl.pallas_call(
        functools.partial(_pool_ln_body, alpha=alpha, pos0=pos0, tiles_per_seq=tiles_per_seq),
        grid=(n_tiles,),
        in_specs=[halo_spec, x_spec,
                  pl.BlockSpec((ngrp, d // ngrp, d // ngrp), lambda i: (0, 0, 0)),
                  pl.BlockSpec((1, d), const), pl.BlockSpec((1, d), const), pl.BlockSpec((1, d), const)],
        out_specs=pl.BlockSpec((tm, d), lambda i: (i, 0)),
        out_shape=jax.ShapeDtypeStruct((n_tiles * tm, d), F32),
        compiler_params=_params("parallel"),
        name="pool_ln",
    )(rows, rows, w_bf, scale, g, b)


def _pool_ln_prompt(x2d, seq_len, w_bf, scale, g, b, alpha):
    m, d = x2d.shape
    tm = _row_tile(seq_len, 512)
    per = tm // POOL_HALO
    halo_spec = pl.BlockSpec((POOL_HALO, d), lambda i: (jnp.maximum(i * per - 1, 0), 0))
    x_spec = pl.BlockSpec((tm, d), lambda i: (i, 0))
    return _pool_ln(x2d, halo_spec, x_spec, m // tm, tm, seq_len // tm, 0, w_bf, scale, g, b, alpha)


def _pool_ln_sample(xe3, ts, pos0, w_bf, scale, g, b, alpha):
    bs, _, d = xe3.shape
    halo_spec = pl.BlockSpec((None, POOL_HALO, d), lambda i: (i, 0, 0))
    x_spec = pl.BlockSpec((None, ts, d), lambda i: (i, POOL_HALO // ts, 0))
    return _pool_ln(xe3, halo_spec, x_spec, bs, ts, 1, pos0, w_bf, scale, g, b, alpha)


def _topk_mask(vals, valid, k, axis):
    n = vals.shape[axis]
    idx = lax.broadcasted_iota(jnp.int32, vals.shape, axis).astype(F32)
    work = jnp.where(valid, vals, -jnp.inf)
    sel = jnp.zeros(vals.shape, F32)
    for _ in range(k):
        m = jnp.max(work, axis=axis, keepdims=True)
        first = jnp.min(jnp.where(work == m, idx, float(n)), axis=axis, keepdims=True)
        pick = idx == first
        sel = jnp.where(pick, 1.0, sel)
        work = jnp.where(pick, -jnp.inf, work)
    return jnp.where(valid, sel, 0.0)


def _cmp_positions(nc, axis):
    shape = (nc, 1) if axis == 0 else (1, nc)
    i = lax.broadcasted_iota(jnp.int32, shape, axis)
    half = nc // 2
    n = jnp.where(i < half, 2 * i, 2 * (i - half) + 1)
    c_end = n * CMP_BLOCK + (CMP_BLOCK - 1)
    c_mid = c_end.astype(F32) - 0.5 * (CMP_BLOCK - 1)
    return c_end, c_mid


def _softmax_init(rows, vdim):
    return (jnp.full((rows, 1), NEG_INF, F32), jnp.zeros((rows, 1), F32), jnp.zeros((rows, vdim), F32))


def _softmax_step(carry, qb, k, v, t, slope, kpos, mask_fn, kv_t):
    m, l, acc = carry
    if kv_t:
        s = jnp.dot(qb, k, preferred_element_type=F32)
    else:
        s = lax.dot_general(qb, k, NT_DIMS, preferred_element_type=F32)
    dist = t - kpos
    s = s - slope * jnp.abs(dist).astype(F32)
    mask = mask_fn(dist)
    s = jnp.where(mask, s, NEG_INF)
    m_new = jnp.maximum(m, jnp.max(s, axis=-1, keepdims=True))
    a = jnp.exp(m - m_new)
    p = jnp.where(mask, jnp.exp(s - m_new), 0.0).astype(BF16)
    l = a * l + jnp.sum(p.astype(F32), axis=-1, keepdims=True)
    if kv_t:
        pv = lax.dot_general(p, v, NT_DIMS, preferred_element_type=F32)
    else:
        pv = jnp.dot(p, v, preferred_element_type=F32)
    return m_new, l, a * acc + pv


def _softmax_done(carry):
    _, l, acc = carry
    return acc / jnp.where(l > 0.0, l, 1.0)


def _block_mask(sel_bf, key0, tk, blk_shift):
    nblk = sel_bf.shape[1]
    kblk = (key0 + lax.broadcasted_iota(jnp.int32, (nblk, tk), 1)) >> blk_shift
    expand = jnp.where(kblk == lax.broadcasted_iota(jnp.int32, (nblk, tk), 0), 1.0, 0.0).astype(BF16)
    return jnp.dot(sel_bf, expand, preferred_element_type=F32) > 0.5


def _cmp_attention(q, kc, vc, t, slope):
    nc = kc.shape[0]
    c_end, c_mid = _cmp_positions(nc, 1)
    s = lax.dot_general(q, kc, NT_DIMS, preferred_element_type=F32, precision=HIGHEST)
    s = s - slope * jnp.abs(t.astype(F32) - c_mid)
    mask = c_end <= t
    s = jnp.where(mask, s, NEG_INF)
    m = jnp.max(s, axis=-1, keepdims=True)
    p = jnp.where(mask, jnp.exp(s - m), 0.0)
    l = jnp.sum(p, axis=-1, keepdims=True)
    p = p / jnp.where(l > 0.0, l, 1.0)
    o = jnp.dot(p.astype(BF16), vc.astype(BF16), preferred_element_type=F32)
    return p, o


def _tile_lanes(x, n):
    return jnp.concatenate([x] * n, axis=1)


def _group_q_t(q_t, g, r_n):
    hd = HEAD_DIM
    return jnp.concatenate([q_t[(g * r_n + r) * hd:(g * r_n + r + 1) * hd] for r in range(r_n)], axis=1)


def _aug_tail(q_t, alibi):
    pad = jnp.zeros((AUG - AUG_OH - HEAD_DIM - AUG_AL, q_t.shape[1]), F32)
    return jnp.concatenate([q_t, alibi, pad], axis=0).astype(BF16)


def _aug_rhs(not_sel, tail):
    pad_oh = AUG_OH - not_sel.shape[0]
    if pad_oh:
        not_sel = jnp.concatenate([not_sel, jnp.zeros((pad_oh, not_sel.shape[1]), F32)], axis=0)
    return jnp.concatenate([tail[:HEAD_DIM], not_sel.astype(BF16), tail[HEAD_DIM:]], axis=0)


def _soft_first(s, v_t):
    m = jnp.max(s, axis=0, keepdims=True)
    p = jnp.exp2(s - m)
    return m, jnp.sum(p, axis=0, keepdims=True), jnp.dot(v_t, p.astype(BF16), preferred_element_type=F32)


def _soft_more(carry, k_ref, v_ref, ksl, vsl, key0, tk, rhs):
    m, l, acc = carry
    part = tk // PAST_SPLIT
    keys = [pl.ds(pl.multiple_of(key0 + u * part, part), part) for u in range(PAST_SPLIT)]
    s = [jnp.dot(k_ref[kk, ksl], rhs, preferred_element_type=F32) for kk in keys]
    m_new = m
    for su in s:
        m_new = jnp.maximum(m_new, jnp.max(su, axis=0, keepdims=True))
    a = jnp.exp2(m - m_new)
    l, acc = a * l, a * acc
    for su, kk in zip(s, keys):
        p = jnp.exp2(su - m_new)
        l = l + jnp.sum(p, axis=0, keepdims=True)
        acc = acc + jnp.dot(v_ref[vsl, kk], p.astype(BF16), preferred_element_type=F32)
    return m_new, l, acc


def _soft_done(carry):
    _, l, acc = carry
    return acc / l


HEADS_PER_LANE_TILE = LANE // HEAD_DIM


def _compress_rows(block_rows, pe_ref, w1_ref, w2_ref, nc):
    acc = jnp.zeros((nc, HEADS_PER_LANE_TILE * CMP_HIDDEN), F32)
    for r in range(CMP_BLOCK):
        rows = block_rows(r) + pe_ref[r:r + 1, :]
        acc = acc + jnp.dot(rows.astype(BF16), w1_ref[r], preferred_element_type=F32)
    return jnp.dot(jax.nn.gelu(acc).astype(BF16), w2_ref[...], preferred_element_type=F32)


def _compress_weights(pe, w1, w2):
    n_h = HEADS_PER_LANE_TILE
    eye = jnp.eye(n_h, dtype=w1.dtype)
    w1r = w1.reshape(2, CMP_BLOCK, HEAD_DIM, CMP_HIDDEN)
    w1_bd = jnp.einsum('ab,crdh->cradbh', eye, w1r).reshape(2, CMP_BLOCK, n_h * HEAD_DIM, n_h * CMP_HIDDEN)
    w2_bd = jnp.einsum('ab,chd->cahbd', eye, w2).reshape(2, n_h * CMP_HIDDEN, n_h * HEAD_DIM)
    return jnp.tile(pe, (1, 1, n_h)), w1_bd.astype(BF16), w2_bd.astype(BF16)


def _compress_prompt_body(src_ref, pe_ref, w1_ref, w2_ref, o_ref, *, nc):
    half = nc // 2

    def block_rows(r):
        return jnp.concatenate([src_ref[pl.ds(r, half, stride=2 * CMP_BLOCK), :],
                                src_ref[pl.ds(CMP_BLOCK + r, half, stride=2 * CMP_BLOCK), :]], axis=0)

    o_ref[...] = _compress_rows(block_rows, pe_ref, w1_ref, w2_ref, nc)


def _compress_prompt(proj3, col0, pe, w1_bf, w2_bf):
    b, t, _ = proj3.shape
    gd = NSA_KV_HEADS * HEAD_DIM
    nc = t // CMP_BLOCK
    tiles = gd // LANE
    return pl.pallas_call(
        functools.partial(_compress_prompt_body, nc=nc),
        grid=(2, b, tiles),
        in_specs=[pl.BlockSpec((None, t, LANE), lambda c, i, h: (i, 0, col0 // LANE + c * tiles + h)),
                  pl.BlockSpec((None,) + pe.shape[1:], lambda c, i, h: (c, 0, 0)),
                  pl.BlockSpec((None,) + w1_bf.shape[1:], lambda c, i, h: (c, 0, 0, 0)),
                  pl.BlockSpec((None,) + w2_bf.shape[1:], lambda c, i, h: (c, 0, 0))],
        out_specs=pl.BlockSpec((None, None, nc, LANE), lambda c, i, h: (c, i, 0, h)),
        out_shape=jax.ShapeDtypeStruct((2, b, nc, gd), F32),
        compiler_params=_params("parallel", "parallel", "parallel"),
        name="nsa_compress_prompt",
    )(proj3, pe, w1_bf, w2_bf)


def _gather_pages(pt_ref, b, cache_hbm, layer, row0, nrows, buf, sem, n_pages):
    def copy(p):
        return pltpu.make_async_copy(cache_hbm.at[layer, pt_ref[b, p], pl.ds(row0, nrows), :], buf.at[p], sem)

    def start():
        for p in range(n_pages):
            copy(p).start()

    def wait():
        for p in range(n_pages):
            copy(p).wait()

    return start, wait


ROW_GROUP = 8
PAGES_PER_GROUP = ROW_GROUP * CMP_BLOCK // PAGE_SIZE


def _block_row_permutation():
    n = PAGES_PER_GROUP * PAGE_SIZE
    out_row = np.arange(n)
    src = (out_row % ROW_GROUP) * CMP_BLOCK + out_row // ROW_GROUP
    return (src[:, None] == np.arange(n)[None, :]).astype(np.float32)


def _compress_sample_body(pt_ref, cache_hbm, perm_ref, pe_ref, w1_ref, w2_ref, o_ref, buf, rowbuf, sem,
                          *, layer, n_pages, nc):
    gd = NSA_KV_HEADS * HEAD_DIM
    tiles = gd // LANE
    b = pl.program_id(0)
    gather = lambda seq: _gather_pages(pt_ref, seq, cache_hbm, layer, 0, 2 * gd, buf, sem, n_pages)

    @pl.when(b == 0)
    def _():
        gather(b)[0]()

    gather(b)[1]()

    def to_rows(pg, carry):
        for c in range(2 * tiles):
            x_t = jnp.concatenate([buf[pg * PAGES_PER_GROUP + u, c * LANE:(c + 1) * LANE, :]
                                   for u in range(PAGES_PER_GROUP)], axis=1).astype(BF16)
            rows = lax.dot_general(perm_ref[...], x_t, NT_DIMS, preferred_element_type=F32)
            rowbuf[c, pg] = rows.reshape(CMP_BLOCK, ROW_GROUP, LANE)
        return carry

    lax.fori_loop(0, n_pages // PAGES_PER_GROUP, to_rows, 0)

    @pl.when(b + 1 < pl.num_programs(0))
    def _():
        gather(b + 1)[0]()

    for c in range(2):
        for h in range(tiles):
            src = rowbuf.at[c * tiles + h]
            o_ref[c, :, h * LANE:(h + 1) * LANE] = _compress_rows(
                lambda r: src[:, r].reshape(nc, LANE), pe_ref.at[c], w1_ref.at[c], w2_ref.at[c], nc)


def _compress_sample(cache_t, layer, page_table, pe, w1_bf, w2_bf):
    bs, n_pages = page_table.shape
    gd = NSA_KV_HEADS * HEAD_DIM
    nc = n_pages * PAGE_SIZE // CMP_BLOCK
    assert n_pages % PAGES_PER_GROUP == 0
    perm = jnp.asarray(_block_row_permutation(), BF16)
    tokens = pl.pallas_call(
        functools.partial(_compress_sample_body, layer=layer, n_pages=n_pages, nc=nc),
        grid_spec=pltpu.PrefetchScalarGridSpec(
            num_scalar_prefetch=1,
            grid=(bs,),
            in_specs=[pl.BlockSpec(memory_space=pl.ANY),
                      pl.BlockSpec(perm.shape, lambda i, pt: (0, 0)),
                      pl.BlockSpec(pe.shape, lambda i, pt: (0, 0, 0)),
                      pl.BlockSpec(w1_bf.shape, lambda i, pt: (0, 0, 0, 0)),
                      pl.BlockSpec(w2_bf.shape, lambda i, pt: (0, 0, 0))],
            out_specs=pl.BlockSpec((None, 2, nc, gd), lambda i, pt: (i, 0, 0, 0)),
            scratch_shapes=[pltpu.VMEM((n_pages, 2 * gd, PAGE_SIZE), F32),
                            pltpu.VMEM((2 * gd // LANE, n_pages // PAGES_PER_GROUP, CMP_BLOCK, ROW_GROUP, LANE),
                                       F32),
                            pltpu.SemaphoreType.DMA(())]),
        out_shape=jax.ShapeDtypeStruct((bs, 2, nc, gd), F32),
        compiler_params=_params("arbitrary"),
        name="nsa_compress_sample",
    )(page_table, cache_t, perm, pe, w1_bf, w2_bf)
    return tokens.reshape(bs, 2, nc // 2, 2, gd).transpose(0, 1, 3, 2, 4).reshape(bs, 2, nc, gd)


def _nsa_prompt_body(slope_ref, alibi_ref, q_ref, gate_ref, kc_ref, vc_ref, ks_ref, vs_ref, kw_ref, vw_ref,
                     o_ref, ns_sc, own_sc, tail_sc, m_sc, l_sc, acc_sc, o_sc, *, tq, tk, wk):
    i = pl.program_id(1)
    g_n, r_n, hd = NSA_KV_HEADS, NSA_GROUP, HEAD_DIM
    n_h = g_n * r_n
    rows = r_n * tq
    nc = kc_ref.shape[0]
    ns = nc // 2
    t_q = i * tq + lax.broadcasted_iota(jnp.int32, (1, tq), 1)
    t_row = _tile_lanes(t_q, r_n)
    cur = t_q // SEL_BLOCK
    blk = lax.broadcasted_iota(jnp.int32, (ns, 1), 0)
    q_t = q_ref[...].T * (QK_SCALE * LOG2E)
    gate_t = jax.nn.sigmoid(gate_ref[...]).T
    vc_t = vc_ref[...].T
    c_end, c_mid = _cmp_positions(nc, 0)
    key_i = lax.broadcasted_iota(jnp.int32, (tq, 1), 0)
    causal = _tile_lanes(jnp.where(key_i <= lax.broadcasted_iota(jnp.int32, (1, tq), 1), 0.0, NEG_INF), r_n)
    key_d = pl.multiple_of(i * tq, tq)
    blk_d = pl.multiple_of((i * tq // SEL_BLOCK) // 8 * 8, 8)
    n_past = (i * tq + tk - 1) // tk
    w0 = pl.multiple_of(jnp.maximum((i + 1) * tq - wk, 0), tq)
    w_dist = t_q - (w0 + lax.broadcasted_iota(jnp.int32, (wk, 1), 0))
    in_win = _tile_lanes(jnp.where((w_dist >= 0) & (w_dist < WINDOW), 0.0, NEG_INF), r_n)
    no_sel = jnp.zeros((AUG_OH, rows), F32)

    for g in range(g_n):
        gsl = slice(g * hd, (g + 1) * hd)
        asl = slice(g * AUG, (g + 1) * AUG)
        qg = _group_q_t(q_t, g, r_n)
        slope = slope_ref[g]

        s = jnp.dot(kc_ref[:, gsl], qg, preferred_element_type=F32, precision=HIGHEST)
        s = s - slope * jnp.abs(t_row.astype(F32) - c_mid)
        valid = c_end <= t_row
        s = jnp.where(valid, s, NEG_INF)
        p = jnp.where(valid, jnp.exp2(s - jnp.max(s, axis=0, keepdims=True)), 0.0)
        l = jnp.sum(p, axis=0, keepdims=True)
        p = p / jnp.where(l > 0.0, l, 1.0)
        o_cmp = jnp.dot(vc_t[gsl].astype(BF16), p.astype(BF16), preferred_element_type=F32)
        imp = p[:, 0:tq]
        for r in range(1, r_n):
            imp = imp + p[:, r * tq:(r + 1) * tq]
        imp = imp[:ns] + imp[ns:]
        sel = _topk_mask(imp, blk < cur, min(N_SEL - 1, ns), 0)
        ns_sc[g] = _tile_lanes(1.0 - jnp.where(blk < i * tq // SEL_BLOCK, sel, 0.0), r_n)
        own_sc[g] = _tile_lanes(1.0 - jnp.where(blk == cur, 1.0, sel), r_n)

        tail = _aug_tail(qg, alibi_ref[g])
        tail_sc[g] = tail

        s = jnp.dot(ks_ref[pl.ds(key_d, tq), asl], _aug_rhs(own_sc[g, pl.ds(blk_d, 8), :], tail),
                    preferred_element_type=F32) + causal
        m_sc[g], l_sc[g], acc_sc[g] = _soft_first(s, vs_ref[gsl, pl.ds(key_d, tq)])

        s = jnp.dot(kw_ref[pl.ds(w0, wk), asl], _aug_rhs(no_sel, tail), preferred_element_type=F32) + in_win
        o_win = _soft_done(_soft_first(s, vw_ref[gsl, pl.ds(w0, wk)]))

        for r in range(r_n):
            h = g * r_n + r
            ls = slice(r * tq, (r + 1) * tq)
            o_sc[h * hd:(h + 1) * hd, :] = (gate_t[h:h + 1] * o_cmp[:, ls]
                                            + gate_t[2 * n_h + h:2 * n_h + h + 1] * o_win[:, ls])

    def past(j, carry):
        key0 = pl.multiple_of(j * tk, tk)
        blk0 = pl.multiple_of(j * 8, 8)
        for g in range(g_n):
            m_sc[g], l_sc[g], acc_sc[g] = _soft_more(
                (m_sc[g], l_sc[g], acc_sc[g]), ks_ref, vs_ref, slice(g * AUG, (g + 1) * AUG),
                slice(g * hd, (g + 1) * hd), key0, tk, _aug_rhs(ns_sc[g, pl.ds(blk0, 8), :], tail_sc[g]))
        return carry

    lax.fori_loop(0, n_past, past, 0)

    for g in range(g_n):
        o_slc = acc_sc[g] / l_sc[g]
        for r in range(r_n):
            h = g * r_n + r
            o_sc[h * hd:(h + 1) * hd, :] += gate_t[n_h + h:n_h + h + 1] * o_slc[:, r * tq:(r + 1) * tq]
    o_ref[...] = o_sc[...].T


def _nsa_prompt_attn(proj3, k_slc, v_slc_t, k_win, v_win_t, kcvc, slope_rows, alibi_rows, tq):
    b, t, _ = proj3.shape
    g_n = NSA_KV_HEADS
    gd = g_n * HEAD_DIM
    qd = g_n * NSA_GROUP * HEAD_DIM
    nc = kcvc.shape[2]
    rows = NSA_GROUP * tq
    tk = NSA_PAST_TILE
    wk = WINDOW + tq
    assert t % tk == 0 and t >= wk and (nc // 2) % 8 == 0
    k_spec = pl.BlockSpec((None, t, g_n * AUG), lambda bi, i: (bi, 0, 0))
    v_spec = pl.BlockSpec((None, gd, t), lambda bi, i: (bi, 0, 0))
    return pl.pallas_call(
        functools.partial(_nsa_prompt_body, tq=tq, tk=tk, wk=wk),
        grid=(b, t // tq),
        in_specs=[pl.BlockSpec((g_n, 1, rows), lambda bi, i: (0, 0, 0)),
                  pl.BlockSpec((g_n, AUG_AL, rows), lambda bi, i: (0, 0, 0)),
                  pl.BlockSpec((None, tq, qd), lambda bi, i: (bi, i, 0)),
                  pl.BlockSpec((None, tq, LANE), lambda bi, i: (bi, i, (qd + 6 * gd) // LANE)),
                  pl.BlockSpec((None, None, nc, gd), lambda bi, i: (0, bi, 0, 0)),
                  pl.BlockSpec((None, None, nc, gd), lambda bi, i: (1, bi, 0, 0)),
                  k_spec, v_spec, k_spec, v_spec],
        out_specs=pl.BlockSpec((None, tq, qd), lambda bi, i: (bi, i, 0)),
        out_shape=jax.ShapeDtypeStruct((b, t, qd), F32),
        scratch_shapes=[pltpu.VMEM((g_n, nc // 2, rows), F32), pltpu.VMEM((g_n, nc // 2, rows), F32),
                        pltpu.VMEM((g_n, AUG - AUG_OH, rows), BF16),
                        pltpu.VMEM((g_n, 1, rows), F32), pltpu.VMEM((g_n, 1, rows), F32),
                        pltpu.VMEM((g_n, HEAD_DIM, rows), F32), pltpu.VMEM((qd, tq), F32)],
        compiler_params=_params("parallel", "arbitrary"),
        name="nsa_prompt_attn",
    )(slope_rows, alibi_rows, proj3, proj3, kcvc, kcvc, k_slc, v_slc_t, k_win, v_win_t)


def _page_tile(buf, slot, j, tk, lo, hi):
    pages = tk // PAGE_SIZE
    return jnp.concatenate([buf[slot, j * pages + u, lo:hi, :] for u in range(pages)], axis=1).astype(BF16)


def _block_diag_q(q_ref, g_n, r_n):
    hd = HEAD_DIM
    ts = q_ref.shape[0]
    out = []
    for g in range(g_n):
        qg = jnp.concatenate([q_ref[:, (g * r_n + r) * hd:(g * r_n + r + 1) * hd] for r in range(r_n)], axis=0)
        parts = [jnp.zeros((r_n * ts, hd), F32)] * g_n
        parts[g] = qg
        out.append(jnp.concatenate(parts, axis=-1))
    return jnp.concatenate(out, axis=0)


def _nsa_sample_body(pt_ref, slope_ref, q_ref, gate_ref, kcvc_ref, snew_ref, wnew_ref, win_ref, cache_hbm,
                     o_ref, buf, sem, *, layer, n_pages, tk):
    g_n, r_n, hd = NSA_KV_HEADS, NSA_GROUP, HEAD_DIM
    gd = g_n * hd
    ts = q_ref.shape[0]
    rows = g_n * r_n * ts
    past = n_pages * PAGE_SIZE
    b = pl.program_id(0)
    slot = b % 2
    gather = lambda seq, sl: _gather_pages(pt_ref, seq, cache_hbm, layer, 2 * gd, 2 * gd, buf.at[sl],
                                           sem.at[sl], n_pages)

    @pl.when(b == 0)
    def _():
        gather(b, slot)[0]()

    @pl.when(b + 1 < pl.num_programs(0))
    def _():
        gather(b + 1, 1 - slot)[0]()

    qf = _block_diag_q(q_ref, g_n, r_n) * QK_SCALE
    qb = qf.astype(BF16)
    slope = slope_ref[...]
    t = past + lax.broadcasted_iota(jnp.int32, (rows, 1), 0) % ts
    new_pos = past + lax.broadcasted_iota(jnp.int32, (1, ts), 1)
    causal = lambda dist: dist >= 0

    nc = kcvc_ref.shape[1]
    ns = nc // 2
    p_cmp, o_cmp = _cmp_attention(qf, kcvc_ref[0], kcvc_ref[1], t, slope)
    imp = []
    for g in range(g_n):
        base = g * r_n * ts
        acc = p_cmp[base:base + ts]
        for r in range(1, r_n):
            acc = acc + p_cmp[base + r * ts:base + (r + 1) * ts]
        imp.append(acc)
    imp = jnp.concatenate(imp, axis=0)
    imp = imp[:, :ns] + imp[:, ns:]
    sel = _topk_mask(imp, jnp.full(imp.shape, True), min(N_SEL - 1, ns), 1).astype(BF16)
    sel = jnp.concatenate([sel[g * ts:(g + 1) * ts] for g in range(g_n) for _ in range(r_n)], axis=0)

    wp = win_ref.shape[1]
    win_pos = (past - wp) + lax.broadcasted_iota(jnp.int32, (1, wp), 1)
    in_win = lambda dist: (dist >= 0) & (dist < WINDOW)
    carry = _softmax_step(_softmax_init(rows, gd), qb, win_ref[:gd, :].astype(BF16),
                          win_ref[gd:, :].astype(BF16), t, slope, win_pos, in_win, True)
    carry = _softmax_step(carry, qb, wnew_ref[:, :gd].astype(BF16), wnew_ref[:, gd:].astype(BF16),
                          t, slope, new_pos, in_win, False)
    o_win = _softmax_done(carry)

    gather(b, slot)[1]()

    def slc_step(j, carry):
        key0 = pl.multiple_of(j * tk, tk)
        kpos = key0 + lax.broadcasted_iota(jnp.int32, (1, tk), 1)
        in_sel = _block_mask(sel, key0, tk, SEL_BLOCK.bit_length() - 1)
        return _softmax_step(carry, qb, _page_tile(buf, slot, j, tk, 0, gd), _page_tile(buf, slot, j, tk, gd, 2 * gd),
                             t, slope, kpos, lambda dist: in_sel, True)

    carry = lax.fori_loop(0, past // tk, slc_step, _softmax_init(rows, gd))
    carry = _softmax_step(carry, qb, snew_ref[:, :gd].astype(BF16), snew_ref[:, gd:].astype(BF16),
                          t, slope, new_pos, causal, False)
    o_slc = _softmax_done(carry)

    gate = jax.nn.sigmoid(gate_ref[...])
    n_h = g_n * r_n
    for h in range(n_h):
        g = h // r_n
        rs = slice(h * ts, (h + 1) * ts)
        cs = slice(g * hd, (g + 1) * hd)
        o_ref[:, h * hd:(h + 1) * hd] = (gate[:, h:h + 1] * o_cmp[rs, cs]
                                         + gate[:, n_h + h:n_h + h + 1] * o_slc[rs, cs]
                                         + gate[:, 2 * n_h + h:2 * n_h + h + 1] * o_win[rs, cs])


def _nsa_sample_attn(proj3, kcvc, win_t, cache_t, layer, page_table, slopes, tk=512):
    bs, ts, _ = proj3.shape
    n_pages = page_table.shape[1]
    gd = NSA_KV_HEADS * HEAD_DIM
    qd = NSA_KV_HEADS * NSA_GROUP * HEAD_DIM
    nc = kcvc.shape[2]
    wp = win_t.shape[3]
    slope_rows = jnp.repeat(slopes, ts).reshape(-1, 1)
    return pl.pallas_call(
        functools.partial(_nsa_sample_body, layer=layer, n_pages=n_pages, tk=tk),
        grid_spec=pltpu.PrefetchScalarGridSpec(
            num_scalar_prefetch=1,
            grid=(bs,),
            in_specs=[pl.BlockSpec((slope_rows.shape[0], 1), lambda i, pt: (0, 0)),
                      pl.BlockSpec((None, ts, qd), lambda i, pt: (i, 0, 0)),
                      pl.BlockSpec((None, ts, LANE), lambda i, pt: (i, 0, (qd + 6 * gd) // LANE)),
                      pl.BlockSpec((None, 2, nc, gd), lambda i, pt: (i, 0, 0, 0)),
                      pl.BlockSpec((None, ts, 2 * gd), lambda i, pt: (i, 0, (qd + 2 * gd) // (2 * gd))),
                      pl.BlockSpec((None, ts, 2 * gd), lambda i, pt: (i, 0, (qd + 4 * gd) // (2 * gd))),
                      pl.BlockSpec((None, None, 2 * gd, wp), lambda i, pt: (layer, i, 0, 0)),
                      pl.BlockSpec(memory_space=pl.ANY)],
            out_specs=pl.BlockSpec((None, ts, qd), lambda i, pt: (i, 0, 0)),
            scratch_shapes=[pltpu.VMEM((2, n_pages, 2 * gd, PAGE_SIZE), F32), pltpu.SemaphoreType.DMA((2,))]),
        out_shape=jax.ShapeDtypeStruct((bs, ts, qd), F32),
        compiler_params=_params("arbitrary"),
        name="nsa_sample_attn",
    )(page_table, slope_rows, proj3, proj3, kcvc, proj3, proj3, win_t, cache_t)


def _moba_prompt_body(alibi_ref, q_ref, kf_ref, k_ref, v_ref, o_ref, kmean_sc, rhs_sc, m_sc, l_sc, acc_sc, o_sc,
                      *, tq, tk):
    i = pl.program_id(1)
    g_n, r_n, hd = MOBA_KV_HEADS, MOBA_GROUP, HEAD_DIM
    rows = r_n * tq
    nb = kmean_sc.shape[0]

    @pl.when(i == 0)
    def _():
        kmean_sc[...] = jnp.concatenate(
            [jnp.mean(kf_ref[n * MOBA_BLOCK:(n + 1) * MOBA_BLOCK, :], axis=0, keepdims=True) for n in range(nb)],
            axis=0)

    t_q = i * tq + lax.broadcasted_iota(jnp.int32, (1, tq), 1)
    cur = i * tq // MOBA_BLOCK
    blk = lax.broadcasted_iota(jnp.int32, (nb, 1), 0)
    q_t = q_ref[...].T
    own0 = pl.multiple_of(cur * MOBA_BLOCK, MOBA_BLOCK)
    own_pos = own0 + lax.broadcasted_iota(jnp.int32, (MOBA_BLOCK, 1), 0)
    causal = _tile_lanes(jnp.where(own_pos <= t_q, 0.0, NEG_INF), r_n)
    n_past = (cur * MOBA_BLOCK + tk - 1) // tk
    no_sel = jnp.zeros((AUG_OH, rows), F32)

    for g in range(g_n):
        gsl = slice(g * hd, (g + 1) * hd)
        asl = slice(g * AUG, (g + 1) * AUG)
        qg = _group_q_t(q_t, g, r_n)
        gs = jnp.dot(kmean_sc[:, gsl], qg, preferred_element_type=F32, precision=HIGHEST)
        sel = _topk_mask(gs, blk < cur, min(MOBA_TOPK, nb), 0)
        tail = _aug_tail(qg * (QK_SCALE * LOG2E), alibi_ref[g])
        rhs_sc[g] = _aug_rhs(1.0 - sel, tail)

        s = jnp.dot(k_ref[pl.ds(own0, MOBA_BLOCK), asl], _aug_rhs(no_sel, tail),
                    preferred_element_type=F32) + causal
        m_sc[g], l_sc[g], acc_sc[g] = _soft_first(s, v_ref[gsl, pl.ds(own0, MOBA_BLOCK)])

    def past(j, carry):
        key0 = pl.multiple_of(j * tk, tk)
        for g in range(g_n):
            m_sc[g], l_sc[g], acc_sc[g] = _soft_more(
                (m_sc[g], l_sc[g], acc_sc[g]), k_ref, v_ref, slice(g * AUG, (g + 1) * AUG),
                slice(g * hd, (g + 1) * hd), key0, tk, rhs_sc[g])
        return carry

    lax.fori_loop(0, n_past, past, 0)

    for g in range(g_n):
        o = acc_sc[g] / l_sc[g]
        for r in range(r_n):
            h = g * r_n + r
            o_sc[h * hd:(h + 1) * hd, :] = o[:, r * tq:(r + 1) * tq]
    o_ref[...] = o_sc[...].T


def _moba_prompt_attn(proj3, k_aug, v_t, alibi_rows, tq, tk=512):
    b, t, _ = proj3.shape
    g_n = MOBA_KV_HEADS
    gd = g_n * HEAD_DIM
    qd = g_n * MOBA_GROUP * HEAD_DIM
    rows = MOBA_GROUP * tq
    assert t % tk == 0 and tk % MOBA_BLOCK == 0 and MOBA_BLOCK % tq == 0 and t // MOBA_BLOCK <= AUG_OH
    return pl.pallas_call(
        functools.partial(_moba_prompt_body, tq=tq, tk=tk),
        grid=(b, t // tq),
        in_specs=[pl.BlockSpec((g_n, AUG_AL, rows), lambda bi, i: (0, 0, 0)),
                  pl.BlockSpec((None, tq, qd), lambda bi, i: (bi, i, 0)),
                  pl.BlockSpec((None, t, gd), lambda bi, i: (bi, 0, qd // gd)),
                  pl.BlockSpec((None, t, g_n * AUG), lambda bi, i: (bi, 0, 0)),
                  pl.BlockSpec((None, gd, t), lambda bi, i: (bi, 0, 0))],
        out_specs=pl.BlockSpec((None, tq, qd), lambda bi, i: (bi, i, 0)),
        out_shape=jax.ShapeDtypeStruct((b, t, qd), F32),
        scratch_shapes=[pltpu.VMEM((t // MOBA_BLOCK, gd), F32), pltpu.VMEM((g_n, AUG, rows), BF16),
                        pltpu.VMEM((g_n, 1, rows), F32), pltpu.VMEM((g_n, 1, rows), F32),
                        pltpu.VMEM((g_n, HEAD_DIM, rows), F32), pltpu.VMEM((qd, tq), F32)],
        compiler_params=_params("parallel", "arbitrary"),
        name="moba_prompt_attn",
    )(alibi_rows, proj3, proj3, k_aug, v_t)


def _moba_sample_body(pt_ref, slope_ref, q_ref, new_ref, cache_hbm, o_ref, buf, sem, *, layer, n_pages, tk):
    g_n, r_n, hd = MOBA_KV_HEADS, MOBA_GROUP, HEAD_DIM
    gd = g_n * hd
    ts = q_ref.shape[0]
    rows = g_n * r_n * ts
    past = n_pages * PAGE_SIZE
    nb = past // MOBA_BLOCK
    b = pl.program_id(0)
    slot = b % 2
    gather = lambda seq, sl: _gather_pages(pt_ref, seq, cache_hbm, layer, 0, 2 * gd, buf.at[sl], sem.at[sl],
                                           n_pages)

    @pl.when(b == 0)
    def _():
        gather(b, slot)[0]()

    @pl.when(b + 1 < pl.num_programs(0))
    def _():
        gather(b + 1, 1 - slot)[0]()

    qf = _block_diag_q(q_ref, g_n, r_n)
    qb = (qf * QK_SCALE).astype(BF16)
    slope = slope_ref[...]
    t = past + lax.broadcasted_iota(jnp.int32, (rows, 1), 0) % ts
    new_pos = past + lax.broadcasted_iota(jnp.int32, (1, ts), 1)
    gather(b, slot)[1]()

    blk = lax.broadcasted_iota(jnp.int32, (gd, nb), 1)
    kmean_t = jnp.zeros((gd, nb), F32)
    pages = MOBA_BLOCK // PAGE_SIZE
    for n in range(nb):
        col = sum(jnp.sum(buf[slot, n * pages + u, :gd, :], axis=1, keepdims=True) for u in range(pages))
        kmean_t = jnp.where(blk == n, col / MOBA_BLOCK, kmean_t)
    gs = jnp.dot(qf, kmean_t, preferred_element_type=F32, precision=HIGHEST)
    sel = _topk_mask(gs, jnp.full(gs.shape, True), min(MOBA_TOPK, nb), 1).astype(BF16)

    def step(j, carry):
        key0 = pl.multiple_of(j * tk, tk)
        kpos = key0 + lax.broadcasted_iota(jnp.int32, (1, tk), 1)
        in_sel = _block_mask(sel, key0, tk, MOBA_BLOCK.bit_length() - 1)
        return _softmax_step(carry, qb, _page_tile(buf, slot, j, tk, 0, gd), _page_tile(buf, slot, j, tk, gd, 2 * gd),
                             t, slope, kpos, lambda dist: in_sel, True)

    carry = lax.fori_loop(0, past // tk, step, _softmax_init(rows, gd))
    carry = _softmax_step(carry, qb, new_ref[:, :gd].astype(BF16), new_ref[:, gd:].astype(BF16),
                          t, slope, new_pos, lambda dist: dist >= 0, False)
    o = _softmax_done(carry)
    for h in range(g_n * r_n):
        g = h // r_n
        o_ref[:, h * hd:(h + 1) * hd] = o[h * ts:(h + 1) * ts, g * hd:(g + 1) * hd]


def _moba_sample_attn(proj3, cache_t, layer, page_table, slopes, tk=512):
    bs, ts, _ = proj3.shape
    n_pages = page_table.shape[1]
    gd = MOBA_KV_HEADS * HEAD_DIM
    qd = MOBA_KV_HEADS * MOBA_GROUP * HEAD_DIM
    slope_rows = jnp.repeat(slopes, ts).reshape(-1, 1)
    return pl.pallas_call(
        functools.partial(_moba_sample_body, layer=layer, n_pages=n_pages, tk=tk),
        grid_spec=pltpu.PrefetchScalarGridSpec(
            num_scalar_prefetch=1,
            grid=(bs,),
            in_specs=[pl.BlockSpec((slope_rows.shape[0], 1), lambda i, pt: (0, 0)),
                      pl.BlockSpec((None, ts, qd), lambda i, pt: (i, 0, 0)),
                      pl.BlockSpec((None, ts, 2 * gd), lambda i, pt: (i, 0, qd // (2 * gd))),
                      pl.BlockSpec(memory_space=pl.ANY)],
            out_specs=pl.BlockSpec((None, ts, qd), lambda i, pt: (i, 0, 0)),
            scratch_shapes=[pltpu.VMEM((2, n_pages, 2 * gd, PAGE_SIZE), F32), pltpu.SemaphoreType.DMA((2,))]),
        out_shape=jax.ShapeDtypeStruct((bs, ts, qd), F32),
        compiler_params=_params("arbitrary"),
        name="moba_sample_attn",
    )(page_table, slope_rows, proj3, proj3, cache_t)


PROMPT_Q_TILE = 128


def _alibi_slopes(n_heads):
    return jnp.exp2(-8.0 * jnp.arange(1, n_heads + 1, dtype=F32) / n_heads)


def _alibi_query_rows(slopes, g_n, r_n, tq):
    s2 = slopes * LOG2E
    d0 = s2.astype(BF16).astype(F32)
    d1 = (s2 - d0).astype(BF16).astype(F32)
    d2 = (s2 - d0 - d1).astype(BF16).astype(F32)
    zero = jnp.zeros_like(s2)
    digits = jnp.stack([d0, d1, d2, POS_DIGIT * d0, POS_DIGIT * d1, POS_DIGIT * d2, zero, zero], axis=0)
    lanes = lambda a: jnp.repeat(a.reshape(-1, g_n, r_n), tq, axis=2).transpose(1, 0, 2)
    return lanes(s2[None]), lanes(digits)


def _positions_minor(a, n_lead):
    nd = a.ndim
    a = jnp.transpose(a, tuple(range(n_lead)) + tuple(range(n_lead + 1, nd)) + (n_lead,))
    return a.reshape(a.shape[:n_lead] + (-1, a.shape[-1]))


def _nsa_layer(xp, xs, bp, bs, cache_t, win_t, layer, win_state, page_table, w_in, w_out, pe, w1, w2, g, b,
               alpha):
    tp, ts = xp.shape[0] // bp, xs.shape[0] // bs
    g_n, hd = NSA_KV_HEADS, HEAD_DIM
    gd = g_n * hd
    qd = g_n * NSA_GROUP * hd
    n_in = w_in.shape[1]
    n_pad = -(-n_in // LANE) * LANE
    w_in_bf = jnp.pad(w_in, ((0, 0), (0, n_pad - n_in))).astype(BF16)
    pe, w1_bf, w2_bf = _compress_weights(pe, w1, w2)
    w_out_bf = w_out.astype(BF16)
    slopes = _alibi_slopes(g_n * NSA_GROUP)
    past = page_table.shape[1] * PAGE_SIZE
    assert PAGE_SIZE == LANE and past % (2 * CMP_BLOCK) == 0 and ts <= CMP_BLOCK

    assert tp <= POS_DIGIT * 256
    proj_p, k_slc, k_win, v_slc_t, v_win_t = _proj(
        xp, w_in_bf, tp, keys=((qd + 2 * gd, SEL_BLOCK, NSA_PAST_TILE // SEL_BLOCK), (qd + 4 * gd, SEL_BLOCK, 0)),
        values=(qd + 3 * gd, qd + 5 * gd), g_n=g_n)
    (proj_s,) = _proj(xs, w_in_bf)
    proj_p3 = proj_p.reshape(bp, tp, n_pad)
    proj_s3 = proj_s.reshape(bs, ts, n_pad)
    kcvc_p = _compress_prompt(proj_p3, qd, pe, w1_bf, w2_bf)
    slope_rows, alibi_rows = _alibi_query_rows(slopes, g_n, NSA_GROUP, PROMPT_Q_TILE)
    o_p = _nsa_prompt_attn(proj_p3, k_slc.reshape(bp, tp, g_n * AUG), v_slc_t, k_win.reshape(bp, tp, g_n * AUG),
                           v_win_t, kcvc_p, slope_rows, alibi_rows, PROMPT_Q_TILE)
    kcvc_s = _compress_sample(cache_t, layer, page_table, pe, w1_bf, w2_bf)
    o_s = _nsa_sample_attn(proj_s3, kcvc_s, win_t, cache_t, layer, page_table, slopes)
    xp = _out_ln(o_p.reshape(bp * tp, qd), xp, w_out_bf, g, b, alpha)
    xs = _out_ln(o_s.reshape(bs * ts, qd), xs, w_out_bf, g, b, alpha)

    kv_shape = (4, g_n, hd)
    win_shape = (2, g_n, hd)
    kv_p = proj_p3[:, :, qd:qd + 4 * gd].reshape((bp, tp) + kv_shape)
    kv_s = proj_s3[:, :, qd:qd + 4 * gd].reshape((bs, ts) + kv_shape)
    win_p = proj_p3[:, tp - min(WINDOW, tp):, qd + 4 * gd:qd + 6 * gd].reshape((bp, min(WINDOW, tp)) + win_shape)
    win_s = jnp.concatenate([win_state, proj_s3[:, :, qd + 4 * gd:qd + 6 * gd].reshape((bs, ts) + win_shape)], axis=1)
    win_s = win_s[:, win_s.shape[1] - min(WINDOW, win_s.shape[1]):]
    return xp, xs, kv_p, kv_s, win_p, win_s


def _pool_layer(xp, xs, bp, bs, state, past, w, scale, g, b, alpha):
    d = xp.shape[1]
    tp, ts = xp.shape[0] // bp, xs.shape[0] // bs
    assert POOL_HALO % ts == 0 and tp % POOL_HALO == 0
    w_bf = w.astype(BF16)
    scale = scale.reshape(1, d)
    xs3 = xs.reshape(bs, ts, d)
    xe_s = jnp.concatenate([jnp.zeros((bs, POOL_HALO - POOL_PAST, d), F32), state, xs3], axis=1)
    pool_p = xp.reshape(bp, tp, d)[:, tp - POOL_PAST:]
    pool_s = xe_s[:, xe_s.shape[1] - POOL_PAST:]
    xp = _pool_ln_prompt(xp, tp, w_bf, scale, g, b, alpha)
    xs = _pool_ln_sample(xe_s, ts, past, w_bf, scale, g, b, alpha)
    return xp, xs, pool_p, pool_s


def _moba_layer(xp, xs, bp, bs, cache_t, layer, page_table, w_in, w_out, g, b, alpha):
    tp, ts = xp.shape[0] // bp, xs.shape[0] // bs
    g_n, hd = MOBA_KV_HEADS, HEAD_DIM
    gd = g_n * hd
    qd = g_n * MOBA_GROUP * hd
    n_in = w_in.shape[1]
    w_in_bf, w_out_bf = w_in.astype(BF16), w_out.astype(BF16)
    slopes = _alibi_slopes(g_n * MOBA_GROUP)
    past = page_table.shape[1] * PAGE_SIZE
    assert PAGE_SIZE == LANE and past % MOBA_BLOCK == 0 and ts <= MOBA_BLOCK

    assert tp <= POS_DIGIT * 256
    proj_p, k_aug, v_t = _proj(xp, w_in_bf, tp, keys=((qd, MOBA_BLOCK, AUG_OH),), values=(qd + gd,), g_n=g_n)
    (proj_s,) = _proj(xs, w_in_bf)
    proj_p3 = proj_p.reshape(bp, tp, n_in)
    proj_s3 = proj_s.reshape(bs, ts, n_in)
    _, alibi_rows = _alibi_query_rows(slopes, g_n, MOBA_GROUP, PROMPT_Q_TILE)
    o_p = _moba_prompt_attn(proj_p3, k_aug.reshape(bp, tp, g_n * AUG), v_t, alibi_rows, PROMPT_Q_TILE)
    o_s = _moba_sample_attn(proj_s3, cache_t, layer, page_table, slopes)
    xp = _out_ln(o_p.reshape(bp * tp, qd), xp, w_out_bf, g, b, alpha)
    xs = _out_ln(o_s.reshape(bs * ts, qd), xs, w_out_bf, g, b, alpha)
    kv_shape = (2, g_n, hd)
    kv_p = proj_p3[:, :, qd:].reshape((bp, tp) + kv_shape)
    kv_s = proj_s3[:, :, qd:].reshape((bs, ts) + kv_shape)
    return xp, xs, kv_p, kv_s


def kernel(x_prompt, x_sample, cache_nsa_kv, state_nsa_win, state_pool, cache_moba, page_table, ln_g, ln_b, mlp_w1, mlp_w2, nsa_w_in, nsa_w_out, nsa_cmp_pe, nsa_cmp_w1, nsa_cmp_w2, pool_w, pool_scale, moba_w_in, moba_w_out):
    bp, tp, d = x_prompt.shape
    bs, ts, _ = x_sample.shape
    depth = ln_g.shape[0]
    alpha = (2 * depth) ** 0.25
    past = page_table.shape[1] * PAGE_SIZE
    xp = x_prompt.reshape(bp * tp, d)
    xs = x_sample.reshape(bs * ts, d)
    nsa_cache_t = _positions_minor(cache_nsa_kv, 2)
    nsa_win_t = _positions_minor(state_nsa_win, 2)
    moba_cache_t = _positions_minor(cache_moba, 2)
    outs = {k: [] for k in ("nsa_kv_p", "nsa_kv_s", "nsa_win_p", "nsa_win_s", "pool_p", "pool_s", "moba_p", "moba_s")}
    for i in range(depth):
        kind, j = i % N_MIXERS, i // N_MIXERS
        g0, b0 = ln_g[i, 0].reshape(1, d), ln_b[i, 0].reshape(1, d)
        g1, b1 = ln_g[i, 1].reshape(1, d), ln_b[i, 1].reshape(1, d)
        if kind == 0:
            xp, xs, kv_p, kv_s, win_p, win_s = _nsa_layer(
                xp, xs, bp, bs, nsa_cache_t, nsa_win_t, j, state_nsa_win[j], page_table, nsa_w_in[j], nsa_w_out[j],
                nsa_cmp_pe[j], nsa_cmp_w1[j], nsa_cmp_w2[j], g0, b0, alpha)
            outs["nsa_kv_p"].append(kv_p); outs["nsa_kv_s"].append(kv_s)
            outs["nsa_win_p"].append(win_p); outs["nsa_win_s"].append(win_s)
        elif kind == 1:
            xp, xs, pool_p, pool_s = _pool_layer(xp, xs, bp, bs, state_pool[j], past, pool_w[j], pool_scale[j],
                                                 g0, b0, alpha)
            outs["pool_p"].append(pool_p); outs["pool_s"].append(pool_s)
        else:
            xp, xs, kv_p, kv_s = _moba_layer(xp, xs, bp, bs, moba_cache_t, j, page_table, moba_w_in[j],
                                             moba_w_out[j], g0, b0, alpha)
            outs["moba_p"].append(kv_p); outs["moba_s"].append(kv_s)
        w1_bf, w2_bf = mlp_w1[i].astype(BF16), mlp_w2[i].astype(BF16)
        xp = _mlp_ln(xp, w1_bf, w2_bf, g1, b1, alpha)
        xs = _mlp_ln(xs, w1_bf, w2_bf, g1, b1, alpha)
    return (xp.reshape(bp, tp, d), xs.reshape(bs, ts, d),
            jnp.stack(outs["nsa_kv_p"]), jnp.stack(outs["nsa_kv_s"]),
            jnp.stack(outs["nsa_win_p"]), jnp.stack(outs["nsa_win_s"]),
            jnp.stack(outs["pool_p"]), jnp.stack(outs["pool_s"]),
            jnp.stack(outs["moba_p"]), jnp.stack(outs["moba_s"]))
```

```python
import functools

import jax
import jax.numpy as jnp
import numpy as np
from jax import lax
from jax.experimental import pallas as pl
from jax.experimental.pallas import tpu as pltpu

F32 = jnp.float32
BF16 = jnp.bfloat16

HEAD_DIM = 64
PAGE_SIZE = 128
N_MIXERS = 3
NSA_KV_HEADS = 4
NSA_GROUP = 4
CMP_BLOCK = 32
SEL_BLOCK = 64
N_SEL = 16
WINDOW = 512
CMP_HIDDEN = 2 * HEAD_DIM
POOL_WINDOWS = (2, 4, 8, 16)
POOL_PAST = max(POOL_WINDOWS) - 1
POOL_HALO = 16
MOBA_KV_HEADS = 4
MOBA_GROUP = 4
MOBA_BLOCK = 256
MOBA_TOPK = 3
LN_EPS = 1e-5
NEG_INF = -1e30
QK_SCALE = HEAD_DIM ** -0.5
LOG2E = 1.4426950408889634

LANE = 128
VMEM_LIMIT = 56 * 1024 * 1024
NT_DIMS = (((1,), (1,)), ((), ()))
HIGHEST = lax.Precision.HIGHEST

AUG = LANE
AUG_OH = 16
AUG_AL = 8
AUG_AL0 = HEAD_DIM + AUG_OH
POS_DIGIT = 16
NSA_PAST_TILE = 8 * SEL_BLOCK


def _params(*sem):
    return pltpu.CompilerParams(dimension_semantics=sem, vmem_limit_bytes=VMEM_LIMIT)


def _layer_norm(y, g, b):
    mu = jnp.mean(y, axis=-1, keepdims=True)
    d = y - mu
    var = jnp.mean(d * d, axis=-1, keepdims=True)
    return d * lax.rsqrt(var + LN_EPS) * g + b


def _row_tile(m, want):
    t = min(m, want)
    while m % t:
        t //= 2
    return t


def _proj_body(x_ref, w_ref, o_ref, *aug_refs, keys, values, tm, seq_len, g_n):
    acc = jnp.dot(x_ref[...].astype(BF16), w_ref[...], preferred_element_type=F32)
    o_ref[...] = acc
    if not aug_refs:
        return
    pos = (pl.program_id(0) * tm) % seq_len + lax.broadcasted_iota(jnp.int32, (tm, AUG), 0)
    lane = lax.broadcasted_iota(jnp.int32, (tm, AUG), 1)
    lo_digit = (lane >= AUG_AL0) & (lane < AUG_AL0 + 3)
    hi_digit = (lane >= AUG_AL0 + 3) & (lane < AUG_AL0 + 6)
    digits = jnp.where(lo_digit, pos % POS_DIGIT, jnp.where(hi_digit, pos // POS_DIGIT, 0)).astype(F32)
    for k_ref, (col0, blk_len, n_onehot) in zip(aug_refs, keys):
        extra = digits
        if n_onehot:
            extra = jnp.where(lane - HEAD_DIM == (pos // blk_len) % n_onehot, NEG_INF, digits)
        for g in range(g_n):
            col = col0 + g * HEAD_DIM
            piece = acc[:, col // LANE * LANE:(col // LANE + 1) * LANE]
            if col % LANE:
                piece = pltpu.roll(piece, LANE - col % LANE, 1)
            k_ref[:, g * AUG:(g + 1) * AUG] = jnp.where(lane < HEAD_DIM, piece, extra).astype(BF16)
    for v_ref, col0 in zip(aug_refs[len(keys):], values):
        v_ref[...] = acc[:, col0:col0 + g_n * HEAD_DIM].T.astype(BF16)


def _proj(x, w_bf, seq_len=None, keys=(), values=(), g_n=0):
    m, d = x.shape
    n = w_bf.shape[1]
    tm = _row_tile(seq_len or m, 512)
    per_seq = (seq_len or m) // tm
    gd = g_n * HEAD_DIM
    out_specs = [pl.BlockSpec((tm, n), lambda i: (i, 0))]
    out_shape = [jax.ShapeDtypeStruct((m, n), F32)]
    out_specs += [pl.BlockSpec((tm, g_n * AUG), lambda i: (i, 0)) for _ in keys]
    out_shape += [jax.ShapeDtypeStruct((m, g_n * AUG), BF16) for _ in keys]
    out_specs += [pl.BlockSpec((None, gd, tm), lambda i: (i // per_seq, 0, i % per_seq)) for _ in values]
    out_shape += [jax.ShapeDtypeStruct((m // (seq_len or m), gd, seq_len or m), BF16) for _ in values]
    return pl.pallas_call(
        functools.partial(_proj_body, keys=keys, values=values, tm=tm, seq_len=seq_len, g_n=g_n),
        grid=(m // tm,),
        in_specs=[pl.BlockSpec((tm, d), lambda i: (i, 0)),
                  pl.BlockSpec((d, n), lambda i: (0, 0))],
        out_specs=out_specs,
        out_shape=out_shape,
        compiler_params=_params("parallel"),
        name="proj",
    )(x, w_bf)


def _out_ln_body(o_ref, x_ref, w_ref, g_ref, b_ref, y_ref, *, alpha):
    h = jnp.dot(o_ref[...].astype(BF16), w_ref[...], preferred_element_type=F32)
    y_ref[...] = _layer_norm(alpha * x_ref[...] + h, g_ref[...], b_ref[...])


def _out_ln(o, x, w_bf, g, b, alpha):
    m, d = x.shape
    k = o.shape[1]
    tm = _row_tile(m, 512)
    return pl.pallas_call(
        functools.partial(_out_ln_body, alpha=alpha),
        grid=(m // tm,),
        in_specs=[pl.BlockSpec((tm, k), lambda i: (i, 0)),
                  pl.BlockSpec((tm, d), lambda i: (i, 0)),
                  pl.BlockSpec((k, d), lambda i: (0, 0)),
                  pl.BlockSpec((1, d), lambda i: (0, 0)),
                  pl.BlockSpec((1, d), lambda i: (0, 0))],
        out_specs=pl.BlockSpec((tm, d), lambda i: (i, 0)),
        out_shape=jax.ShapeDtypeStruct((m, d), F32),
        compiler_params=_params("parallel"),
        name="out_ln",
    )(o, x, w_bf, g, b)


MLP_F_CHUNK = 1024


def _mlp_ln_body(x_ref, w1_ref, w2_ref, g_ref, b_ref, y_ref, *, alpha):
    x = x_ref[...]
    xb = x.astype(BF16)
    acc = None
    for c in range(w1_ref.shape[1] // MLP_F_CHUNK):
        fs = slice(c * MLP_F_CHUNK, (c + 1) * MLP_F_CHUNK)
        h = jnp.maximum(jnp.dot(xb, w1_ref[:, fs], preferred_element_type=F32), 0.0)
        part = jnp.dot((h * h).astype(BF16), w2_ref[fs, :], preferred_element_type=F32)
        acc = part if acc is None else acc + part
    y_ref[...] = _layer_norm(alpha * x + acc, g_ref[...], b_ref[...])


def _mlp_ln(x, w1_bf, w2_bf, g, b, alpha):
    m, d = x.shape
    f = w1_bf.shape[1]
    tm = _row_tile(m, 512)
    once = pl.Buffered(1)
    return pl.pallas_call(
        functools.partial(_mlp_ln_body, alpha=alpha),
        grid=(m // tm,),
        in_specs=[pl.BlockSpec((tm, d), lambda i: (i, 0)),
                  pl.BlockSpec((d, f), lambda i: (0, 0), pipeline_mode=once),
                  pl.BlockSpec((f, d), lambda i: (0, 0), pipeline_mode=once),
                  pl.BlockSpec((1, d), lambda i: (0, 0)),
                  pl.BlockSpec((1, d), lambda i: (0, 0))],
        out_specs=pl.BlockSpec((tm, d), lambda i: (i, 0)),
        out_shape=jax.ShapeDtypeStruct((m, d), F32),
        compiler_params=_params("parallel"),
        name="mlp_ln",
    )(x, w1_bf, w2_bf, g, b)


def _pool_ln_body(halo_ref, x_ref, w_ref, scale_ref, g_ref, b_ref, y_ref, *, alpha, pos0, tiles_per_seq):
    tm, d = x_ref.shape
    grp = d // len(POOL_WINDOWS)
    seq_tile = pl.program_id(0) % tiles_per_seq
    x = x_ref[...]
    halo = halo_ref[...]
    if pos0 == 0:
        halo = jnp.where(seq_tile == 0, 0.0, halo)
    xe = jnp.concatenate([halo, x], axis=0)
    n_avail = pos0 + seq_tile * tm + lax.broadcasted_iota(jnp.int32, (tm, 1), 0) + 1
    mixed = []
    for gi, w in enumerate(POOL_WINDOWS):
        s = xe[:, gi * grp:(gi + 1) * grp]
        step = 1
        while step < w:
            s = s + pltpu.roll(s, step, 0)
            step *= 2
        cnt = jnp.minimum(n_avail, w).astype(F32)
        pooled = s[POOL_HALO:] / cnt - x[:, gi * grp:(gi + 1) * grp]
        mixed.append(jnp.dot(pooled.astype(BF16), w_ref[gi], preferred_element_type=F32))
    h = jnp.concatenate(mixed, axis=-1) * scale_ref[...]
    y_ref[...] = _layer_norm(alpha * x + h, g_ref[...], b_ref[...])


def _pool_ln(rows, halo_spec, x_spec, n_tiles, tm, tiles_per_seq, pos0, w_bf, scale, g, b, alpha):
    d = rows.shape[-1]
    ngrp = len(POOL_WINDOWS)
    const = lambda i: (0, 0)
    return pl.pallas_call(
        functools.partial(_pool_ln_body, alpha=alpha, pos0=pos0, tiles_per_seq=tiles_per_seq),
        grid=(n_tiles,),
        in_specs=[halo_spec, x_spec,
                  pl.BlockSpec((ngrp, d // ngrp, d // ngrp), lambda i: (0, 0, 0)),
                  pl.BlockSpec((1, d), const), pl.BlockSpec((1, d), const), pl.BlockSpec((1, d), const)],
        out_specs=pl.BlockSpec((tm, d), lambda i: (i, 0)),
        out_shape=jax.ShapeDtypeStruct((n_tiles * tm, d), F32),
        compiler_params=_params("parallel"),
        name="pool_ln",
    )(rows, rows, w_bf, scale, g, b)


def _pool_ln_prompt(x2d, seq_len, w_bf, scale, g, b, alpha):
    m, d = x2d.shape
    tm = _row_tile(seq_len, 512)
    per = tm // POOL_HALO
    halo_spec = pl.BlockSpec((POOL_HALO, d), lambda i: (jnp.maximum(i * per - 1, 0), 0))
    x_spec = pl.BlockSpec((tm, d), lambda i: (i, 0))
    return _pool_ln(x2d, halo_spec, x_spec, m // tm, tm, seq_len // tm, 0, w_bf, scale, g, b, alpha)


def _pool_ln_sample(xe3, ts, pos0, w_bf, scale, g, b, alpha):
    bs, _, d = xe3.shape
    halo_spec = pl.BlockSpec((None, POOL_HALO, d), lambda i: (i, 0, 0))
    x_spec = pl.BlockSpec((None, ts, d), lambda i: (i, POOL_HALO // ts, 0))
    return _pool_ln(xe3, halo_spec, x_spec, bs, ts, 1, pos0, w_bf, scale, g, b, alpha)


def _topk_mask(vals, valid, k, axis):
    n = vals.shape[axis]
    idx = lax.broadcasted_iota(jnp.int32, vals.shape, axis).astype(F32)
    work = jnp.where(valid, vals, -jnp.inf)
    sel = jnp.zeros(vals.shape, F32)
    for _ in range(k):
        m = jnp.max(work, axis=axis, keepdims=True)
        first = jnp.min(jnp.where(work == m, idx, float(n)), axis=axis, keepdims=True)
        pick = idx == first
        sel = jnp.where(pick, 1.0, sel)
        work = jnp.where(pick, -jnp.inf, work)
    return jnp.where(valid, sel, 0.0)


def _cmp_positions(nc, axis):
    shape = (nc, 1) if axis == 0 else (1, nc)
    i = lax.broadcasted_iota(jnp.int32, shape, axis)
    half = nc // 2
    n = jnp.where(i < half, 2 * i, 2 * (i - half) + 1)
    c_end = n * CMP_BLOCK + (CMP_BLOCK - 1)
    c_mid = c_end.astype(F32) - 0.5 * (CMP_BLOCK - 1)
    return c_end, c_mid


def _softmax_init(rows, vdim):
    return (jnp.full((rows, 1), NEG_INF, F32), jnp.zeros((rows, 1), F32), jnp.zeros((rows, vdim), F32))


def _softmax_step(carry, qb, k, v, t, slope, kpos, mask_fn, kv_t):
    m, l, acc = carry
    if kv_t:
        s = jnp.dot(qb, k, preferred_element_type=F32)
    else:
        s = lax.dot_general(qb, k, NT_DIMS, preferred_element_type=F32)
    dist = t - kpos
    s = s - slope * jnp.abs(dist).astype(F32)
    mask = mask_fn(dist)
    s = jnp.where(mask, s, NEG_INF)
    m_new = jnp.maximum(m, jnp.max(s, axis=-1, keepdims=True))
    a = jnp.exp(m - m_new)
    p = jnp.where(mask, jnp.exp(s - m_new), 0.0).astype(BF16)
    l = a * l + jnp.sum(p.astype(F32), axis=-1, keepdims=True)
    if kv_t:
        pv = lax.dot_general(p, v, NT_DIMS, preferred_element_type=F32)
    else:
        pv = jnp.dot(p, v, preferred_element_type=F32)
    return m_new, l, a * acc + pv


def _softmax_done(carry):
    _, l, acc = carry
    return acc / jnp.where(l > 0.0, l, 1.0)


def _block_mask(sel_bf, key0, tk, blk_shift):
    nblk = sel_bf.shape[1]
    kblk = (key0 + lax.broadcasted_iota(jnp.int32, (nblk, tk), 1)) >> blk_shift
    expand = jnp.where(kblk == lax.broadcasted_iota(jnp.int32, (nblk, tk), 0), 1.0, 0.0).astype(BF16)
    return jnp.dot(sel_bf, expand, preferred_element_type=F32) > 0.5


def _cmp_attention(q, kc, vc, t, slope):
    nc = kc.shape[0]
    c_end, c_mid = _cmp_positions(nc, 1)
    s = lax.dot_general(q, kc, NT_DIMS, preferred_element_type=F32, precision=HIGHEST)
    s = s - slope * jnp.abs(t.astype(F32) - c_mid)
    mask = c_end <= t
    s = jnp.where(mask, s, NEG_INF)
    m = jnp.max(s, axis=-1, keepdims=True)
    p = jnp.where(mask, jnp.exp(s - m), 0.0)
    l = jnp.sum(p, axis=-1, keepdims=True)
    p = p / jnp.where(l > 0.0, l, 1.0)
    o = jnp.dot(p.astype(BF16), vc.astype(BF16), preferred_element_type=F32)
    return p, o


def _tile_lanes(x, n):
    return jnp.concatenate([x] * n, axis=1)


def _group_q_t(q_t, g, r_n):
    hd = HEAD_DIM
    return jnp.concatenate([q_t[(g * r_n + r) * hd:(g * r_n + r + 1) * hd] for r in range(r_n)], axis=1)


def _aug_tail(q_t, alibi):
    pad = jnp.zeros((AUG - AUG_OH - HEAD_DIM - AUG_AL, q_t.shape[1]), F32)
    return jnp.concatenate([q_t, alibi, pad], axis=0).astype(BF16)


def _aug_rhs(not_sel, tail):
    pad_oh = AUG_OH - not_sel.shape[0]
    if pad_oh:
        not_sel = jnp.concatenate([not_sel, jnp.zeros((pad_oh, not_sel.shape[1]), F32)], axis=0)
    return jnp.concatenate([tail[:HEAD_DIM], not_sel.astype(BF16), tail[HEAD_DIM:]], axis=0)


def _soft_first(s, v_t):
    m = jnp.max(s, axis=0, keepdims=True)
    p = jnp.exp2(s - m)
    return m, jnp.sum(p, axis=0, keepdims=True), jnp.dot(v_t, p.astype(BF16), preferred_element_type=F32)


def _soft_more_groups(m_sc, l_sc, acc_sc, k_ref, v_ref, key0, tk, rhs_of):
    g_n, hd = m_sc.shape[0], HEAD_DIM
    keys = pl.ds(key0, tk)
    s, m_new = [None] * g_n, [None] * g_n

    def logits(g):
        s[g] = jnp.dot(k_ref[keys, g * AUG:(g + 1) * AUG], rhs_of(g), preferred_element_type=F32)
        m_new[g] = jnp.maximum(m_sc[g], jnp.max(s[g], axis=0, keepdims=True))

    logits(0)
    for g in range(g_n):
        if g + 1 < g_n:
            logits(g + 1)
        a = jnp.exp2(m_sc[g] - m_new[g])
        ref = m_new[g]
        if g + 1 < g_n:
            ref = ref + jnp.max(s[g + 1][0:8], axis=0, keepdims=True) * 0.0
        p = jnp.exp2(s[g] - ref)
        l_sc[g] = a * l_sc[g] + jnp.sum(p, axis=0, keepdims=True)
        acc_sc[g] = a * acc_sc[g] + jnp.dot(v_ref[g * hd:(g + 1) * hd, keys], p.astype(BF16),
                                            preferred_element_type=F32)
        m_sc[g] = m_new[g]


def _soft_done(carry):
    _, l, acc = carry
    return acc / l


HEADS_PER_LANE_TILE = LANE // HEAD_DIM


def _compress_rows(block_rows, pe_ref, w1_ref, w2_ref, nc):
    acc = jnp.zeros((nc, HEADS_PER_LANE_TILE * CMP_HIDDEN), F32)
    for r in range(CMP_BLOCK):
        rows = block_rows(r) + pe_ref[r:r + 1, :]
        acc = acc + jnp.dot(rows.astype(BF16), w1_ref[r], preferred_element_type=F32)
    return jnp.dot(jax.nn.gelu(acc).astype(BF16), w2_ref[...], preferred_element_type=F32)


def _compress_weights(pe, w1, w2):
    n_h = HEADS_PER_LANE_TILE
    eye = jnp.eye(n_h, dtype=w1.dtype)
    w1r = w1.reshape(2, CMP_BLOCK, HEAD_DIM, CMP_HIDDEN)
    w1_bd = jnp.einsum('ab,crdh->cradbh', eye, w1r).reshape(2, CMP_BLOCK, n_h * HEAD_DIM, n_h * CMP_HIDDEN)
    w2_bd = jnp.einsum('ab,chd->cahbd', eye, w2).reshape(2, n_h * CMP_HIDDEN, n_h * HEAD_DIM)
    return jnp.tile(pe, (1, 1, n_h)), w1_bd.astype(BF16), w2_bd.astype(BF16)


def _compress_prompt_body(src_ref, pe_ref, w1_ref, w2_ref, o_ref, *, nc):
    half = nc // 2

    def block_rows(r):
        return jnp.concatenate([src_ref[pl.ds(r, half, stride=2 * CMP_BLOCK), :],
                                src_ref[pl.ds(CMP_BLOCK + r, half, stride=2 * CMP_BLOCK), :]], axis=0)

    o_ref[...] = _compress_rows(block_rows, pe_ref, w1_ref, w2_ref, nc)


def _compress_prompt(proj3, col0, pe, w1_bf, w2_bf):
    b, t, _ = proj3.shape
    gd = NSA_KV_HEADS * HEAD_DIM
    nc = t // CMP_BLOCK
    tiles = gd // LANE
    return pl.pallas_call(
        functools.partial(_compress_prompt_body, nc=nc),
        grid=(2, b, tiles),
        in_specs=[pl.BlockSpec((None, t, LANE), lambda c, i, h: (i, 0, col0 // LANE + c * tiles + h)),
                  pl.BlockSpec((None,) + pe.shape[1:], lambda c, i, h: (c, 0, 0)),
                  pl.BlockSpec((None,) + w1_bf.shape[1:], lambda c, i, h: (c, 0, 0, 0)),
                  pl.BlockSpec((None,) + w2_bf.shape[1:], lambda c, i, h: (c, 0, 0))],
        out_specs=pl.BlockSpec((None, None, nc, LANE), lambda c, i, h: (c, i, 0, h)),
        out_shape=jax.ShapeDtypeStruct((2, b, nc, gd), F32),
        compiler_params=_params("parallel", "parallel", "parallel"),
        name="nsa_compress_prompt",
    )(proj3, pe, w1_bf, w2_bf)


def _gather_pages(pt_ref, b, cache_hbm, layer, row0, nrows, buf, sem, n_pages):
    def copy(p):
        return pltpu.make_async_copy(cache_hbm.at[layer, pt_ref[b, p], pl.ds(row0, nrows), :], buf.at[p], sem)

    def start():
        for p in range(n_pages):
            copy(p).start()

    def wait():
        for p in range(n_pages):
            copy(p).wait()

    return start, wait


ROW_GROUP = 8
PAGES_PER_GROUP = ROW_GROUP * CMP_BLOCK // PAGE_SIZE


def _block_row_permutation():
    n = PAGES_PER_GROUP * PAGE_SIZE
    out_row = np.arange(n)
    src = (out_row % ROW_GROUP) * CMP_BLOCK + out_row // ROW_GROUP
    return (src[:, None] == np.arange(n)[None, :]).astype(np.float32)


def _compress_sample_body(pt_ref, cache_hbm, perm_ref, pe_ref, w1_ref, w2_ref, o_ref, buf, rowbuf, sem,
                          *, layer, n_pages, nc):
    gd = NSA_KV_HEADS * HEAD_DIM
    tiles = gd // LANE
    b = pl.program_id(0)
    gather = lambda seq: _gather_pages(pt_ref, seq, cache_hbm, layer, 0, 2 * gd, buf, sem, n_pages)

    @pl.when(b == 0)
    def _():
        gather(b)[0]()

    gather(b)[1]()

    def to_rows(pg, carry):
        for c in range(2 * tiles):
            x_t = jnp.concatenate([buf[pg * PAGES_PER_GROUP + u, c * LANE:(c + 1) * LANE, :]
                                   for u in range(PAGES_PER_GROUP)], axis=1).astype(BF16)
            rows = lax.dot_general(perm_ref[...], x_t, NT_DIMS, preferred_element_type=F32)
            rowbuf[c, pg] = rows.reshape(CMP_BLOCK, ROW_GROUP, LANE)
        return carry

    lax.fori_loop(0, n_pages // PAGES_PER_GROUP, to_rows, 0)

    @pl.when(b + 1 < pl.num_programs(0))
    def _():
        gather(b + 1)[0]()

    for c in range(2):
        for h in range(tiles):
            src = rowbuf.at[c * tiles + h]
            o_ref[c, :, h * LANE:(h + 1) * LANE] = _compress_rows(
                lambda r: src[:, r].reshape(nc, LANE), pe_ref.at[c], w1_ref.at[c], w2_ref.at[c], nc)


def _compress_sample(cache_t, layer, page_table, pe, w1_bf, w2_bf):
    bs, n_pages = page_table.shape
    gd = NSA_KV_HEADS * HEAD_DIM
    nc = n_pages * PAGE_SIZE // CMP_BLOCK
    assert n_pages % PAGES_PER_GROUP == 0
    perm = jnp.asarray(_block_row_permutation(), BF16)
    tokens = pl.pallas_call(
        functools.partial(_compress_sample_body, layer=layer, n_pages=n_pages, nc=nc),
        grid_spec=pltpu.PrefetchScalarGridSpec(
            num_scalar_prefetch=1,
            grid=(bs,),
            in_specs=[pl.BlockSpec(memory_space=pl.ANY),
                      pl.BlockSpec(perm.shape, lambda i, pt: (0, 0)),
                      pl.BlockSpec(pe.shape, lambda i, pt: (0, 0, 0)),
                      pl.BlockSpec(w1_bf.shape, lambda i, pt: (0, 0, 0, 0)),
                      pl.BlockSpec(w2_bf.shape, lambda i, pt: (0, 0, 0))],
            out_specs=pl.BlockSpec((None, 2, nc, gd), lambda i, pt: (i, 0, 0, 0)),
            scratch_shapes=[pltpu.VMEM((n_pages, 2 * gd, PAGE_SIZE), F32),
                            pltpu.VMEM((2 * gd // LANE, n_pages // PAGES_PER_GROUP, CMP_BLOCK, ROW_GROUP, LANE),
                                       F32),
                            pltpu.SemaphoreType.DMA(())]),
        out_shape=jax.ShapeDtypeStruct((bs, 2, nc, gd), F32),
        compiler_params=_params("arbitrary"),
        name="nsa_compress_sample",
    )(page_table, cache_t, perm, pe, w1_bf, w2_bf)
    return tokens.reshape(bs, 2, nc // 2, 2, gd).transpose(0, 1, 3, 2, 4).reshape(bs, 2, nc, gd)


def _nsa_prompt_body(slope_ref, alibi_ref, q_ref, gate_ref, kc_ref, vc_ref, ks_ref, vs_ref, kw_ref, vw_ref,
                     o_ref, ns_sc, own_sc, tail_sc, m_sc, l_sc, acc_sc, o_sc, *, tq, tk, wk):
    i = pl.program_id(1)
    g_n, r_n, hd = NSA_KV_HEADS, NSA_GROUP, HEAD_DIM
    n_h = g_n * r_n
    rows = r_n * tq
    nc = kc_ref.shape[0]
    ns = nc // 2
    t_q = i * tq + lax.broadcasted_iota(jnp.int32, (1, tq), 1)
    t_row = _tile_lanes(t_q, r_n)
    cur = t_q // SEL_BLOCK
    blk = lax.broadcasted_iota(jnp.int32, (ns, 1), 0)
    q_t = q_ref[...].T * (QK_SCALE * LOG2E)
    gate_t = jax.nn.sigmoid(gate_ref[...]).T
    vc_t = vc_ref[...].T
    c_end, c_mid = _cmp_positions(nc, 0)
    key_i = lax.broadcasted_iota(jnp.int32, (tq, 1), 0)
    causal = _tile_lanes(jnp.where(key_i <= lax.broadcasted_iota(jnp.int32, (1, tq), 1), 0.0, NEG_INF), r_n)
    key_d = pl.multiple_of(i * tq, tq)
    blk_d = pl.multiple_of((i * tq // SEL_BLOCK) // 8 * 8, 8)
    n_past = (i * tq + tk - 1) // tk
    w0 = pl.multiple_of(jnp.maximum((i + 1) * tq - wk, 0), tq)
    w_dist = t_q - (w0 + lax.broadcasted_iota(jnp.int32, (wk, 1), 0))
    in_win = _tile_lanes(jnp.where((w_dist >= 0) & (w_dist < WINDOW), 0.0, NEG_INF), r_n)
    no_sel = jnp.zeros((AUG_OH, rows), F32)

    for g in range(g_n):
        gsl = slice(g * hd, (g + 1) * hd)
        asl = slice(g * AUG, (g + 1) * AUG)
        qg = _group_q_t(q_t, g, r_n)
        slope = slope_ref[g]

        s = jnp.dot(kc_ref[:, gsl], qg, preferred_element_type=F32, precision=HIGHEST)
        s = s - slope * jnp.abs(t_row.astype(F32) - c_mid)
        valid = c_end <= t_row
        s = jnp.where(valid, s, NEG_INF)
        p = jnp.where(valid, jnp.exp2(s - jnp.max(s, axis=0, keepdims=True)), 0.0)
        l = jnp.sum(p, axis=0, keepdims=True)
        p = p / jnp.where(l > 0.0, l, 1.0)
        o_cmp = jnp.dot(vc_t[gsl].astype(BF16), p.astype(BF16), preferred_element_type=F32)
        imp = p[:, 0:tq]
        for r in range(1, r_n):
            imp = imp + p[:, r * tq:(r + 1) * tq]
        imp = imp[:ns] + imp[ns:]
        sel = _topk_mask(imp, blk < cur, min(N_SEL - 1, ns), 0)
        ns_sc[g] = _tile_lanes(1.0 - jnp.where(blk < i * tq // SEL_BLOCK, sel, 0.0), r_n)
        own_sc[g] = _tile_lanes(1.0 - jnp.where(blk == cur, 1.0, sel), r_n)

        tail = _aug_tail(qg, alibi_ref[g])
        tail_sc[g] = tail

        s = jnp.dot(ks_ref[pl.ds(key_d, tq), asl], _aug_rhs(own_sc[g, pl.ds(blk_d, 8), :], tail),
                    preferred_element_type=F32) + causal
        m_sc[g], l_sc[g], acc_sc[g] = _soft_first(s, vs_ref[gsl, pl.ds(key_d, tq)])

        s = jnp.dot(kw_ref[pl.ds(w0, wk), asl], _aug_rhs(no_sel, tail), preferred_element_type=F32) + in_win
        o_win = _soft_done(_soft_first(s, vw_ref[gsl, pl.ds(w0, wk)]))

        for r in range(r_n):
            h = g * r_n + r
            ls = slice(r * tq, (r + 1) * tq)
            o_sc[h * hd:(h + 1) * hd, :] = (gate_t[h:h + 1] * o_cmp[:, ls]
                                            + gate_t[2 * n_h + h:2 * n_h + h + 1] * o_win[:, ls])

    def past(j, carry):
        key0 = pl.multiple_of(j * tk, tk)
        blk0 = pl.multiple_of(j * 8, 8)
        _soft_more_groups(m_sc, l_sc, acc_sc, ks_ref, vs_ref, key0, tk,
                          lambda g: _aug_rhs(ns_sc[g, pl.ds(blk0, 8), :], tail_sc[g]))
        return carry

    lax.fori_loop(0, n_past, past, 0)

    for g in range(g_n):
        o_slc = acc_sc[g] / l_sc[g]
        for r in range(r_n):
            h = g * r_n + r
            o_sc[h * hd:(h + 1) * hd, :] += gate_t[n_h + h:n_h + h + 1] * o_slc[:, r * tq:(r + 1) * tq]
    o_ref[...] = o_sc[...].T


def _nsa_prompt_attn(proj3, k_slc, v_slc_t, k_win, v_win_t, kcvc, slope_rows, alibi_rows, tq):
    b, t, _ = proj3.shape
    g_n = NSA_KV_HEADS
    gd = g_n * HEAD_DIM
    qd = g_n * NSA_GROUP * HEAD_DIM
    nc = kcvc.shape[2]
    rows = NSA_GROUP * tq
    tk = NSA_PAST_TILE
    wk = WINDOW + tq
    assert t % tk == 0 and t >= wk and (nc // 2) % 8 == 0
    k_spec = pl.BlockSpec((None, t, g_n * AUG), lambda bi, i: (bi, 0, 0))
    v_spec = pl.BlockSpec((None, gd, t), lambda bi, i: (bi, 0, 0))
    return pl.pallas_call(
        functools.partial(_nsa_prompt_body, tq=tq, tk=tk, wk=wk),
        grid=(b, t // tq),
        in_specs=[pl.BlockSpec((g_n, 1, rows), lambda bi, i: (0, 0, 0)),
                  pl.BlockSpec((g_n, AUG_AL, rows), lambda bi, i: (0, 0, 0)),
                  pl.BlockSpec((None, tq, qd), lambda bi, i: (bi, i, 0)),
                  pl.BlockSpec((None, tq, LANE), lambda bi, i: (bi, i, (qd + 6 * gd) // LANE)),
                  pl.BlockSpec((None, None, nc, gd), lambda bi, i: (0, bi, 0, 0)),
                  pl.BlockSpec((None, None, nc, gd), lambda bi, i: (1, bi, 0, 0)),
                  k_spec, v_spec, k_spec, v_spec],
        out_specs=pl.BlockSpec((None, tq, qd), lambda bi, i: (bi, i, 0)),
        out_shape=jax.ShapeDtypeStruct((b, t, qd), F32),
        scratch_shapes=[pltpu.VMEM((g_n, nc // 2, rows), F32), pltpu.VMEM((g_n, nc // 2, rows), F32),
                        pltpu.VMEM((g_n, AUG - AUG_OH, rows), BF16),
                        pltpu.VMEM((g_n, 1, rows), F32), pltpu.VMEM((g_n, 1, rows), F32),
                        pltpu.VMEM((g_n, HEAD_DIM, rows), F32), pltpu.VMEM((qd, tq), F32)],
        compiler_params=_params("parallel", "arbitrary"),
        name="nsa_prompt_attn",
    )(slope_rows, alibi_rows, proj3, proj3, kcvc, kcvc, k_slc, v_slc_t, k_win, v_win_t)


def _page_tile(buf, slot, j, tk, lo, hi):
    pages = tk // PAGE_SIZE
    return jnp.concatenate([buf[slot, j * pages + u, lo:hi, :] for u in range(pages)], axis=1).astype(BF16)


def _block_diag_q(q_ref, g_n, r_n):
    hd = HEAD_DIM
    ts = q_ref.shape[0]
    out = []
    for g in range(g_n):
        qg = jnp.concatenate([q_ref[:, (g * r_n + r) * hd:(g * r_n + r + 1) * hd] for r in range(r_n)], axis=0)
        parts = [jnp.zeros((r_n * ts, hd), F32)] * g_n
        parts[g] = qg
        out.append(jnp.concatenate(parts, axis=-1))
    return jnp.concatenate(out, axis=0)


def _nsa_sample_body(pt_ref, slope_ref, q_ref, gate_ref, kcvc_ref, snew_ref, wnew_ref, win_ref, cache_hbm,
                     o_ref, buf, sem, *, layer, n_pages, tk):
    g_n, r_n, hd = NSA_KV_HEADS, NSA_GROUP, HEAD_DIM
    gd = g_n * hd
    ts = q_ref.shape[0]
    rows = g_n * r_n * ts
    past = n_pages * PAGE_SIZE
    b = pl.program_id(0)
    slot = b % 2
    gather = lambda seq, sl: _gather_pages(pt_ref, seq, cache_hbm, layer, 2 * gd, 2 * gd, buf.at[sl],
                                           sem.at[sl], n_pages)

    @pl.when(b == 0)
    def _():
        gather(b, slot)[0]()

    @pl.when(b + 1 < pl.num_programs(0))
    def _():
        gather(b + 1, 1 - slot)[0]()

    qf = _block_diag_q(q_ref, g_n, r_n) * QK_SCALE
    qb = qf.astype(BF16)
    slope = slope_ref[...]
    t = past + lax.broadcasted_iota(jnp.int32, (rows, 1), 0) % ts
    new_pos = past + lax.broadcasted_iota(jnp.int32, (1, ts), 1)
    causal = lambda dist: dist >= 0

    nc = kcvc_ref.shape[1]
    ns = nc // 2
    p_cmp, o_cmp = _cmp_attention(qf, kcvc_ref[0], kcvc_ref[1], t, slope)
    imp = []
    for g in range(g_n):
        base = g * r_n * ts
        acc = p_cmp[base:base + ts]
        for r in range(1, r_n):
            acc = acc + p_cmp[base + r * ts:base + (r + 1) * ts]
        imp.append(acc)
    imp = jnp.concatenate(imp, axis=0)
    imp = imp[:, :ns] + imp[:, ns:]
    sel = _topk_mask(imp, jnp.full(imp.shape, True), min(N_SEL - 1, ns), 1).astype(BF16)
    sel = jnp.concatenate([sel[g * ts:(g + 1) * ts] for g in range(g_n) for _ in range(r_n)], axis=0)

    wp = win_ref.shape[1]
    win_pos = (past - wp) + lax.broadcasted_iota(jnp.int32, (1, wp), 1)
    in_win = lambda dist: (dist >= 0) & (dist < WINDOW)
    carry = _softmax_step(_softmax_init(rows, gd), qb, win_ref[:gd, :].astype(BF16),
                          win_ref[gd:, :].astype(BF16), t, slope, win_pos, in_win, True)
    carry = _softmax_step(carry, qb, wnew_ref[:, :gd].astype(BF16), wnew_ref[:, gd:].astype(BF16),
                          t, slope, new_pos, in_win, False)
    o_win = _softmax_done(carry)

    gather(b, slot)[1]()

    def slc_step(j, carry):
        key0 = pl.multiple_of(j * tk, tk)
        kpos = key0 + lax.broadcasted_iota(jnp.int32, (1, tk), 1)
        in_sel = _block_mask(sel, key0, tk, SEL_BLOCK.bit_length() - 1)
        return _softmax_step(carry, qb, _page_tile(buf, slot, j, tk, 0, gd), _page_tile(buf, slot, j, tk, gd, 2 * gd),
                             t, slope, kpos, lambda dist: in_sel, True)

    carry = lax.fori_loop(0, past // tk, slc_step, _softmax_init(rows, gd))
    carry = _softmax_step(carry, qb, snew_ref[:, :gd].astype(BF16), snew_ref[:, gd:].astype(BF16),
                          t, slope, new_pos, causal, False)
    o_slc = _softmax_done(carry)

    gate = jax.nn.sigmoid(gate_ref[...])
    n_h = g_n * r_n
    for h in range(n_h):
        g = h // r_n
        rs = slice(h * ts, (h + 1) * ts)
        cs = slice(g * hd, (g + 1) * hd)
        o_ref[:, h * hd:(h + 1) * hd] = (gate[:, h:h + 1] * o_cmp[rs, cs]
                                         + gate[:, n_h + h:n_h + h + 1] * o_slc[rs, cs]
                                         + gate[:, 2 * n_h + h:2 * n_h + h + 1] * o_win[rs, cs])


def _nsa_sample_attn(proj3, kcvc, win_t, cache_t, layer, page_table, slopes, tk=512):
    bs, ts, _ = proj3.shape
    n_pages = page_table.shape[1]
    gd = NSA_KV_HEADS * HEAD_DIM
    qd = NSA_KV_HEADS * NSA_GROUP * HEAD_DIM
    nc = kcvc.shape[2]
    wp = win_t.shape[3]
    slope_rows = jnp.repeat(slopes, ts).reshape(-1, 1)
    return pl.pallas_call(
        functools.partial(_nsa_sample_body, layer=layer, n_pages=n_pages, tk=tk),
        grid_spec=pltpu.PrefetchScalarGridSpec(
            num_scalar_prefetch=1,
            grid=(bs,),
            in_specs=[pl.BlockSpec((slope_rows.shape[0], 1), lambda i, pt: (0, 0)),
                      pl.BlockSpec((None, ts, qd), lambda i, pt: (i, 0, 0)),
                      pl.BlockSpec((None, ts, LANE), lambda i, pt: (i, 0, (qd + 6 * gd) // LANE)),
                      pl.BlockSpec((None, 2, nc, gd), lambda i, pt: (i, 0, 0, 0)),
                      pl.BlockSpec((None, ts, 2 * gd), lambda i, pt: (i, 0, (qd + 2 * gd) // (2 * gd))),
                      pl.BlockSpec((None, ts, 2 * gd), lambda i, pt: (i, 0, (qd + 4 * gd) // (2 * gd))),
                      pl.BlockSpec((None, None, 2 * gd, wp), lambda i, pt: (layer, i, 0, 0)),
                      pl.BlockSpec(memory_space=pl.ANY)],
            out_specs=pl.BlockSpec((None, ts, qd), lambda i, pt: (i, 0, 0)),
            scratch_shapes=[pltpu.VMEM((2, n_pages, 2 * gd, PAGE_SIZE), F32), pltpu.SemaphoreType.DMA((2,))]),
        out_shape=jax.ShapeDtypeStruct((bs, ts, qd), F32),
        compiler_params=_params("arbitrary"),
        name="nsa_sample_attn",
    )(page_table, slope_rows, proj3, proj3, kcvc, proj3, proj3, win_t, cache_t)


def _moba_prompt_body(alibi_ref, q_ref, kf_ref, k_ref, v_ref, o_ref, kmean_sc, rhs_sc, m_sc, l_sc, acc_sc, o_sc,
                      *, tq, tk):
    i = pl.program_id(1)
    g_n, r_n, hd = MOBA_KV_HEADS, MOBA_GROUP, HEAD_DIM
    rows = r_n * tq
    nb = kmean_sc.shape[0]

    @pl.when(i == 0)
    def _():
        kmean_sc[...] = jnp.concatenate(
            [jnp.mean(kf_ref[n * MOBA_BLOCK:(n + 1) * MOBA_BLOCK, :], axis=0, keepdims=True) for n in range(nb)],
            axis=0)

    t_q = i * tq + lax.broadcasted_iota(jnp.int32, (1, tq), 1)
    cur = i * tq // MOBA_BLOCK
    blk = lax.broadcasted_iota(jnp.int32, (nb, 1), 0)
    q_t = q_ref[...].T
    own0 = pl.multiple_of(cur * MOBA_BLOCK, MOBA_BLOCK)
    own_pos = own0 + lax.broadcasted_iota(jnp.int32, (MOBA_BLOCK, 1), 0)
    causal = _tile_lanes(jnp.where(own_pos <= t_q, 0.0, NEG_INF), r_n)
    n_past = (cur * MOBA_BLOCK + tk - 1) // tk
    no_sel = jnp.zeros((AUG_OH, rows), F32)

    for g in range(g_n):
        gsl = slice(g * hd, (g + 1) * hd)
        asl = slice(g * AUG, (g + 1) * AUG)
        qg = _group_q_t(q_t, g, r_n)
        gs = jnp.dot(kmean_sc[:, gsl], qg, preferred_element_type=F32, precision=HIGHEST)
        sel = _topk_mask(gs, blk < cur, min(MOBA_TOPK, nb), 0)
        tail = _aug_tail(qg * (QK_SCALE * LOG2E), alibi_ref[g])
        rhs_sc[g] = _aug_rhs(1.0 - sel, tail)

        s = jnp.dot(k_ref[pl.ds(own0, MOBA_BLOCK), asl], _aug_rhs(no_sel, tail),
                    preferred_element_type=F32) + causal
        m_sc[g], l_sc[g], acc_sc[g] = _soft_first(s, v_ref[gsl, pl.ds(own0, MOBA_BLOCK)])

    def past(j, carry):
        key0 = pl.multiple_of(j * tk, tk)
        _soft_more_groups(m_sc, l_sc, acc_sc, k_ref, v_ref, key0, tk, lambda g: rhs_sc[g])
        return carry

    lax.fori_loop(0, n_past, past, 0)

    for g in range(g_n):
        o = acc_sc[g] / l_sc[g]
        for r in range(r_n):
            h = g * r_n + r
            o_sc[h * hd:(h + 1) * hd, :] = o[:, r * tq:(r + 1) * tq]
    o_ref[...] = o_sc[...].T


def _moba_prompt_attn(proj3, k_aug, v_t, alibi_rows, tq, tk=512):
    b, t, _ = proj3.shape
    g_n = MOBA_KV_HEADS
    gd = g_n * HEAD_DIM
    qd = g_n * MOBA_GROUP * HEAD_DIM
    rows = MOBA_GROUP * tq
    assert t % tk == 0 and tk % MOBA_BLOCK == 0 and MOBA_BLOCK % tq == 0 and t // MOBA_BLOCK <= AUG_OH
    return pl.pallas_call(
        functools.partial(_moba_prompt_body, tq=tq, tk=tk),
        grid=(b, t // tq),
        in_specs=[pl.BlockSpec((g_n, AUG_AL, rows), lambda bi, i: (0, 0, 0)),
                  pl.BlockSpec((None, tq, qd), lambda bi, i: (bi, i, 0)),
                  pl.BlockSpec((None, t, gd), lambda bi, i: (bi, 0, qd // gd)),
                  pl.BlockSpec((None, t, g_n * AUG), lambda bi, i: (bi, 0, 0)),
                  pl.BlockSpec((None, gd, t), lambda bi, i: (bi, 0, 0))],
        out_specs=pl.BlockSpec((None, tq, qd), lambda bi, i: (bi, i, 0)),
        out_shape=jax.ShapeDtypeStruct((b, t, qd), F32),
        scratch_shapes=[pltpu.VMEM((t // MOBA_BLOCK, gd), F32), pltpu.VMEM((g_n, AUG, rows), BF16),
                        pltpu.VMEM((g_n, 1, rows), F32), pltpu.VMEM((g_n, 1, rows), F32),
                        pltpu.VMEM((g_n, HEAD_DIM, rows), F32), pltpu.VMEM((qd, tq), F32)],
        compiler_params=_params("parallel", "arbitrary"),
        name="moba_prompt_attn",
    )(alibi_rows, proj3, proj3, k_aug, v_t)


def _moba_sample_body(pt_ref, slope_ref, q_ref, new_ref, cache_hbm, o_ref, buf, sem, *, layer, n_pages, tk):
    g_n, r_n, hd = MOBA_KV_HEADS, MOBA_GROUP, HEAD_DIM
    gd = g_n * hd
    ts = q_ref.shape[0]
    rows = g_n * r_n * ts
    past = n_pages * PAGE_SIZE
    nb = past // MOBA_BLOCK
    b = pl.program_id(0)
    slot = b % 2
    gather = lambda seq, sl: _gather_pages(pt_ref, seq, cache_hbm, layer, 0, 2 * gd, buf.at[sl], sem.at[sl],
                                           n_pages)

    @pl.when(b == 0)
    def _():
        gather(b, slot)[0]()

    @pl.when(b + 1 < pl.num_programs(0))
    def _():
        gather(b + 1, 1 - slot)[0]()

    qf = _block_diag_q(q_ref, g_n, r_n)
    qb = (qf * QK_SCALE).astype(BF16)
    slope = slope_ref[...]
    t = past + lax.broadcasted_iota(jnp.int32, (rows, 1), 0) % ts
    new_pos = past + lax.broadcasted_iota(jnp.int32, (1, ts), 1)
    gather(b, slot)[1]()

    blk = lax.broadcasted_iota(jnp.int32, (gd, nb), 1)
    kmean_t = jnp.zeros((gd, nb), F32)
    pages = MOBA_BLOCK // PAGE_SIZE
    for n in range(nb):
        col = sum(jnp.sum(buf[slot, n * pages + u, :gd, :], axis=1, keepdims=True) for u in range(pages))
        kmean_t = jnp.where(blk == n, col / MOBA_BLOCK, kmean_t)
    gs = jnp.dot(qf, kmean_t, preferred_element_type=F32, precision=HIGHEST)
    sel = _topk_mask(gs, jnp.full(gs.shape, True), min(MOBA_TOPK, nb), 1).astype(BF16)

    def step(j, carry):
        key0 = pl.multiple_of(j * tk, tk)
        kpos = key0 + lax.broadcasted_iota(jnp.int32, (1, tk), 1)
        in_sel = _block_mask(sel, key0, tk, MOBA_BLOCK.bit_length() - 1)
        return _softmax_step(carry, qb, _page_tile(buf, slot, j, tk, 0, gd), _page_tile(buf, slot, j, tk, gd, 2 * gd),
                             t, slope, kpos, lambda dist: in_sel, True)

    carry = lax.fori_loop(0, past // tk, step, _softmax_init(rows, gd))
    carry = _softmax_step(carry, qb, new_ref[:, :gd].astype(BF16), new_ref[:, gd:].astype(BF16),
                          t, slope, new_pos, lambda dist: dist >= 0, False)
    o = _softmax_done(carry)
    for h in range(g_n * r_n):
        g = h // r_n
        o_ref[:, h * hd:(h + 1) * hd] = o[h * ts:(h + 1) * ts, g * hd:(g + 1) * hd]


def _moba_sample_attn(proj3, cache_t, layer, page_table, slopes, tk=512):
    bs, ts, _ = proj3.shape
    n_pages = page_table.shape[1]
    gd = MOBA_KV_HEADS * HEAD_DIM
    qd = MOBA_KV_HEADS * MOBA_GROUP * HEAD_DIM
    slope_rows = jnp.repeat(slopes, ts).reshape(-1, 1)
    return pl.pallas_call(
        functools.partial(_moba_sample_body, layer=layer, n_pages=n_pages, tk=tk),
        grid_spec=pltpu.PrefetchScalarGridSpec(
            num_scalar_prefetch=1,
            grid=(bs,),
            in_specs=[pl.BlockSpec((slope_rows.shape[0], 1), lambda i, pt: (0, 0)),
                      pl.BlockSpec((None, ts, qd), lambda i, pt: (i, 0, 0)),
                      pl.BlockSpec((None, ts, 2 * gd), lambda i, pt: (i, 0, qd // (2 * gd))),
                      pl.BlockSpec(memory_space=pl.ANY)],
            out_specs=pl.BlockSpec((None, ts, qd), lambda i, pt: (i, 0, 0)),
            scratch_shapes=[pltpu.VMEM((2, n_pages, 2 * gd, PAGE_SIZE), F32), pltpu.SemaphoreType.DMA((2,))]),
        out_shape=jax.ShapeDtypeStruct((bs, ts, qd), F32),
        compiler_params=_params("arbitrary"),
        name="moba_sample_attn",
    )(page_table, slope_rows, proj3, proj3, cache_t)


PROMPT_Q_TILE = 128


def _alibi_slopes(n_heads):
    return jnp.exp2(-8.0 * jnp.arange(1, n_heads + 1, dtype=F32) / n_heads)


def _alibi_query_rows(slopes, g_n, r_n, tq):
    s2 = slopes * LOG2E
    d0 = s2.astype(BF16).astype(F32)
    d1 = (s2 - d0).astype(BF16).astype(F32)
    d2 = (s2 - d0 - d1).astype(BF16).astype(F32)
    zero = jnp.zeros_like(s2)
    digits = jnp.stack([d0, d1, d2, POS_DIGIT * d0, POS_DIGIT * d1, POS_DIGIT * d2, zero, zero], axis=0)
    lanes = lambda a: jnp.repeat(a.reshape(-1, g_n, r_n), tq, axis=2).transpose(1, 0, 2)
    return lanes(s2[None]), lanes(digits)


def _positions_minor(a, n_lead):
    nd = a.ndim
    a = jnp.transpose(a, tuple(range(n_lead)) + tuple(range(n_lead + 1, nd)) + (n_lead,))
    return a.reshape(a.shape[:n_lead] + (-1, a.shape[-1]))


def _nsa_layer(xp, xs, bp, bs, cache_t, win_t, layer, win_state, page_table, w_in, w_out, pe, w1, w2, g, b,
               alpha):
    tp, ts = xp.shape[0] // bp, xs.shape[0] // bs
    g_n, hd = NSA_KV_HEADS, HEAD_DIM
    gd = g_n * hd
    qd = g_n * NSA_GROUP * hd
    n_in = w_in.shape[1]
    n_pad = -(-n_in // LANE) * LANE
    w_in_bf = jnp.pad(w_in, ((0, 0), (0, n_pad - n_in))).astype(BF16)
    pe, w1_bf, w2_bf = _compress_weights(pe, w1, w2)
    w_out_bf = w_out.astype(BF16)
    slopes = _alibi_slopes(g_n * NSA_GROUP)
    past = page_table.shape[1] * PAGE_SIZE
    assert PAGE_SIZE == LANE and past % (2 * CMP_BLOCK) == 0 and ts <= CMP_BLOCK

    assert tp <= POS_DIGIT * 256
    proj_p, k_slc, k_win, v_slc_t, v_win_t = _proj(
        xp, w_in_bf, tp, keys=((qd + 2 * gd, SEL_BLOCK, NSA_PAST_TILE // SEL_BLOCK), (qd + 4 * gd, SEL_BLOCK, 0)),
        values=(qd + 3 * gd, qd + 5 * gd), g_n=g_n)
    (proj_s,) = _proj(xs, w_in_bf)
    proj_p3 = proj_p.reshape(bp, tp, n_pad)
    proj_s3 = proj_s.reshape(bs, ts, n_pad)
    kcvc_p = _compress_prompt(proj_p3, qd, pe, w1_bf, w2_bf)
    slope_rows, alibi_rows = _alibi_query_rows(slopes, g_n, NSA_GROUP, PROMPT_Q_TILE)
    o_p = _nsa_prompt_attn(proj_p3, k_slc.reshape(bp, tp, g_n * AUG), v_slc_t, k_win.reshape(bp, tp, g_n * AUG),
                           v_win_t, kcvc_p, slope_rows, alibi_rows, PROMPT_Q_TILE)
    kcvc_s = _compress_sample(cache_t, layer, page_table, pe, w1_bf, w2_bf)
    o_s = _nsa_sample_attn(proj_s3, kcvc_s, win_t, cache_t, layer, page_table, slopes)
    xp = _out_ln(o_p.reshape(bp * tp, qd), xp, w_out_bf, g, b, alpha)
    xs = _out_ln(o_s.reshape(bs * ts, qd), xs, w_out_bf, g, b, alpha)

    kv_shape = (4, g_n, hd)
    win_shape = (2, g_n, hd)
    kv_p = proj_p3[:, :, qd:qd + 4 * gd].reshape((bp, tp) + kv_shape)
    kv_s = proj_s3[:, :, qd:qd + 4 * gd].reshape((bs, ts) + kv_shape)
    win_p = proj_p3[:, tp - min(WINDOW, tp):, qd + 4 * gd:qd + 6 * gd].reshape((bp, min(WINDOW, tp)) + win_shape)
    win_s = jnp.concatenate([win_state, proj_s3[:, :, qd + 4 * gd:qd + 6 * gd].reshape((bs, ts) + win_shape)], axis=1)
    win_s = win_s[:, win_s.shape[1] - min(WINDOW, win_s.shape[1]):]
    return xp, xs, kv_p, kv_s, win_p, win_s


def _pool_layer(xp, xs, bp, bs, state, past, w, scale, g, b, alpha):
    d = xp.shape[1]
    tp, ts = xp.shape[0] // bp, xs.shape[0] // bs
    assert POOL_HALO % ts == 0 and tp % POOL_HALO == 0
    w_bf = w.astype(BF16)
    scale = scale.reshape(1, d)
    xs3 = xs.reshape(bs, ts, d)
    xe_s = jnp.concatenate([jnp.zeros((bs, POOL_HALO - POOL_PAST, d), F32), state, xs3], axis=1)
    pool_p = xp.reshape(bp, tp, d)[:, tp - POOL_PAST:]
    pool_s = xe_s[:, xe_s.shape[1] - POOL_PAST:]
    xp = _pool_ln_prompt(xp, tp, w_bf, scale, g, b, alpha)
    xs = _pool_ln_sample(xe_s, ts, past, w_bf, scale, g, b, alpha)
    return xp, xs, pool_p, pool_s


def _moba_layer(xp, xs, bp, bs, cache_t, layer, page_table, w_in, w_out, g, b, alpha):
    tp, ts = xp.shape[0] // bp, xs.shape[0] // bs
    g_n, hd = MOBA_KV_HEADS, HEAD_DIM
    gd = g_n * hd
    qd = g_n * MOBA_GROUP * hd
    n_in = w_in.shape[1]
    w_in_bf, w_out_bf = w_in.astype(BF16), w_out.astype(BF16)
    slopes = _alibi_slopes(g_n * MOBA_GROUP)
    past = page_table.shape[1] * PAGE_SIZE
    assert PAGE_SIZE == LANE and past % MOBA_BLOCK == 0 and ts <= MOBA_BLOCK

    assert tp <= POS_DIGIT * 256
    proj_p, k_aug, v_t = _proj(xp, w_in_bf, tp, keys=((qd, MOBA_BLOCK, AUG_OH),), values=(qd + gd,), g_n=g_n)
    (proj_s,) = _proj(xs, w_in_bf)
    proj_p3 = proj_p.reshape(bp, tp, n_in)
    proj_s3 = proj_s.reshape(bs, ts, n_in)
    _, alibi_rows = _alibi_query_rows(slopes, g_n, MOBA_GROUP, PROMPT_Q_TILE)
    o_p = _moba_prompt_attn(proj_p3, k_aug.reshape(bp, tp, g_n * AUG), v_t, alibi_rows, PROMPT_Q_TILE)
    o_s = _moba_sample_attn(proj_s3, cache_t, layer, page_table, slopes)
    xp = _out_ln(o_p.reshape(bp * tp, qd), xp, w_out_bf, g, b, alpha)
    xs = _out_ln(o_s.reshape(bs * ts, qd), xs, w_out_bf, g, b, alpha)
    kv_shape = (2, g_n, hd)
    kv_p = proj_p3[:, :, qd:].reshape((bp, tp) + kv_shape)
    kv_s = proj_s3[:, :, qd:].reshape((bs, ts) + kv_shape)
    return xp, xs, kv_p, kv_s


def kernel(x_prompt, x_sample, cache_nsa_kv, state_nsa_win, state_pool, cache_moba, page_table, ln_g, ln_b, mlp_w1, mlp_w2, nsa_w_in, nsa_w_out, nsa_cmp_pe, nsa_cmp_w1, nsa_cmp_w2, pool_w, pool_scale, moba_w_in, moba_w_out):
    bp, tp, d = x_prompt.shape
    bs, ts, _ = x_sample.shape
    depth = ln_g.shape[0]
    alpha = (2 * depth) ** 0.25
    past = page_table.shape[1] * PAGE_SIZE
    xp = x_prompt.reshape(bp * tp, d)
    xs = x_sample.reshape(bs * ts, d)
    nsa_cache_t = _positions_minor(cache_nsa_kv, 2)
    nsa_win_t = _positions_minor(state_nsa_win, 2)
    moba_cache_t = _positions_minor(cache_moba, 2)
    outs = {k: [] for k in ("nsa_kv_p", "nsa_kv_s", "nsa_win_p", "nsa_win_s", "pool_p", "pool_s", "moba_p", "moba_s")}
    for i in range(depth):
        kind, j = i % N_MIXERS, i // N_MIXERS
        g0, b0 = ln_g[i, 0].reshape(1, d), ln_b[i, 0].reshape(1, d)
        g1, b1 = ln_g[i, 1].reshape(1, d), ln_b[i, 1].reshape(1, d)
        if kind == 0:
            xp, xs, kv_p, kv_s, win_p, win_s = _nsa_layer(
                xp, xs, bp, bs, nsa_cache_t, nsa_win_t, j, state_nsa_win[j], page_table, nsa_w_in[j], nsa_w_out[j],
                nsa_cmp_pe[j], nsa_cmp_w1[j], nsa_cmp_w2[j], g0, b0, alpha)
            outs["nsa_kv_p"].append(kv_p); outs["nsa_kv_s"].append(kv_s)
            outs["nsa_win_p"].append(win_p); outs["nsa_win_s"].append(win_s)
        elif kind == 1:
            xp, xs, pool_p, pool_s = _pool_layer(xp, xs, bp, bs, state_pool[j], past, pool_w[j], pool_scale[j],
                                                 g0, b0, alpha)
            outs["pool_p"].append(pool_p); outs["pool_s"].append(pool_s)
        else:
            xp, xs, kv_p, kv_s = _moba_layer(xp, xs, bp, bs, moba_cache_t, j, page_table, moba_w_in[j],
                                             moba_w_out[j], g0, b0, alpha)
            outs["moba_p"].append(kv_p); outs["moba_s"].append(kv_s)
        w1_bf, w2_bf = mlp_w1[i].astype(BF16), mlp_w2[i].astype(BF16)
        xp = _mlp_ln(xp, w1_bf, w2_bf, g1, b1, alpha)
        xs = _mlp_ln(xs, w1_bf, w2_bf, g1, b1, alpha)
    return (xp.reshape(bp, tp, d), xs.reshape(bs, ts, d),
            jnp.stack(outs["nsa_kv_p"]), jnp.stack(outs["nsa_kv_s"]),
            jnp.stack(outs["nsa_win_p"]), jnp.stack(outs["nsa_win_s"]),
            jnp.stack(outs["pool_p"]), jnp.stack(outs["pool_s"]),
            jnp.stack(outs["moba_p"]), jnp.stack(outs["moba_s"]))
```

```python
import functools

import jax
import jax.numpy as jnp
import numpy as np
from jax import lax
from jax.experimental import pallas as pl
from jax.experimental.pallas import tpu as pltpu

F32 = jnp.float32
BF16 = jnp.bfloat16

HEAD_DIM = 64
PAGE_SIZE = 128
N_MIXERS = 3
NSA_KV_HEADS = 4
NSA_GROUP = 4
CMP_BLOCK = 32
SEL_BLOCK = 64
N_SEL = 16
WINDOW = 512
CMP_HIDDEN = 2 * HEAD_DIM
POOL_WINDOWS = (2, 4, 8, 16)
POOL_PAST = max(POOL_WINDOWS) - 1
POOL_HALO = 16
MOBA_KV_HEADS = 4
MOBA_GROUP = 4
MOBA_BLOCK = 256
MOBA_TOPK = 3
LN_EPS = 1e-5
NEG_INF = -1e30
QK_SCALE = HEAD_DIM ** -0.5
LOG2E = 1.4426950408889634

LANE = 128
VMEM_LIMIT = 56 * 1024 * 1024
NT_DIMS = (((1,), (1,)), ((), ()))
HIGHEST = lax.Precision.HIGHEST

AUG = LANE
AUG_OH = 16
AUG_AL = 8
AUG_AL0 = HEAD_DIM + AUG_OH
POS_DIGIT = 16
NSA_PAST_TILE = 8 * SEL_BLOCK


def _params(*sem):
    return pltpu.CompilerParams(dimension_semantics=sem, vmem_limit_bytes=VMEM_LIMIT)


def _layer_norm(y, g, b):
    mu = jnp.mean(y, axis=-1, keepdims=True)
    d = y - mu
    var = jnp.mean(d * d, axis=-1, keepdims=True)
    return d * lax.rsqrt(var + LN_EPS) * g + b


def _row_tile(m, want):
    t = min(m, want)
    while m % t:
        t //= 2
    return t


def _proj_body(x_ref, w_ref, o_ref, *aug_refs, keys, values, tm, seq_len, g_n):
    acc = jnp.dot(x_ref[...].astype(BF16), w_ref[...], preferred_element_type=F32)
    o_ref[...] = acc
    if not aug_refs:
        return
    pos = (pl.program_id(0) * tm) % seq_len + lax.broadcasted_iota(jnp.int32, (tm, AUG), 0)
    lane = lax.broadcasted_iota(jnp.int32, (tm, AUG), 1)
    lo_digit = (lane >= AUG_AL0) & (lane < AUG_AL0 + 3)
    hi_digit = (lane >= AUG_AL0 + 3) & (lane < AUG_AL0 + 6)
    digits = jnp.where(lo_digit, pos % POS_DIGIT, jnp.where(hi_digit, pos // POS_DIGIT, 0)).astype(F32)
    for k_ref, (col0, blk_len, n_onehot) in zip(aug_refs, keys):
        extra = digits
        if n_onehot:
            extra = jnp.where(lane - HEAD_DIM == (pos // blk_len) % n_onehot, NEG_INF, digits)
        for g in range(g_n):
            col = col0 + g * HEAD_DIM
            piece = acc[:, col // LANE * LANE:(col // LANE + 1) * LANE]
            if col % LANE:
                piece = pltpu.roll(piece, LANE - col % LANE, 1)
            k_ref[:, g * AUG:(g + 1) * AUG] = jnp.where(lane < HEAD_DIM, piece, extra).astype(BF16)
    for v_ref, col0 in zip(aug_refs[len(keys):], values):
        v_ref[...] = acc[:, col0:col0 + g_n * HEAD_DIM].T.astype(BF16)


def _proj(x, w_bf, seq_len=None, keys=(), values=(), g_n=0):
    m, d = x.shape
    n = w_bf.shape[1]
    tm = _row_tile(seq_len or m, 512)
    per_seq = (seq_len or m) // tm
    gd = g_n * HEAD_DIM
    out_specs = [pl.BlockSpec((tm, n), lambda i: (i, 0))]
    out_shape = [jax.ShapeDtypeStruct((m, n), F32)]
    out_specs += [pl.BlockSpec((tm, g_n * AUG), lambda i: (i, 0)) for _ in keys]
    out_shape += [jax.ShapeDtypeStruct((m, g_n * AUG), BF16) for _ in keys]
    out_specs += [pl.BlockSpec((None, gd, tm), lambda i: (i // per_seq, 0, i % per_seq)) for _ in values]
    out_shape += [jax.ShapeDtypeStruct((m // (seq_len or m), gd, seq_len or m), BF16) for _ in values]
    return pl.pallas_call(
        functools.partial(_proj_body, keys=keys, values=values, tm=tm, seq_len=seq_len, g_n=g_n),
        grid=(m // tm,),
        in_specs=[pl.BlockSpec((tm, d), lambda i: (i, 0)),
                  pl.BlockSpec((d, n), lambda i: (0, 0))],
        out_specs=out_specs,
        out_shape=out_shape,
        compiler_params=_params("parallel"),
        name="proj",
    )(x, w_bf)


def _out_ln_body(o_ref, x_ref, w_ref, g_ref, b_ref, y_ref, *, alpha):
    h = jnp.dot(o_ref[...].astype(BF16), w_ref[...], preferred_element_type=F32)
    y_ref[...] = _layer_norm(alpha * x_ref[...] + h, g_ref[...], b_ref[...])


def _out_ln(o, x, w_bf, g, b, alpha):
    m, d = x.shape
    k = o.shape[1]
    tm = _row_tile(m, 512)
    return pl.pallas_call(
        functools.partial(_out_ln_body, alpha=alpha),
        grid=(m // tm,),
        in_specs=[pl.BlockSpec((tm, k), lambda i: (i, 0)),
                  pl.BlockSpec((tm, d), lambda i: (i, 0)),
                  pl.BlockSpec((k, d), lambda i: (0, 0)),
                  pl.BlockSpec((1, d), lambda i: (0, 0)),
                  pl.BlockSpec((1, d), lambda i: (0, 0))],
        out_specs=pl.BlockSpec((tm, d), lambda i: (i, 0)),
        out_shape=jax.ShapeDtypeStruct((m, d), F32),
        compiler_params=_params("parallel"),
        name="out_ln",
    )(o, x, w_bf, g, b)


MLP_F_CHUNK = 1024


def _mlp_ln_body(x_ref, w1_ref, w2_ref, g_ref, b_ref, y_ref, *, alpha):
    x = x_ref[...]
    xb = x.astype(BF16)
    acc = None
    for c in range(w1_ref.shape[1] // MLP_F_CHUNK):
        fs = slice(c * MLP_F_CHUNK, (c + 1) * MLP_F_CHUNK)
        h = jnp.maximum(jnp.dot(xb, w1_ref[:, fs], preferred_element_type=F32), 0.0)
        part = jnp.dot((h * h).astype(BF16), w2_ref[fs, :], preferred_element_type=F32)
        acc = part if acc is None else acc + part
    y_ref[...] = _layer_norm(alpha * x + acc, g_ref[...], b_ref[...])


def _mlp_ln(x, w1_bf, w2_bf, g, b, alpha):
    m, d = x.shape
    f = w1_bf.shape[1]
    tm = _row_tile(m, 512)
    once = pl.Buffered(1)
    return pl.pallas_call(
        functools.partial(_mlp_ln_body, alpha=alpha),
        grid=(m // tm,),
        in_specs=[pl.BlockSpec((tm, d), lambda i: (i, 0)),
                  pl.BlockSpec((d, f), lambda i: (0, 0), pipeline_mode=once),
                  pl.BlockSpec((f, d), lambda i: (0, 0), pipeline_mode=once),
                  pl.BlockSpec((1, d), lambda i: (0, 0)),
                  pl.BlockSpec((1, d), lambda i: (0, 0))],
        out_specs=pl.BlockSpec((tm, d), lambda i: (i, 0)),
        out_shape=jax.ShapeDtypeStruct((m, d), F32),
        compiler_params=_params("parallel"),
        name="mlp_ln",
    )(x, w1_bf, w2_bf, g, b)


def _pool_ln_body(halo_ref, x_ref, w_ref, scale_ref, g_ref, b_ref, y_ref, *, alpha, pos0, tiles_per_seq):
    tm, d = x_ref.shape
    grp = d // len(POOL_WINDOWS)
    seq_tile = pl.program_id(0) % tiles_per_seq
    x = x_ref[...]
    halo = halo_ref[...]
    if pos0 == 0:
        halo = jnp.where(seq_tile == 0, 0.0, halo)
    xe = jnp.concatenate([halo, x], axis=0)
    n_avail = pos0 + seq_tile * tm + lax.broadcasted_iota(jnp.int32, (tm, 1), 0) + 1
    mixed = []
    for gi, w in enumerate(POOL_WINDOWS):
        s = xe[:, gi * grp:(gi + 1) * grp]
        step = 1
        while step < w:
            s = s + pltpu.roll(s, step, 0)
            step *= 2
        cnt = jnp.minimum(n_avail, w).astype(F32)
        pooled = s[POOL_HALO:] / cnt - x[:, gi * grp:(gi + 1) * grp]
        mixed.append(jnp.dot(pooled.astype(BF16), w_ref[gi], preferred_element_type=F32))
    h = jnp.concatenate(mixed, axis=-1) * scale_ref[...]
    y_ref[...] = _layer_norm(alpha * x + h, g_ref[...], b_ref[...])


def _pool_ln(rows, halo_spec, x_spec, n_tiles, tm, tiles_per_seq, pos0, w_bf, scale, g, b, alpha):
    d = rows.shape[-1]
    ngrp = len(POOL_WINDOWS)
    const = lambda i: (0, 0)
    return pl.pallas_call(
        functools.partial(_pool_ln_body, alpha=alpha, pos0=pos0, tiles_per_seq=tiles_per_seq),
        grid=(n_tiles,),
        in_specs=[halo_spec, x_spec,
                  pl.BlockSpec((ngrp, d // ngrp, d // ngrp), lambda i: (0, 0, 0)),
                  pl.BlockSpec((1, d), const), pl.BlockSpec((1, d), const), pl.BlockSpec((1, d), const)],
        out_specs=pl.BlockSpec((tm, d), lambda i: (i, 0)),
        out_shape=jax.ShapeDtypeStruct((n_tiles * tm, d), F32),
        compiler_params=_params("parallel"),
        name="pool_ln",
    )(rows, rows, w_bf, scale, g, b)


def _pool_ln_prompt(x2d, seq_len, w_bf, scale, g, b, alpha):
    m, d = x2d.shape
    tm = _row_tile(seq_len, 512)
    per = tm // POOL_HALO
    halo_spec = pl.BlockSpec((POOL_HALO, d), lambda i: (jnp.maximum(i * per - 1, 0), 0))
    x_spec = pl.BlockSpec((tm, d), lambda i: (i, 0))
    return _pool_ln(x2d, halo_spec, x_spec, m // tm, tm, seq_len // tm, 0, w_bf, scale, g, b, alpha)


def _pool_ln_sample(xe3, ts, pos0, w_bf, scale, g, b, alpha):
    bs, _, d = xe3.shape
    halo_spec = pl.BlockSpec((None, POOL_HALO, d), lambda i: (i, 0, 0))
    x_spec = pl.BlockSpec((None, ts, d), lambda i: (i, POOL_HALO // ts, 0))
    return _pool_ln(xe3, halo_spec, x_spec, bs, ts, 1, pos0, w_bf, scale, g, b, alpha)


def _topk_mask(vals, valid, k, axis):
    n = vals.shape[axis]
    idx = lax.broadcasted_iota(jnp.int32, vals.shape, axis).astype(F32)
    work = jnp.where(valid, vals, -jnp.inf)
    sel = jnp.zeros(vals.shape, F32)
    for _ in range(k):
        m = jnp.max(work, axis=axis, keepdims=True)
        first = jnp.min(jnp.where(work == m, idx, float(n)), axis=axis, keepdims=True)
        pick = idx == first
        sel = jnp.where(pick, 1.0, sel)
        work = jnp.where(pick, -jnp.inf, work)
    return jnp.where(valid, sel, 0.0)


def _cmp_positions(nc, axis):
    shape = (nc, 1) if axis == 0 else (1, nc)
    i = lax.broadcasted_iota(jnp.int32, shape, axis)
    half = nc // 2
    n = jnp.where(i < half, 2 * i, 2 * (i - half) + 1)
    c_end = n * CMP_BLOCK + (CMP_BLOCK - 1)
    c_mid = c_end.astype(F32) - 0.5 * (CMP_BLOCK - 1)
    return c_end, c_mid


def _softmax_init(rows, vdim):
    return (jnp.full((rows, 1), NEG_INF, F32), jnp.zeros((rows, 1), F32), jnp.zeros((rows, vdim), F32))


def _softmax_step(carry, qb, k, v, t, slope, kpos, mask_fn, kv_t):
    m, l, acc = carry
    if kv_t:
        s = jnp.dot(qb, k, preferred_element_type=F32)
    else:
        s = lax.dot_general(qb, k, NT_DIMS, preferred_element_type=F32)
    dist = t - kpos
    s = s - slope * jnp.abs(dist).astype(F32)
    mask = mask_fn(dist)
    s = jnp.where(mask, s, NEG_INF)
    m_new = jnp.maximum(m, jnp.max(s, axis=-1, keepdims=True))
    a = jnp.exp(m - m_new)
    p = jnp.where(mask, jnp.exp(s - m_new), 0.0).astype(BF16)
    l = a * l + jnp.sum(p.astype(F32), axis=-1, keepdims=True)
    if kv_t:
        pv = lax.dot_general(p, v, NT_DIMS, preferred_element_type=F32)
    else:
        pv = jnp.dot(p, v, preferred_element_type=F32)
    return m_new, l, a * acc + pv


def _softmax_done(carry):
    _, l, acc = carry
    return acc / jnp.where(l > 0.0, l, 1.0)


def _block_mask(sel_bf, key0, tk, blk_shift):
    nblk = sel_bf.shape[1]
    kblk = (key0 + lax.broadcasted_iota(jnp.int32, (nblk, tk), 1)) >> blk_shift
    expand = jnp.where(kblk == lax.broadcasted_iota(jnp.int32, (nblk, tk), 0), 1.0, 0.0).astype(BF16)
    return jnp.dot(sel_bf, expand, preferred_element_type=F32) > 0.5


def _cmp_attention(q, kc, vc, t, slope):
    nc = kc.shape[0]
    c_end, c_mid = _cmp_positions(nc, 1)
    s = lax.dot_general(q, kc, NT_DIMS, preferred_element_type=F32, precision=HIGHEST)
    s = s - slope * jnp.abs(t.astype(F32) - c_mid)
    mask = c_end <= t
    s = jnp.where(mask, s, NEG_INF)
    m = jnp.max(s, axis=-1, keepdims=True)
    p = jnp.where(mask, jnp.exp(s - m), 0.0)
    l = jnp.sum(p, axis=-1, keepdims=True)
    p = p / jnp.where(l > 0.0, l, 1.0)
    o = jnp.dot(p.astype(BF16), vc.astype(BF16), preferred_element_type=F32)
    return p, o


def _tile_lanes(x, n):
    return jnp.concatenate([x] * n, axis=1)


def _group_q_t(q_t, g, r_n):
    hd = HEAD_DIM
    return jnp.concatenate([q_t[(g * r_n + r) * hd:(g * r_n + r + 1) * hd] for r in range(r_n)], axis=1)


def _aug_tail(q_t, alibi):
    pad = jnp.zeros((AUG - AUG_OH - HEAD_DIM - AUG_AL, q_t.shape[1]), F32)
    return jnp.concatenate([q_t, alibi, pad], axis=0).astype(BF16)


def _aug_rhs(not_sel, tail):
    pad_oh = AUG_OH - not_sel.shape[0]
    if pad_oh:
        not_sel = jnp.concatenate([not_sel, jnp.zeros((pad_oh, not_sel.shape[1]), F32)], axis=0)
    return jnp.concatenate([tail[:HEAD_DIM], not_sel.astype(BF16), tail[HEAD_DIM:]], axis=0)


def _soft_first_tiles(jobs):
    n = len(jobs)
    s = [None] * n
    s[0] = jobs[0][0]()
    out = []
    for j in range(n):
        if j + 1 < n:
            s[j + 1] = jobs[j + 1][0]()
        m = jnp.max(s[j], axis=0, keepdims=True)
        ref = m
        if j + 1 < n:
            ref = ref + jnp.max(s[j + 1][0:8], axis=0, keepdims=True) * 0.0
        p = jnp.exp2(s[j] - ref)
        out.append((m, jnp.sum(p, axis=0, keepdims=True),
                    jnp.dot(jobs[j][1](), p.astype(BF16), preferred_element_type=F32)))
    return out


def _soft_more_groups(m_sc, l_sc, acc_sc, k_ref, v_ref, key0, tk, rhs_of):
    g_n, hd = m_sc.shape[0], HEAD_DIM
    keys = pl.ds(key0, tk)
    s, m_new = [None] * g_n, [None] * g_n

    def logits(g):
        s[g] = jnp.dot(k_ref[keys, g * AUG:(g + 1) * AUG], rhs_of(g), preferred_element_type=F32)
        m_new[g] = jnp.maximum(m_sc[g], jnp.max(s[g], axis=0, keepdims=True))

    logits(0)
    for g in range(g_n):
        if g + 1 < g_n:
            logits(g + 1)
        a = jnp.exp2(m_sc[g] - m_new[g])
        ref = m_new[g]
        if g + 1 < g_n:
            ref = ref + jnp.max(s[g + 1][0:8], axis=0, keepdims=True) * 0.0
        p = jnp.exp2(s[g] - ref)
        l_sc[g] = a * l_sc[g] + jnp.sum(p, axis=0, keepdims=True)
        acc_sc[g] = a * acc_sc[g] + jnp.dot(v_ref[g * hd:(g + 1) * hd, keys], p.astype(BF16),
                                            preferred_element_type=F32)
        m_sc[g] = m_new[g]


def _soft_done(carry):
    _, l, acc = carry
    return acc / l


HEADS_PER_LANE_TILE = LANE // HEAD_DIM


def _compress_rows(block_rows, pe_ref, w1_ref, w2_ref, nc):
    acc = jnp.zeros((nc, HEADS_PER_LANE_TILE * CMP_HIDDEN), F32)
    for r in range(CMP_BLOCK):
        rows = block_rows(r) + pe_ref[r:r + 1, :]
        acc = acc + jnp.dot(rows.astype(BF16), w1_ref[r], preferred_element_type=F32)
    return jnp.dot(jax.nn.gelu(acc).astype(BF16), w2_ref[...], preferred_element_type=F32)


def _compress_weights(pe, w1, w2):
    n_h = HEADS_PER_LANE_TILE
    eye = jnp.eye(n_h, dtype=w1.dtype)
    w1r = w1.reshape(2, CMP_BLOCK, HEAD_DIM, CMP_HIDDEN)
    w1_bd = jnp.einsum('ab,crdh->cradbh', eye, w1r).reshape(2, CMP_BLOCK, n_h * HEAD_DIM, n_h * CMP_HIDDEN)
    w2_bd = jnp.einsum('ab,chd->cahbd', eye, w2).reshape(2, n_h * CMP_HIDDEN, n_h * HEAD_DIM)
    return jnp.tile(pe, (1, 1, n_h)), w1_bd.astype(BF16), w2_bd.astype(BF16)


def _compress_prompt_body(src_ref, pe_ref, w1_ref, w2_ref, o_ref, *, nc):
    half = nc // 2

    def block_rows(r):
        return jnp.concatenate([src_ref[pl.ds(r, half, stride=2 * CMP_BLOCK), :],
                                src_ref[pl.ds(CMP_BLOCK + r, half, stride=2 * CMP_BLOCK), :]], axis=0)

    o_ref[...] = _compress_rows(block_rows, pe_ref, w1_ref, w2_ref, nc)


def _compress_prompt(proj3, col0, pe, w1_bf, w2_bf):
    b, t, _ = proj3.shape
    gd = NSA_KV_HEADS * HEAD_DIM
    nc = t // CMP_BLOCK
    tiles = gd // LANE
    return pl.pallas_call(
        functools.partial(_compress_prompt_body, nc=nc),
        grid=(2, b, tiles),
        in_specs=[pl.BlockSpec((None, t, LANE), lambda c, i, h: (i, 0, col0 // LANE + c * tiles + h)),
                  pl.BlockSpec((None,) + pe.shape[1:], lambda c, i, h: (c, 0, 0)),
                  pl.BlockSpec((None,) + w1_bf.shape[1:], lambda c, i, h: (c, 0, 0, 0)),
                  pl.BlockSpec((None,) + w2_bf.shape[1:], lambda c, i, h: (c, 0, 0))],
        out_specs=pl.BlockSpec((None, None, nc, LANE), lambda c, i, h: (c, i, 0, h)),
        out_shape=jax.ShapeDtypeStruct((2, b, nc, gd), F32),
        compiler_params=_params("parallel", "parallel", "parallel"),
        name="nsa_compress_prompt",
    )(proj3, pe, w1_bf, w2_bf)


def _gather_pages(pt_ref, b, cache_hbm, layer, row0, nrows, buf, sem, n_pages):
    def copy(p):
        return pltpu.make_async_copy(cache_hbm.at[layer, pt_ref[b, p], pl.ds(row0, nrows), :], buf.at[p], sem)

    def start():
        for p in range(n_pages):
            copy(p).start()

    def wait():
        for p in range(n_pages):
            copy(p).wait()

    return start, wait


ROW_GROUP = 8
PAGES_PER_GROUP = ROW_GROUP * CMP_BLOCK // PAGE_SIZE


def _block_row_permutation():
    n = PAGES_PER_GROUP * PAGE_SIZE
    out_row = np.arange(n)
    src = (out_row % ROW_GROUP) * CMP_BLOCK + out_row // ROW_GROUP
    return (src[:, None] == np.arange(n)[None, :]).astype(np.float32)


def _compress_sample_body(pt_ref, cache_hbm, perm_ref, pe_ref, w1_ref, w2_ref, o_ref, buf, rowbuf, sem,
                          *, layer, n_pages, nc):
    gd = NSA_KV_HEADS * HEAD_DIM
    tiles = gd // LANE
    b = pl.program_id(0)
    gather = lambda seq: _gather_pages(pt_ref, seq, cache_hbm, layer, 0, 2 * gd, buf, sem, n_pages)

    @pl.when(b == 0)
    def _():
        gather(b)[0]()

    gather(b)[1]()

    def to_rows(pg, carry):
        for c in range(2 * tiles):
            x_t = jnp.concatenate([buf[pg * PAGES_PER_GROUP + u, c * LANE:(c + 1) * LANE, :]
                                   for u in range(PAGES_PER_GROUP)], axis=1).astype(BF16)
            rows = lax.dot_general(perm_ref[...], x_t, NT_DIMS, preferred_element_type=F32)
            rowbuf[c, pg] = rows.reshape(CMP_BLOCK, ROW_GROUP, LANE)
        return carry

    lax.fori_loop(0, n_pages // PAGES_PER_GROUP, to_rows, 0)

    @pl.when(b + 1 < pl.num_programs(0))
    def _():
        gather(b + 1)[0]()

    for c in range(2):
        for h in range(tiles):
            src = rowbuf.at[c * tiles + h]
            o_ref[c, :, h * LANE:(h + 1) * LANE] = _compress_rows(
                lambda r: src[:, r].reshape(nc, LANE), pe_ref.at[c], w1_ref.at[c], w2_ref.at[c], nc)


def _compress_sample(cache_t, layer, page_table, pe, w1_bf, w2_bf):
    bs, n_pages = page_table.shape
    gd = NSA_KV_HEADS * HEAD_DIM
    nc = n_pages * PAGE_SIZE // CMP_BLOCK
    assert n_pages % PAGES_PER_GROUP == 0
    perm = jnp.asarray(_block_row_permutation(), BF16)
    tokens = pl.pallas_call(
        functools.partial(_compress_sample_body, layer=layer, n_pages=n_pages, nc=nc),
        grid_spec=pltpu.PrefetchScalarGridSpec(
            num_scalar_prefetch=1,
            grid=(bs,),
            in_specs=[pl.BlockSpec(memory_space=pl.ANY),
                      pl.BlockSpec(perm.shape, lambda i, pt: (0, 0)),
                      pl.BlockSpec(pe.shape, lambda i, pt: (0, 0, 0)),
                      pl.BlockSpec(w1_bf.shape, lambda i, pt: (0, 0, 0, 0)),
                      pl.BlockSpec(w2_bf.shape, lambda i, pt: (0, 0, 0))],
            out_specs=pl.BlockSpec((None, 2, nc, gd), lambda i, pt: (i, 0, 0, 0)),
            scratch_shapes=[pltpu.VMEM((n_pages, 2 * gd, PAGE_SIZE), F32),
                            pltpu.VMEM((2 * gd // LANE, n_pages // PAGES_PER_GROUP, CMP_BLOCK, ROW_GROUP, LANE),
                                       F32),
                            pltpu.SemaphoreType.DMA(())]),
        out_shape=jax.ShapeDtypeStruct((bs, 2, nc, gd), F32),
        compiler_params=_params("arbitrary"),
        name="nsa_compress_sample",
    )(page_table, cache_t, perm, pe, w1_bf, w2_bf)
    return tokens.reshape(bs, 2, nc // 2, 2, gd).transpose(0, 1, 3, 2, 4).reshape(bs, 2, nc, gd)


def _nsa_prompt_body(slope_ref, alibi_ref, q_ref, gate_ref, kc_ref, vc_ref, ks_ref, vs_ref, kw_ref, vw_ref,
                     o_ref, ns_sc, own_sc, tail_sc, m_sc, l_sc, acc_sc, o_sc, *, tq, tk, wk):
    i = pl.program_id(1)
    g_n, r_n, hd = NSA_KV_HEADS, NSA_GROUP, HEAD_DIM
    n_h = g_n * r_n
    rows = r_n * tq
    nc = kc_ref.shape[0]
    ns = nc // 2
    t_q = i * tq + lax.broadcasted_iota(jnp.int32, (1, tq), 1)
    t_row = _tile_lanes(t_q, r_n)
    cur = t_q // SEL_BLOCK
    blk = lax.broadcasted_iota(jnp.int32, (ns, 1), 0)
    q_t = q_ref[...].T * (QK_SCALE * LOG2E)
    gate_t = jax.nn.sigmoid(gate_ref[...]).T
    vc_t = vc_ref[...].T
    c_end, c_mid = _cmp_positions(nc, 0)
    key_i = lax.broadcasted_iota(jnp.int32, (tq, 1), 0)
    causal = _tile_lanes(jnp.where(key_i <= lax.broadcasted_iota(jnp.int32, (1, tq), 1), 0.0, NEG_INF), r_n)
    key_d = pl.multiple_of(i * tq, tq)
    blk_d = pl.multiple_of((i * tq // SEL_BLOCK) // 8 * 8, 8)
    n_past = (i * tq + tk - 1) // tk
    w0 = pl.multiple_of(jnp.maximum((i + 1) * tq - wk, 0), tq)
    w_dist = t_q - (w0 + lax.broadcasted_iota(jnp.int32, (wk, 1), 0))
    in_win = _tile_lanes(jnp.where((w_dist >= 0) & (w_dist < WINDOW), 0.0, NEG_INF), r_n)
    no_sel = jnp.zeros((AUG_OH, rows), F32)

    for g in range(g_n):
        gsl = slice(g * hd, (g + 1) * hd)
        asl = slice(g * AUG, (g + 1) * AUG)
        qg = _group_q_t(q_t, g, r_n)
        slope = slope_ref[g]

        s = jnp.dot(kc_ref[:, gsl], qg, preferred_element_type=F32, precision=HIGHEST)
        s = s - slope * jnp.abs(t_row.astype(F32) - c_mid)
        valid = c_end <= t_row
        s = jnp.where(valid, s, NEG_INF)
        p = jnp.where(valid, jnp.exp2(s - jnp.max(s, axis=0, keepdims=True)), 0.0)
        l = jnp.sum(p, axis=0, keepdims=True)
        p = p / jnp.where(l > 0.0, l, 1.0)
        o_cmp = jnp.dot(vc_t[gsl].astype(BF16), p.astype(BF16), preferred_element_type=F32)
        imp = p[:, 0:tq]
        for r in range(1, r_n):
            imp = imp + p[:, r * tq:(r + 1) * tq]
        imp = imp[:ns] + imp[ns:]
        sel = _topk_mask(imp, blk < cur, min(N_SEL - 1, ns), 0)
        ns_sc[g] = _tile_lanes(1.0 - jnp.where(blk < i * tq // SEL_BLOCK, sel, 0.0), r_n)
        own_sc[g] = _tile_lanes(1.0 - jnp.where(blk == cur, 1.0, sel), r_n)

        tail_sc[g] = _aug_tail(qg, alibi_ref[g])
        for r in range(r_n):
            h = g * r_n + r
            o_sc[h * hd:(h + 1) * hd, :] = gate_t[h:h + 1] * o_cmp[:, r * tq:(r + 1) * tq]

    def own_tile(g):
        return (lambda: jnp.dot(ks_ref[pl.ds(key_d, tq), g * AUG:(g + 1) * AUG],
                                _aug_rhs(own_sc[g, pl.ds(blk_d, 8), :], tail_sc[g]),
                                preferred_element_type=F32) + causal,
                lambda: vs_ref[g * hd:(g + 1) * hd, pl.ds(key_d, tq)])

    def window_tile(g):
        return (lambda: jnp.dot(kw_ref[pl.ds(w0, wk), g * AUG:(g + 1) * AUG], _aug_rhs(no_sel, tail_sc[g]),
                                preferred_element_type=F32) + in_win,
                lambda: vw_ref[g * hd:(g + 1) * hd, pl.ds(w0, wk)])

    states = _soft_first_tiles([job(g) for g in range(g_n) for job in (own_tile, window_tile)])
    for g in range(g_n):
        m_sc[g], l_sc[g], acc_sc[g] = states[2 * g]
        o_win = _soft_done(states[2 * g + 1])
        for r in range(r_n):
            h = g * r_n + r
            o_sc[h * hd:(h + 1) * hd, :] += gate_t[2 * n_h + h:2 * n_h + h + 1] * o_win[:, r * tq:(r + 1) * tq]

    def past(j, carry):
        key0 = pl.multiple_of(j * tk, tk)
        blk0 = pl.multiple_of(j * 8, 8)
        _soft_more_groups(m_sc, l_sc, acc_sc, ks_ref, vs_ref, key0, tk,
                          lambda g: _aug_rhs(ns_sc[g, pl.ds(blk0, 8), :], tail_sc[g]))
        return carry

    lax.fori_loop(0, n_past, past, 0)

    for g in range(g_n):
        o_slc = acc_sc[g] / l_sc[g]
        for r in range(r_n):
            h = g * r_n + r
            o_sc[h * hd:(h + 1) * hd, :] += gate_t[n_h + h:n_h + h + 1] * o_slc[:, r * tq:(r + 1) * tq]
    o_ref[...] = o_sc[...].T


def _nsa_prompt_attn(proj3, k_slc, v_slc_t, k_win, v_win_t, kcvc, slope_rows, alibi_rows, tq):
    b, t, _ = proj3.shape
    g_n = NSA_KV_HEADS
    gd = g_n * HEAD_DIM
    qd = g_n * NSA_GROUP * HEAD_DIM
    nc = kcvc.shape[2]
    rows = NSA_GROUP * tq
    tk = NSA_PAST_TILE
    wk = WINDOW + tq
    assert t % tk == 0 and t >= wk and (nc // 2) % 8 == 0
    k_spec = pl.BlockSpec((None, t, g_n * AUG), lambda bi, i: (bi, 0, 0))
    v_spec = pl.BlockSpec((None, gd, t), lambda bi, i: (bi, 0, 0))
    return pl.pallas_call(
        functools.partial(_nsa_prompt_body, tq=tq, tk=tk, wk=wk),
        grid=(b, t // tq),
        in_specs=[pl.BlockSpec((g_n, 1, rows), lambda bi, i: (0, 0, 0)),
                  pl.BlockSpec((g_n, AUG_AL, rows), lambda bi, i: (0, 0, 0)),
                  pl.BlockSpec((None, tq, qd), lambda bi, i: (bi, i, 0)),
                  pl.BlockSpec((None, tq, LANE), lambda bi, i: (bi, i, (qd + 6 * gd) // LANE)),
                  pl.BlockSpec((None, None, nc, gd), lambda bi, i: (0, bi, 0, 0)),
                  pl.BlockSpec((None, None, nc, gd), lambda bi, i: (1, bi, 0, 0)),
                  k_spec, v_spec, k_spec, v_spec],
        out_specs=pl.BlockSpec((None, tq, qd), lambda bi, i: (bi, i, 0)),
        out_shape=jax.ShapeDtypeStruct((b, t, qd), F32),
        scratch_shapes=[pltpu.VMEM((g_n, nc // 2, rows), F32), pltpu.VMEM((g_n, nc // 2, rows), F32),
                        pltpu.VMEM((g_n, AUG - AUG_OH, rows), BF16),
                        pltpu.VMEM((g_n, 1, rows), F32), pltpu.VMEM((g_n, 1, rows), F32),
                        pltpu.VMEM((g_n, HEAD_DIM, rows), F32), pltpu.VMEM((qd, tq), F32)],
        compiler_params=_params("parallel", "arbitrary"),
        name="nsa_prompt_attn",
    )(slope_rows, alibi_rows, proj3, proj3, kcvc, kcvc, k_slc, v_slc_t, k_win, v_win_t)


def _page_tile(buf, slot, j, tk, lo, hi):
    pages = tk // PAGE_SIZE
    return jnp.concatenate([buf[slot, j * pages + u, lo:hi, :] for u in range(pages)], axis=1).astype(BF16)


def _block_diag_q(q_ref, g_n, r_n):
    hd = HEAD_DIM
    ts = q_ref.shape[0]
    out = []
    for g in range(g_n):
        qg = jnp.concatenate([q_ref[:, (g * r_n + r) * hd:(g * r_n + r + 1) * hd] for r in range(r_n)], axis=0)
        parts = [jnp.zeros((r_n * ts, hd), F32)] * g_n
        parts[g] = qg
        out.append(jnp.concatenate(parts, axis=-1))
    return jnp.concatenate(out, axis=0)


def _nsa_sample_body(pt_ref, slope_ref, q_ref, gate_ref, kcvc_ref, snew_ref, wnew_ref, win_ref, cache_hbm,
                     o_ref, buf, sem, *, layer, n_pages, tk):
    g_n, r_n, hd = NSA_KV_HEADS, NSA_GROUP, HEAD_DIM
    gd = g_n * hd
    ts = q_ref.shape[0]
    rows = g_n * r_n * ts
    past = n_pages * PAGE_SIZE
    b = pl.program_id(0)
    slot = b % 2
    gather = lambda seq, sl: _gather_pages(pt_ref, seq, cache_hbm, layer, 2 * gd, 2 * gd, buf.at[sl],
                                           sem.at[sl], n_pages)

    @pl.when(b == 0)
    def _():
        gather(b, slot)[0]()

    @pl.when(b + 1 < pl.num_programs(0))
    def _():
        gather(b + 1, 1 - slot)[0]()

    qf = _block_diag_q(q_ref, g_n, r_n) * QK_SCALE
    qb = qf.astype(BF16)
    slope = slope_ref[...]
    t = past + lax.broadcasted_iota(jnp.int32, (rows, 1), 0) % ts
    new_pos = past + lax.broadcasted_iota(jnp.int32, (1, ts), 1)
    causal = lambda dist: dist >= 0

    nc = kcvc_ref.shape[1]
    ns = nc // 2
    p_cmp, o_cmp = _cmp_attention(qf, kcvc_ref[0], kcvc_ref[1], t, slope)
    imp = []
    for g in range(g_n):
        base = g * r_n * ts
        acc = p_cmp[base:base + ts]
        for r in range(1, r_n):
            acc = acc + p_cmp[base + r * ts:base + (r + 1) * ts]
        imp.append(acc)
    imp = jnp.concatenate(imp, axis=0)
    imp = imp[:, :ns] + imp[:, ns:]
    sel = _topk_mask(imp, jnp.full(imp.shape, True), min(N_SEL - 1, ns), 1).astype(BF16)
    sel = jnp.concatenate([sel[g * ts:(g + 1) * ts] for g in range(g_n) for _ in range(r_n)], axis=0)

    wp = win_ref.shape[1]
    win_pos = (past - wp) + lax.broadcasted_iota(jnp.int32, (1, wp), 1)
    in_win = lambda dist: (dist >= 0) & (dist < WINDOW)
    carry = _softmax_step(_softmax_init(rows, gd), qb, win_ref[:gd, :].astype(BF16),
                          win_ref[gd:, :].astype(BF16), t, slope, win_pos, in_win, True)
    carry = _softmax_step(carry, qb, wnew_ref[:, :gd].astype(BF16), wnew_ref[:, gd:].astype(BF16),
                          t, slope, new_pos, in_win, False)
    o_win = _softmax_done(carry)

    gather(b, slot)[1]()

    def slc_step(j, carry):
        key0 = pl.multiple_of(j * tk, tk)
        kpos = key0 + lax.broadcasted_iota(jnp.int32, (1, tk), 1)
        in_sel = _block_mask(sel, key0, tk, SEL_BLOCK.bit_length() - 1)
        return _softmax_step(carry, qb, _page_tile(buf, slot, j, tk, 0, gd), _page_tile(buf, slot, j, tk, gd, 2 * gd),
                             t, slope, kpos, lambda dist: in_sel, True)

    carry = lax.fori_loop(0, past // tk, slc_step, _softmax_init(rows, gd))
    carry = _softmax_step(carry, qb, snew_ref[:, :gd].astype(BF16), snew_ref[:, gd:].astype(BF16),
                          t, slope, new_pos, causal, False)
    o_slc = _softmax_done(carry)

    gate = jax.nn.sigmoid(gate_ref[...])
    n_h = g_n * r_n
    for h in range(n_h):
        g = h // r_n
        rs = slice(h * ts, (h + 1) * ts)
        cs = slice(g * hd, (g + 1) * hd)
        o_ref[:, h * hd:(h + 1) * hd] = (gate[:, h:h + 1] * o_cmp[rs, cs]
                                         + gate[:, n_h + h:n_h + h + 1] * o_slc[rs, cs]
                                         + gate[:, 2 * n_h + h:2 * n_h + h + 1] * o_win[rs, cs])


def _nsa_sample_attn(proj3, kcvc, win_t, cache_t, layer, page_table, slopes, tk=512):
    bs, ts, _ = proj3.shape
    n_pages = page_table.shape[1]
    gd = NSA_KV_HEADS * HEAD_DIM
    qd = NSA_KV_HEADS * NSA_GROUP * HEAD_DIM
    nc = kcvc.shape[2]
    wp = win_t.shape[3]
    slope_rows = jnp.repeat(slopes, ts).reshape(-1, 1)
    return pl.pallas_call(
        functools.partial(_nsa_sample_body, layer=layer, n_pages=n_pages, tk=tk),
        grid_spec=pltpu.PrefetchScalarGridSpec(
            num_scalar_prefetch=1,
            grid=(bs,),
            in_specs=[pl.BlockSpec((slope_rows.shape[0], 1), lambda i, pt: (0, 0)),
                      pl.BlockSpec((None, ts, qd), lambda i, pt: (i, 0, 0)),
                      pl.BlockSpec((None, ts, LANE), lambda i, pt: (i, 0, (qd + 6 * gd) // LANE)),
                      pl.BlockSpec((None, 2, nc, gd), lambda i, pt: (i, 0, 0, 0)),
                      pl.BlockSpec((None, ts, 2 * gd), lambda i, pt: (i, 0, (qd + 2 * gd) // (2 * gd))),
                      pl.BlockSpec((None, ts, 2 * gd), lambda i, pt: (i, 0, (qd + 4 * gd) // (2 * gd))),
                      pl.BlockSpec((None, None, 2 * gd, wp), lambda i, pt: (layer, i, 0, 0)),
                      pl.BlockSpec(memory_space=pl.ANY)],
            out_specs=pl.BlockSpec((None, ts, qd), lambda i, pt: (i, 0, 0)),
            scratch_shapes=[pltpu.VMEM((2, n_pages, 2 * gd, PAGE_SIZE), F32), pltpu.SemaphoreType.DMA((2,))]),
        out_shape=jax.ShapeDtypeStruct((bs, ts, qd), F32),
        compiler_params=_params("arbitrary"),
        name="nsa_sample_attn",
    )(page_table, slope_rows, proj3, proj3, kcvc, proj3, proj3, win_t, cache_t)


def _moba_prompt_body(alibi_ref, q_ref, kf_ref, k_ref, v_ref, o_ref, kmean_sc, rhs_sc, m_sc, l_sc, acc_sc, o_sc,
                      *, tq, tk):
    i = pl.program_id(1)
    g_n, r_n, hd = MOBA_KV_HEADS, MOBA_GROUP, HEAD_DIM
    rows = r_n * tq
    nb = kmean_sc.shape[0]

    @pl.when(i == 0)
    def _():
        kmean_sc[...] = jnp.concatenate(
            [jnp.mean(kf_ref[n * MOBA_BLOCK:(n + 1) * MOBA_BLOCK, :], axis=0, keepdims=True) for n in range(nb)],
            axis=0)

    t_q = i * tq + lax.broadcasted_iota(jnp.int32, (1, tq), 1)
    cur = i * tq // MOBA_BLOCK
    blk = lax.broadcasted_iota(jnp.int32, (nb, 1), 0)
    q_t = q_ref[...].T
    own0 = pl.multiple_of(cur * MOBA_BLOCK, MOBA_BLOCK)
    own_pos = own0 + lax.broadcasted_iota(jnp.int32, (MOBA_BLOCK, 1), 0)
    causal = _tile_lanes(jnp.where(own_pos <= t_q, 0.0, NEG_INF), r_n)
    n_past = (cur * MOBA_BLOCK + tk - 1) // tk
    no_sel = jnp.zeros((AUG_OH, rows), F32)

    for g in range(g_n):
        gsl = slice(g * hd, (g + 1) * hd)
        qg = _group_q_t(q_t, g, r_n)
        gs = jnp.dot(kmean_sc[:, gsl], qg, preferred_element_type=F32, precision=HIGHEST)
        sel = _topk_mask(gs, blk < cur, min(MOBA_TOPK, nb), 0)
        rhs_sc[g] = _aug_rhs(1.0 - sel, _aug_tail(qg * (QK_SCALE * LOG2E), alibi_ref[g]))

    def own_tile(g):
        def logits():
            tail = jnp.concatenate([rhs_sc[g, :hd], rhs_sc[g, hd + AUG_OH:]], axis=0)
            return jnp.dot(k_ref[pl.ds(own0, MOBA_BLOCK), g * AUG:(g + 1) * AUG], _aug_rhs(no_sel, tail),
                           preferred_element_type=F32) + causal
        return logits, lambda: v_ref[g * hd:(g + 1) * hd, pl.ds(own0, MOBA_BLOCK)]

    for g, state in enumerate(_soft_first_tiles([own_tile(g) for g in range(g_n)])):
        m_sc[g], l_sc[g], acc_sc[g] = state

    def past(j, carry):
        key0 = pl.multiple_of(j * tk, tk)
        _soft_more_groups(m_sc, l_sc, acc_sc, k_ref, v_ref, key0, tk, lambda g: rhs_sc[g])
        return carry

    lax.fori_loop(0, n_past, past, 0)

    for g in range(g_n):
        o = acc_sc[g] / l_sc[g]
        for r in range(r_n):
            h = g * r_n + r
            o_sc[h * hd:(h + 1) * hd, :] = o[:, r * tq:(r + 1) * tq]
    o_ref[...] = o_sc[...].T


def _moba_prompt_attn(proj3, k_aug, v_t, alibi_rows, tq, tk=512):
    b, t, _ = proj3.shape
    g_n = MOBA_KV_HEADS
    gd = g_n * HEAD_DIM
    qd = g_n * MOBA_GROUP * HEAD_DIM
    rows = MOBA_GROUP * tq
    assert t % tk == 0 and tk % MOBA_BLOCK == 0 and MOBA_BLOCK % tq == 0 and t // MOBA_BLOCK <= AUG_OH
    return pl.pallas_call(
        functools.partial(_moba_prompt_body, tq=tq, tk=tk),
        grid=(b, t // tq),
        in_specs=[pl.BlockSpec((g_n, AUG_AL, rows), lambda bi, i: (0, 0, 0)),
                  pl.BlockSpec((None, tq, qd), lambda bi, i: (bi, i, 0)),
                  pl.BlockSpec((None, t, gd), lambda bi, i: (bi, 0, qd // gd)),
                  pl.BlockSpec((None, t, g_n * AUG), lambda bi, i: (bi, 0, 0)),
                  pl.BlockSpec((None, gd, t), lambda bi, i: (bi, 0, 0))],
        out_specs=pl.BlockSpec((None, tq, qd), lambda bi, i: (bi, i, 0)),
        out_shape=jax.ShapeDtypeStruct((b, t, qd), F32),
        scratch_shapes=[pltpu.VMEM((t // MOBA_BLOCK, gd), F32), pltpu.VMEM((g_n, AUG, rows), BF16),
                        pltpu.VMEM((g_n, 1, rows), F32), pltpu.VMEM((g_n, 1, rows), F32),
                        pltpu.VMEM((g_n, HEAD_DIM, rows), F32), pltpu.VMEM((qd, tq), F32)],
        compiler_params=_params("parallel", "arbitrary"),
        name="moba_prompt_attn",
    )(alibi_rows, proj3, proj3, k_aug, v_t)


def _moba_sample_body(pt_ref, slope_ref, q_ref, new_ref, cache_hbm, o_ref, buf, sem, *, layer, n_pages, tk):
    g_n, r_n, hd = MOBA_KV_HEADS, MOBA_GROUP, HEAD_DIM
    gd = g_n * hd
    ts = q_ref.shape[0]
    rows = g_n * r_n * ts
    past = n_pages * PAGE_SIZE
    nb = past // MOBA_BLOCK
    b = pl.program_id(0)
    slot = b % 2
    gather = lambda seq, sl: _gather_pages(pt_ref, seq, cache_hbm, layer, 0, 2 * gd, buf.at[sl], sem.at[sl],
                                           n_pages)

    @pl.when(b == 0)
    def _():
        gather(b, slot)[0]()

    @pl.when(b + 1 < pl.num_programs(0))
    def _():
        gather(b + 1, 1 - slot)[0]()

    qf = _block_diag_q(q_ref, g_n, r_n)
    qb = (qf * QK_SCALE).astype(BF16)
    slope = slope_ref[...]
    t = past + lax.broadcasted_iota(jnp.int32, (rows, 1), 0) % ts
    new_pos = past + lax.broadcasted_iota(jnp.int32, (1, ts), 1)
    gather(b, slot)[1]()

    blk = lax.broadcasted_iota(jnp.int32, (gd, nb), 1)
    kmean_t = jnp.zeros((gd, nb), F32)
    pages = MOBA_BLOCK // PAGE_SIZE
    for n in range(nb):
        col = sum(jnp.sum(buf[slot, n * pages + u, :gd, :], axis=1, keepdims=True) for u in range(pages))
        kmean_t = jnp.where(blk == n, col / MOBA_BLOCK, kmean_t)
    gs = jnp.dot(qf, kmean_t, preferred_element_type=F32, precision=HIGHEST)
    sel = _topk_mask(gs, jnp.full(gs.shape, True), min(MOBA_TOPK, nb), 1).astype(BF16)

    def step(j, carry):
        key0 = pl.multiple_of(j * tk, tk)
        kpos = key0 + lax.broadcasted_iota(jnp.int32, (1, tk), 1)
        in_sel = _block_mask(sel, key0, tk, MOBA_BLOCK.bit_length() - 1)
        return _softmax_step(carry, qb, _page_tile(buf, slot, j, tk, 0, gd), _page_tile(buf, slot, j, tk, gd, 2 * gd),
                             t, slope, kpos, lambda dist: in_sel, True)

    carry = lax.fori_loop(0, past // tk, step, _softmax_init(rows, gd))
    carry = _softmax_step(carry, qb, new_ref[:, :gd].astype(BF16), new_ref[:, gd:].astype(BF16),
                          t, slope, new_pos, lambda dist: dist >= 0, False)
    o = _softmax_done(carry)
    for h in range(g_n * r_n):
        g = h // r_n
        o_ref[:, h * hd:(h + 1) * hd] = o[h * ts:(h + 1) * ts, g * hd:(g + 1) * hd]


def _moba_sample_attn(proj3, cache_t, layer, page_table, slopes, tk=512):
    bs, ts, _ = proj3.shape
    n_pages = page_table.shape[1]
    gd = MOBA_KV_HEADS * HEAD_DIM
    qd = MOBA_KV_HEADS * MOBA_GROUP * HEAD_DIM
    slope_rows = jnp.repeat(slopes, ts).reshape(-1, 1)
    return pl.pallas_call(
        functools.partial(_moba_sample_body, layer=layer, n_pages=n_pages, tk=tk),
        grid_spec=pltpu.PrefetchScalarGridSpec(
            num_scalar_prefetch=1,
            grid=(bs,),
            in_specs=[pl.BlockSpec((slope_rows.shape[0], 1), lambda i, pt: (0, 0)),
                      pl.BlockSpec((None, ts, qd), lambda i, pt: (i, 0, 0)),
                      pl.BlockSpec((None, ts, 2 * gd), lambda i, pt: (i, 0, qd // (2 * gd))),
                      pl.BlockSpec(memory_space=pl.ANY)],
            out_specs=pl.BlockSpec((None, ts, qd), lambda i, pt: (i, 0, 0)),
            scratch_shapes=[pltpu.VMEM((2, n_pages, 2 * gd, PAGE_SIZE), F32), pltpu.SemaphoreType.DMA((2,))]),
        out_shape=jax.ShapeDtypeStruct((bs, ts, qd), F32),
        compiler_params=_params("arbitrary"),
        name="moba_sample_attn",
    )(page_table, slope_rows, proj3, proj3, cache_t)


PROMPT_Q_TILE = 128


def _alibi_slopes(n_heads):
    return jnp.exp2(-8.0 * jnp.arange(1, n_heads + 1, dtype=F32) / n_heads)


def _alibi_query_rows(slopes, g_n, r_n, tq):
    s2 = slopes * LOG2E
    d0 = s2.astype(BF16).astype(F32)
    d1 = (s2 - d0).astype(BF16).astype(F32)
    d2 = (s2 - d0 - d1).astype(BF16).astype(F32)
    zero = jnp.zeros_like(s2)
    digits = jnp.stack([d0, d1, d2, POS_DIGIT * d0, POS_DIGIT * d1, POS_DIGIT * d2, zero, zero], axis=0)
    lanes = lambda a: jnp.repeat(a.reshape(-1, g_n, r_n), tq, axis=2).transpose(1, 0, 2)
    return lanes(s2[None]), lanes(digits)


def _positions_minor(a, n_lead):
    nd = a.ndim
    a = jnp.transpose(a, tuple(range(n_lead)) + tuple(range(n_lead + 1, nd)) + (n_lead,))
    return a.reshape(a.shape[:n_lead] + (-1, a.shape[-1]))


def _nsa_layer(xp, xs, bp, bs, cache_t, win_t, layer, win_state, page_table, w_in, w_out, pe, w1, w2, g, b,
               alpha):
    tp, ts = xp.shape[0] // bp, xs.shape[0] // bs
    g_n, hd = NSA_KV_HEADS, HEAD_DIM
    gd = g_n * hd
    qd = g_n * NSA_GROUP * hd
    n_in = w_in.shape[1]
    n_pad = -(-n_in // LANE) * LANE
    w_in_bf = jnp.pad(w_in, ((0, 0), (0, n_pad - n_in))).astype(BF16)
    pe, w1_bf, w2_bf = _compress_weights(pe, w1, w2)
    w_out_bf = w_out.astype(BF16)
    slopes = _alibi_slopes(g_n * NSA_GROUP)
    past = page_table.shape[1] * PAGE_SIZE
    assert PAGE_SIZE == LANE and past % (2 * CMP_BLOCK) == 0 and ts <= CMP_BLOCK

    assert tp <= POS_DIGIT * 256
    proj_p, k_slc, k_win, v_slc_t, v_win_t = _proj(
        xp, w_in_bf, tp, keys=((qd + 2 * gd, SEL_BLOCK, NSA_PAST_TILE // SEL_BLOCK), (qd + 4 * gd, SEL_BLOCK, 0)),
        values=(qd + 3 * gd, qd + 5 * gd), g_n=g_n)
    (proj_s,) = _proj(xs, w_in_bf)
    proj_p3 = proj_p.reshape(bp, tp, n_pad)
    proj_s3 = proj_s.reshape(bs, ts, n_pad)
    kcvc_p = _compress_prompt(proj_p3, qd, pe, w1_bf, w2_bf)
    slope_rows, alibi_rows = _alibi_query_rows(slopes, g_n, NSA_GROUP, PROMPT_Q_TILE)
    o_p = _nsa_prompt_attn(proj_p3, k_slc.reshape(bp, tp, g_n * AUG), v_slc_t, k_win.reshape(bp, tp, g_n * AUG),
                           v_win_t, kcvc_p, slope_rows, alibi_rows, PROMPT_Q_TILE)
    kcvc_s = _compress_sample(cache_t, layer, page_table, pe, w1_bf, w2_bf)
    o_s = _nsa_sample_attn(proj_s3, kcvc_s, win_t, cache_t, layer, page_table, slopes)
    xp = _out_ln(o_p.reshape(bp * tp, qd), xp, w_out_bf, g, b, alpha)
    xs = _out_ln(o_s.reshape(bs * ts, qd), xs, w_out_bf, g, b, alpha)

    kv_shape = (4, g_n, hd)
    win_shape = (2, g_n, hd)
    kv_p = proj_p3[:, :, qd:qd + 4 * gd].reshape((bp, tp) + kv_shape)
    kv_s = proj_s3[:, :, qd:qd + 4 * gd].reshape((bs, ts) + kv_shape)
    win_p = proj_p3[:, tp - min(WINDOW, tp):, qd + 4 * gd:qd + 6 * gd].reshape((bp, min(WINDOW, tp)) + win_shape)
    win_s = jnp.concatenate([win_state, proj_s3[:, :, qd + 4 * gd:qd + 6 * gd].reshape((bs, ts) + win_shape)], axis=1)
    win_s = win_s[:, win_s.shape[1] - min(WINDOW, win_s.shape[1]):]
    return xp, xs, kv_p, kv_s, win_p, win_s


def _pool_layer(xp, xs, bp, bs, state, past, w, scale, g, b, alpha):
    d = xp.shape[1]
    tp, ts = xp.shape[0] // bp, xs.shape[0] // bs
    assert POOL_HALO % ts == 0 and tp % POOL_HALO == 0
    w_bf = w.astype(BF16)
    scale = scale.reshape(1, d)
    xs3 = xs.reshape(bs, ts, d)
    xe_s = jnp.concatenate([jnp.zeros((bs, POOL_HALO - POOL_PAST, d), F32), state, xs3], axis=1)
    pool_p = xp.reshape(bp, tp, d)[:, tp - POOL_PAST:]
    pool_s = xe_s[:, xe_s.shape[1] - POOL_PAST:]
    xp = _pool_ln_prompt(xp, tp, w_bf, scale, g, b, alpha)
    xs = _pool_ln_sample(xe_s, ts, past, w_bf, scale, g, b, alpha)
    return xp, xs, pool_p, pool_s


def _moba_layer(xp, xs, bp, bs, cache_t, layer, page_table, w_in, w_out, g, b, alpha):
    tp, ts = xp.shape[0] // bp, xs.shape[0] // bs
    g_n, hd = MOBA_KV_HEADS, HEAD_DIM
    gd = g_n * hd
    qd = g_n * MOBA_GROUP * hd
    n_in = w_in.shape[1]
    w_in_bf, w_out_bf = w_in.astype(BF16), w_out.astype(BF16)
    slopes = _alibi_slopes(g_n * MOBA_GROUP)
    past = page_table.shape[1] * PAGE_SIZE
    assert PAGE_SIZE == LANE and past % MOBA_BLOCK == 0 and ts <= MOBA_BLOCK

    assert tp <= POS_DIGIT * 256
    proj_p, k_aug, v_t = _proj(xp, w_in_bf, tp, keys=((qd, MOBA_BLOCK, AUG_OH),), values=(qd + gd,), g_n=g_n)
    (proj_s,) = _proj(xs, w_in_bf)
    proj_p3 = proj_p.reshape(bp, tp, n_in)
    proj_s3 = proj_s.reshape(bs, ts, n_in)
    _, alibi_rows = _alibi_query_rows(slopes, g_n, MOBA_GROUP, PROMPT_Q_TILE)
    o_p = _moba_prompt_attn(proj_p3, k_aug.reshape(bp, tp, g_n * AUG), v_t, alibi_rows, PROMPT_Q_TILE)
    o_s = _moba_sample_attn(proj_s3, cache_t, layer, page_table, slopes)
    xp = _out_ln(o_p.reshape(bp * tp, qd), xp, w_out_bf, g, b, alpha)
    xs = _out_ln(o_s.reshape(bs * ts, qd), xs, w_out_bf, g, b, alpha)
    kv_shape = (2, g_n, hd)
    kv_p = proj_p3[:, :, qd:].reshape((bp, tp) + kv_shape)
    kv_s = proj_s3[:, :, qd:].reshape((bs, ts) + kv_shape)
    return xp, xs, kv_p, kv_s


def kernel(x_prompt, x_sample, cache_nsa_kv, state_nsa_win, state_pool, cache_moba, page_table, ln_g, ln_b, mlp_w1, mlp_w2, nsa_w_in, nsa_w_out, nsa_cmp_pe, nsa_cmp_w1, nsa_cmp_w2, pool_w, pool_scale, moba_w_in, moba_w_out):
    bp, tp, d = x_prompt.shape
    bs, ts, _ = x_sample.shape
    depth = ln_g.shape[0]
    alpha = (2 * depth) ** 0.25
    past = page_table.shape[1] * PAGE_SIZE
    xp = x_prompt.reshape(bp * tp, d)
    xs = x_sample.reshape(bs * ts, d)
    nsa_cache_t = _positions_minor(cache_nsa_kv, 2)
    nsa_win_t = _positions_minor(state_nsa_win, 2)
    moba_cache_t = _positions_minor(cache_moba, 2)
    outs = {k: [] for k in ("nsa_kv_p", "nsa_kv_s", "nsa_win_p", "nsa_win_s", "pool_p", "pool_s", "moba_p", "moba_s")}
    for i in range(depth):
        kind, j = i % N_MIXERS, i // N_MIXERS
        g0, b0 = ln_g[i, 0].reshape(1, d), ln_b[i, 0].reshape(1, d)
        g1, b1 = ln_g[i, 1].reshape(1, d), ln_b[i, 1].reshape(1, d)
        if kind == 0:
            xp, xs, kv_p, kv_s, win_p, win_s = _nsa_layer(
                xp, xs, bp, bs, nsa_cache_t, nsa_win_t, j, state_nsa_win[j], page_table, nsa_w_in[j], nsa_w_out[j],
                nsa_cmp_pe[j], nsa_cmp_w1[j], nsa_cmp_w2[j], g0, b0, alpha)
            outs["nsa_kv_p"].append(kv_p); outs["nsa_kv_s"].append(kv_s)
            outs["nsa_win_p"].append(win_p); outs["nsa_win_s"].append(win_s)
        elif kind == 1:
            xp, xs, pool_p, pool_s = _pool_layer(xp, xs, bp, bs, state_pool[j], past, pool_w[j], pool_scale[j],
                                                 g0, b0, alpha)
            outs["pool_p"].append(pool_p); outs["pool_s"].append(pool_s)
        else:
            xp, xs, kv_p, kv_s = _moba_layer(xp, xs, bp, bs, moba_cache_t, j, page_table, moba_w_in[j],
                                             moba_w_out[j], g0, b0, alpha)
            outs["moba_p"].append(kv_p); outs["moba_s"].append(kv_s)
        w1_bf, w2_bf = mlp_w1[i].astype(BF16), mlp_w2[i].astype(BF16)
        xp = _mlp_ln(xp, w1_bf, w2_bf, g1, b1, alpha)
        xs = _mlp_ln(xs, w1_bf, w2_bf, g1, b1, alpha)
    return (xp.reshape(bp, tp, d), xs.reshape(bs, ts, d),
            jnp.stack(outs["nsa_kv_p"]), jnp.stack(outs["nsa_kv_s"]),
            jnp.stack(outs["nsa_win_p"]), jnp.stack(outs["nsa_win_s"]),
            jnp.stack(outs["pool_p"]), jnp.stack(outs["pool_s"]),
            jnp.stack(outs["moba_p"]), jnp.stack(outs["moba_s"]))
```

```python
import functools

import jax
import jax.numpy as jnp
import numpy as np
from jax import lax
from jax.experimental import pallas as pl
from jax.experimental.pallas import tpu as pltpu

F32 = jnp.float32
BF16 = jnp.bfloat16

HEAD_DIM = 64
PAGE_SIZE = 128
N_MIXERS = 3
NSA_KV_HEADS = 4
NSA_GROUP = 4
CMP_BLOCK = 32
SEL_BLOCK = 64
N_SEL = 16
WINDOW = 512
CMP_HIDDEN = 2 * HEAD_DIM
POOL_WINDOWS = (2, 4, 8, 16)
POOL_PAST = max(POOL_WINDOWS) - 1
POOL_HALO = 16
MOBA_KV_HEADS = 4
MOBA_GROUP = 4
MOBA_BLOCK = 256
MOBA_TOPK = 3
LN_EPS = 1e-5
NEG_INF = -1e30
QK_SCALE = HEAD_DIM ** -0.5
LOG2E = 1.4426950408889634

LANE = 128
VMEM_LIMIT = 56 * 1024 * 1024
NT_DIMS = (((1,), (1,)), ((), ()))
HIGHEST = lax.Precision.HIGHEST

AUG = LANE
AUG_OH = 16
AUG_AL = 8
AUG_AL0 = HEAD_DIM + AUG_OH
POS_DIGIT = 16
NSA_PAST_TILE = 8 * SEL_BLOCK


def _params(*sem):
    return pltpu.CompilerParams(dimension_semantics=sem, vmem_limit_bytes=VMEM_LIMIT)


def _layer_norm(y, g, b):
    mu = jnp.mean(y, axis=-1, keepdims=True)
    d = y - mu
    var = jnp.mean(d * d, axis=-1, keepdims=True)
    return d * lax.rsqrt(var + LN_EPS) * g + b


def _row_tile(m, want):
    t = min(m, want)
    while m % t:
        t //= 2
    return t


def _proj_body(x_ref, w_ref, o_ref, *aug_refs, keys, values, tm, seq_len, g_n):
    acc = jnp.dot(x_ref[...].astype(BF16), w_ref[...], preferred_element_type=F32)
    o_ref[...] = acc
    if not aug_refs:
        return
    pos = (pl.program_id(0) * tm) % seq_len + lax.broadcasted_iota(jnp.int32, (tm, AUG), 0)
    lane = lax.broadcasted_iota(jnp.int32, (tm, AUG), 1)
    lo_digit = (lane >= AUG_AL0) & (lane < AUG_AL0 + 3)
    hi_digit = (lane >= AUG_AL0 + 3) & (lane < AUG_AL0 + 6)
    digits = jnp.where(lo_digit, pos % POS_DIGIT, jnp.where(hi_digit, pos // POS_DIGIT, 0)).astype(F32)
    for k_ref, (col0, blk_len, n_onehot) in zip(aug_refs, keys):
        extra = digits
        if n_onehot:
            extra = jnp.where(lane - HEAD_DIM == (pos // blk_len) % n_onehot, NEG_INF, digits)
        for g in range(g_n):
            col = col0 + g * HEAD_DIM
            piece = acc[:, col // LANE * LANE:(col // LANE + 1) * LANE]
            if col % LANE:
                piece = pltpu.roll(piece, LANE - col % LANE, 1)
            k_ref[:, g * AUG:(g + 1) * AUG] = jnp.where(lane < HEAD_DIM, piece, extra).astype(BF16)
    for v_ref, col0 in zip(aug_refs[len(keys):], values):
        v_ref[...] = acc[:, col0:col0 + g_n * HEAD_DIM].T.astype(BF16)


def _proj(x, w_bf, seq_len=None, keys=(), values=(), g_n=0):
    m, d = x.shape
    n = w_bf.shape[1]
    tm = _row_tile(seq_len or m, 512)
    per_seq = (seq_len or m) // tm
    gd = g_n * HEAD_DIM
    out_specs = [pl.BlockSpec((tm, n), lambda i: (i, 0))]
    out_shape = [jax.ShapeDtypeStruct((m, n), F32)]
    out_specs += [pl.BlockSpec((tm, g_n * AUG), lambda i: (i, 0)) for _ in keys]
    out_shape += [jax.ShapeDtypeStruct((m, g_n * AUG), BF16) for _ in keys]
    out_specs += [pl.BlockSpec((None, gd, tm), lambda i: (i // per_seq, 0, i % per_seq)) for _ in values]
    out_shape += [jax.ShapeDtypeStruct((m // (seq_len or m), gd, seq_len or m), BF16) for _ in values]
    return pl.pallas_call(
        functools.partial(_proj_body, keys=keys, values=values, tm=tm, seq_len=seq_len, g_n=g_n),
        grid=(m // tm,),
        in_specs=[pl.BlockSpec((tm, d), lambda i: (i, 0)),
                  pl.BlockSpec((d, n), lambda i: (0, 0))],
        out_specs=out_specs,
        out_shape=out_shape,
        compiler_params=_params("parallel"),
        name="proj",
    )(x, w_bf)


def _out_ln_body(o_ref, x_ref, w_ref, g_ref, b_ref, y_ref, *, alpha):
    h = jnp.dot(o_ref[...].astype(BF16), w_ref[...], preferred_element_type=F32)
    y_ref[...] = _layer_norm(alpha * x_ref[...] + h, g_ref[...], b_ref[...])


def _out_ln(o, x, w_bf, g, b, alpha):
    m, d = x.shape
    k = o.shape[1]
    tm = _row_tile(m, 512)
    return pl.pallas_call(
        functools.partial(_out_ln_body, alpha=alpha),
        grid=(m // tm,),
        in_specs=[pl.BlockSpec((tm, k), lambda i: (i, 0)),
                  pl.BlockSpec((tm, d), lambda i: (i, 0)),
                  pl.BlockSpec((k, d), lambda i: (0, 0)),
                  pl.BlockSpec((1, d), lambda i: (0, 0)),
                  pl.BlockSpec((1, d), lambda i: (0, 0))],
        out_specs=pl.BlockSpec((tm, d), lambda i: (i, 0)),
        out_shape=jax.ShapeDtypeStruct((m, d), F32),
        compiler_params=_params("parallel"),
        name="out_ln",
    )(o, x, w_bf, g, b)


MLP_F_CHUNK = 1024


def _mlp_ln_body(x_ref, w1_ref, w2_ref, g_ref, b_ref, y_ref, *, alpha):
    x = x_ref[...]
    xb = x.astype(BF16)
    acc = None
    for c in range(w1_ref.shape[1] // MLP_F_CHUNK):
        fs = slice(c * MLP_F_CHUNK, (c + 1) * MLP_F_CHUNK)
        h = jnp.maximum(jnp.dot(xb, w1_ref[:, fs], preferred_element_type=F32), 0.0)
        part = jnp.dot((h * h).astype(BF16), w2_ref[fs, :], preferred_element_type=F32)
        acc = part if acc is None else acc + part
    y_ref[...] = _layer_norm(alpha * x + acc, g_ref[...], b_ref[...])


def _mlp_ln(x, w1_bf, w2_bf, g, b, alpha):
    m, d = x.shape
    f = w1_bf.shape[1]
    tm = _row_tile(m, 512)
    once = pl.Buffered(1)
    return pl.pallas_call(
        functools.partial(_mlp_ln_body, alpha=alpha),
        grid=(m // tm,),
        in_specs=[pl.BlockSpec((tm, d), lambda i: (i, 0)),
                  pl.BlockSpec((d, f), lambda i: (0, 0), pipeline_mode=once),
                  pl.BlockSpec((f, d), lambda i: (0, 0), pipeline_mode=once),
                  pl.BlockSpec((1, d), lambda i: (0, 0)),
                  pl.BlockSpec((1, d), lambda i: (0, 0))],
        out_specs=pl.BlockSpec((tm, d), lambda i: (i, 0)),
        out_shape=jax.ShapeDtypeStruct((m, d), F32),
        compiler_params=_params("parallel"),
        name="mlp_ln",
    )(x, w1_bf, w2_bf, g, b)


def _pool_ln_body(halo_ref, x_ref, w_ref, scale_ref, g_ref, b_ref, y_ref, *, alpha, pos0, tiles_per_seq):
    tm, d = x_ref.shape
    grp = d // len(POOL_WINDOWS)
    seq_tile = pl.program_id(0) % tiles_per_seq
    x = x_ref[...]
    halo = halo_ref[...]
    if pos0 == 0:
        halo = jnp.where(seq_tile == 0, 0.0, halo)
    xe = jnp.concatenate([halo, x], axis=0)
    n_avail = pos0 + seq_tile * tm + lax.broadcasted_iota(jnp.int32, (tm, 1), 0) + 1
    mixed = []
    for gi, w in enumerate(POOL_WINDOWS):
        s = xe[:, gi * grp:(gi + 1) * grp]
        step = 1
        while step < w:
            s = s + pltpu.roll(s, step, 0)
            step *= 2
        cnt = jnp.minimum(n_avail, w).astype(F32)
        pooled = s[POOL_HALO:] / cnt - x[:, gi * grp:(gi + 1) * grp]
        mixed.append(jnp.dot(pooled.astype(BF16), w_ref[gi], preferred_element_type=F32))
    h = jnp.concatenate(mixed, axis=-1) * scale_ref[...]
    y_ref[...] = _layer_norm(alpha * x + h, g_ref[...], b_ref[...])


def _pool_ln(rows, halo_spec, x_spec, n_tiles, tm, tiles_per_seq, pos0, w_bf, scale, g, b, alpha):
    d = rows.shape[-1]
    ngrp = len(POOL_WINDOWS)
    const = lambda i: (0, 0)
    return pl.pallas_call(
        functools.partial(_pool_ln_body, alpha=alpha, pos0=pos0, tiles_per_seq=tiles_per_seq),
        grid=(n_tiles,),
        in_specs=[halo_spec, x_spec,
                  pl.BlockSpec((ngrp, d // ngrp, d // ngrp), lambda i: (0, 0, 0)),
                  pl.BlockSpec((1, d), const), pl.BlockSpec((1, d), const), pl.BlockSpec((1, d), const)],
        out_specs=pl.BlockSpec((tm, d), lambda i: (i, 0)),
        out_shape=jax.ShapeDtypeStruct((n_tiles * tm, d), F32),
        compiler_params=_params("parallel"),
        name="pool_ln",
    )(rows, rows, w_bf, scale, g, b)


def _pool_ln_prompt(x2d, seq_len, w_bf, scale, g, b, alpha):
    m, d = x2d.shape
    tm = _row_tile(seq_len, 512)
    per = tm // POOL_HALO
    halo_spec = pl.BlockSpec((POOL_HALO, d), lambda i: (jnp.maximum(i * per - 1, 0), 0))
    x_spec = pl.BlockSpec((tm, d), lambda i: (i, 0))
    return _pool_ln(x2d, halo_spec, x_spec, m // tm, tm, seq_len // tm, 0, w_bf, scale, g, b, alpha)


def _pool_ln_sample(xe3, ts, pos0, w_bf, scale, g, b, alpha):
    bs, _, d = xe3.shape
    halo_spec = pl.BlockSpec((None, POOL_HALO, d), lambda i: (i, 0, 0))
    x_spec = pl.BlockSpec((None, ts, d), lambda i: (i, POOL_HALO // ts, 0))
    return _pool_ln(xe3, halo_spec, x_spec, bs, ts, 1, pos0, w_bf, scale, g, b, alpha)


def _topk_mask(vals, valid, k, axis):
    n = vals.shape[axis]
    idx = lax.broadcasted_iota(jnp.int32, vals.shape, axis).astype(F32)
    work = jnp.where(valid, vals, -jnp.inf)
    sel = jnp.zeros(vals.shape, F32)
    for _ in range(k):
        m = jnp.max(work, axis=axis, keepdims=True)
        first = jnp.min(jnp.where(work == m, idx, float(n)), axis=axis, keepdims=True)
        pick = idx == first
        sel = jnp.where(pick, 1.0, sel)
        work = jnp.where(pick, -jnp.inf, work)
    return jnp.where(valid, sel, 0.0)


def _cmp_positions(nc, axis):
    shape = (nc, 1) if axis == 0 else (1, nc)
    i = lax.broadcasted_iota(jnp.int32, shape, axis)
    half = nc // 2
    n = jnp.where(i < half, 2 * i, 2 * (i - half) + 1)
    c_end = n * CMP_BLOCK + (CMP_BLOCK - 1)
    c_mid = c_end.astype(F32) - 0.5 * (CMP_BLOCK - 1)
    return c_end, c_mid


def _softmax_init(rows, vdim):
    return (jnp.full((rows, 1), NEG_INF, F32), jnp.zeros((rows, 1), F32), jnp.zeros((rows, vdim), F32))


def _sample_logits(qb, k, t, slope, kpos, mask_fn, kv_t):
    if kv_t:
        s = jnp.dot(qb, k, preferred_element_type=F32)
    else:
        s = lax.dot_general(qb, k, NT_DIMS, preferred_element_type=F32)
    dist = t - kpos
    s = s - slope * jnp.abs(dist).astype(F32)
    mask = mask_fn(dist)
    return jnp.where(mask, s, NEG_INF), mask


def _sample_update(carry, s, mask, v, kv_t, wait_for=None):
    m, l, acc = carry
    m_new = jnp.maximum(m, jnp.max(s, axis=-1, keepdims=True))
    a = jnp.exp(m - m_new)
    ref = m_new
    if wait_for is not None:
        ref = ref + jnp.max(wait_for[0:8, 0:LANE], axis=(0, 1), keepdims=True) * 0.0
    p = jnp.where(mask, jnp.exp(s - ref), 0.0).astype(BF16)
    l = a * l + jnp.sum(p.astype(F32), axis=-1, keepdims=True)
    if kv_t:
        pv = lax.dot_general(p, v, NT_DIMS, preferred_element_type=F32)
    else:
        pv = jnp.dot(p, v, preferred_element_type=F32)
    return m_new, l, a * acc + pv


def _softmax_step(carry, qb, k, v, t, slope, kpos, mask_fn, kv_t):
    s, mask = _sample_logits(qb, k, t, slope, kpos, mask_fn, kv_t)
    return _sample_update(carry, s, mask, v, kv_t)


def _paged_tile_pair(carry, jj, qb, buf, slot, tk, gd, t, slope, sel, blk_shift):
    s, mask = [], []
    for u in range(2):
        j = 2 * jj + u
        key0 = pl.multiple_of(j * tk, tk)
        kpos = key0 + lax.broadcasted_iota(jnp.int32, (1, tk), 1)
        in_sel = _block_mask(sel, key0, tk, blk_shift)
        su, mu = _sample_logits(qb, _page_tile(buf, slot, j, tk, 0, gd), t, slope, kpos, lambda dist: in_sel, True)
        s.append(su)
        mask.append(mu)
    carry = _sample_update(carry, s[0], mask[0], _page_tile(buf, slot, 2 * jj, tk, gd, 2 * gd), True, s[1])
    return _sample_update(carry, s[1], mask[1], _page_tile(buf, slot, 2 * jj + 1, tk, gd, 2 * gd), True)


def _softmax_done(carry):
    _, l, acc = carry
    return acc / jnp.where(l > 0.0, l, 1.0)


def _block_mask(sel_bf, key0, tk, blk_shift):
    nblk = sel_bf.shape[1]
    kblk = (key0 + lax.broadcasted_iota(jnp.int32, (nblk, tk), 1)) >> blk_shift
    expand = jnp.where(kblk == lax.broadcasted_iota(jnp.int32, (nblk, tk), 0), 1.0, 0.0).astype(BF16)
    return jnp.dot(sel_bf, expand, preferred_element_type=F32) > 0.5


def _cmp_attention(q, kc, vc, t, slope):
    nc = kc.shape[0]
    c_end, c_mid = _cmp_positions(nc, 1)
    s = lax.dot_general(q, kc, NT_DIMS, preferred_element_type=F32, precision=HIGHEST)
    s = s - slope * jnp.abs(t.astype(F32) - c_mid)
    mask = c_end <= t
    s = jnp.where(mask, s, NEG_INF)
    m = jnp.max(s, axis=-1, keepdims=True)
    p = jnp.where(mask, jnp.exp(s - m), 0.0)
    l = jnp.sum(p, axis=-1, keepdims=True)
    p = p / jnp.where(l > 0.0, l, 1.0)
    o = jnp.dot(p.astype(BF16), vc.astype(BF16), preferred_element_type=F32)
    return p, o


def _tile_lanes(x, n):
    return jnp.concatenate([x] * n, axis=1)


def _group_q_t(q_t, g, r_n):
    hd = HEAD_DIM
    return jnp.concatenate([q_t[(g * r_n + r) * hd:(g * r_n + r + 1) * hd] for r in range(r_n)], axis=1)


def _aug_tail(q_t, alibi):
    pad = jnp.zeros((AUG - AUG_OH - HEAD_DIM - AUG_AL, q_t.shape[1]), F32)
    return jnp.concatenate([q_t, alibi, pad], axis=0).astype(BF16)


def _aug_rhs(not_sel, tail):
    pad_oh = AUG_OH - not_sel.shape[0]
    if pad_oh:
        not_sel = jnp.concatenate([not_sel, jnp.zeros((pad_oh, not_sel.shape[1]), F32)], axis=0)
    return jnp.concatenate([tail[:HEAD_DIM], not_sel.astype(BF16), tail[HEAD_DIM:]], axis=0)


def _soft_first_tiles(jobs):
    n = len(jobs)
    s = [None] * n
    s[0] = jobs[0][0]()
    out = []
    for j in range(n):
        if j + 1 < n:
            s[j + 1] = jobs[j + 1][0]()
        m = jnp.max(s[j], axis=0, keepdims=True)
        ref = m
        if j + 1 < n:
            ref = ref + jnp.max(s[j + 1][0:8], axis=0, keepdims=True) * 0.0
        p = jnp.exp2(s[j] - ref)
        out.append((m, jnp.sum(p, axis=0, keepdims=True),
                    jnp.dot(jobs[j][1](), p.astype(BF16), preferred_element_type=F32)))
    return out


def _soft_more_groups(m_sc, l_sc, acc_sc, k_ref, v_ref, key0, tk, rhs_of):
    g_n, hd = m_sc.shape[0], HEAD_DIM
    keys = pl.ds(key0, tk)
    s, m_new = [None] * g_n, [None] * g_n

    def logits(g):
        s[g] = jnp.dot(k_ref[keys, g * AUG:(g + 1) * AUG], rhs_of(g), preferred_element_type=F32)
        m_new[g] = jnp.maximum(m_sc[g], jnp.max(s[g], axis=0, keepdims=True))

    logits(0)
    for g in range(g_n):
        if g + 1 < g_n:
            logits(g + 1)
        a = jnp.exp2(m_sc[g] - m_new[g])
        ref = m_new[g]
        if g + 1 < g_n:
            ref = ref + jnp.max(s[g + 1][0:8], axis=0, keepdims=True) * 0.0
        p = jnp.exp2(s[g] - ref)
        l_sc[g] = a * l_sc[g] + jnp.sum(p, axis=0, keepdims=True)
        acc_sc[g] = a * acc_sc[g] + jnp.dot(v_ref[g * hd:(g + 1) * hd, keys], p.astype(BF16),
                                            preferred_element_type=F32)
        m_sc[g] = m_new[g]


def _soft_done(carry):
    _, l, acc = carry
    return acc / l


HEADS_PER_LANE_TILE = LANE // HEAD_DIM


def _compress_rows(block_rows, pe_ref, w1_ref, w2_ref, nc):
    acc = jnp.zeros((nc, HEADS_PER_LANE_TILE * CMP_HIDDEN), F32)
    for r in range(CMP_BLOCK):
        rows = block_rows(r) + pe_ref[r:r + 1, :]
        acc = acc + jnp.dot(rows.astype(BF16), w1_ref[r], preferred_element_type=F32)
    return jnp.dot(jax.nn.gelu(acc).astype(BF16), w2_ref[...], preferred_element_type=F32)


def _compress_weights(pe, w1, w2):
    n_h = HEADS_PER_LANE_TILE
    eye = jnp.eye(n_h, dtype=w1.dtype)
    w1r = w1.reshape(2, CMP_BLOCK, HEAD_DIM, CMP_HIDDEN)
    w1_bd = jnp.einsum('ab,crdh->cradbh', eye, w1r).reshape(2, CMP_BLOCK, n_h * HEAD_DIM, n_h * CMP_HIDDEN)
    w2_bd = jnp.einsum('ab,chd->cahbd', eye, w2).reshape(2, n_h * CMP_HIDDEN, n_h * HEAD_DIM)
    return jnp.tile(pe, (1, 1, n_h)), w1_bd.astype(BF16), w2_bd.astype(BF16)


def _compress_prompt_body(src_ref, pe_ref, w1_ref, w2_ref, o_ref, *, nc):
    half = nc // 2

    def block_rows(r):
        return jnp.concatenate([src_ref[pl.ds(r, half, stride=2 * CMP_BLOCK), :],
                                src_ref[pl.ds(CMP_BLOCK + r, half, stride=2 * CMP_BLOCK), :]], axis=0)

    o_ref[...] = _compress_rows(block_rows, pe_ref, w1_ref, w2_ref, nc)


def _compress_prompt(proj3, col0, pe, w1_bf, w2_bf):
    b, t, _ = proj3.shape
    gd = NSA_KV_HEADS * HEAD_DIM
    nc = t // CMP_BLOCK
    tiles = gd // LANE
    return pl.pallas_call(
        functools.partial(_compress_prompt_body, nc=nc),
        grid=(2, b, tiles),
        in_specs=[pl.BlockSpec((None, t, LANE), lambda c, i, h: (i, 0, col0 // LANE + c * tiles + h)),
                  pl.BlockSpec((None,) + pe.shape[1:], lambda c, i, h: (c, 0, 0)),
                  pl.BlockSpec((None,) + w1_bf.shape[1:], lambda c, i, h: (c, 0, 0, 0)),
                  pl.BlockSpec((None,) + w2_bf.shape[1:], lambda c, i, h: (c, 0, 0))],
        out_specs=pl.BlockSpec((None, None, nc, LANE), lambda c, i, h: (c, i, 0, h)),
        out_shape=jax.ShapeDtypeStruct((2, b, nc, gd), F32),
        compiler_params=_params("parallel", "parallel", "parallel"),
        name="nsa_compress_prompt",
    )(proj3, pe, w1_bf, w2_bf)


def _gather_pages(pt_ref, b, cache_hbm, layer, row0, nrows, buf, sem, n_pages):
    def copy(p):
        return pltpu.make_async_copy(cache_hbm.at[layer, pt_ref[b, p], pl.ds(row0, nrows), :], buf.at[p], sem)

    def start():
        for p in range(n_pages):
            copy(p).start()

    def wait():
        for p in range(n_pages):
            copy(p).wait()

    return start, wait


ROW_GROUP = 8
PAGES_PER_GROUP = ROW_GROUP * CMP_BLOCK // PAGE_SIZE


def _block_row_permutation():
    n = PAGES_PER_GROUP * PAGE_SIZE
    out_row = np.arange(n)
    src = (out_row % ROW_GROUP) * CMP_BLOCK + out_row // ROW_GROUP
    return (src[:, None] == np.arange(n)[None, :]).astype(np.float32)


def _compress_sample_body(pt_ref, cache_hbm, perm_ref, pe_ref, w1_ref, w2_ref, o_ref, buf, rowbuf, sem,
                          *, layer, n_pages, nc):
    gd = NSA_KV_HEADS * HEAD_DIM
    tiles = gd // LANE
    b = pl.program_id(0)
    gather = lambda seq: _gather_pages(pt_ref, seq, cache_hbm, layer, 0, 2 * gd, buf, sem, n_pages)

    @pl.when(b == 0)
    def _():
        gather(b)[0]()

    gather(b)[1]()

    def to_rows(pg, carry):
        for c in range(2 * tiles):
            x_t = jnp.concatenate([buf[pg * PAGES_PER_GROUP + u, c * LANE:(c + 1) * LANE, :]
                                   for u in range(PAGES_PER_GROUP)], axis=1).astype(BF16)
            rows = lax.dot_general(perm_ref[...], x_t, NT_DIMS, preferred_element_type=F32)
            rowbuf[c, pg] = rows.reshape(CMP_BLOCK, ROW_GROUP, LANE)
        return carry

    lax.fori_loop(0, n_pages // PAGES_PER_GROUP, to_rows, 0)

    @pl.when(b + 1 < pl.num_programs(0))
    def _():
        gather(b + 1)[0]()

    for c in range(2):
        srcs = [rowbuf.at[c * tiles + h] for h in range(tiles)]
        out = _compress_rows(lambda r: jnp.concatenate([s[:, r].reshape(nc, LANE) for s in srcs], axis=0),
                             pe_ref.at[c], w1_ref.at[c], w2_ref.at[c], tiles * nc)
        for h in range(tiles):
            o_ref[c, :, h * LANE:(h + 1) * LANE] = out[h * nc:(h + 1) * nc]


def _compress_sample(cache_t, layer, page_table, pe, w1_bf, w2_bf):
    bs, n_pages = page_table.shape
    gd = NSA_KV_HEADS * HEAD_DIM
    nc = n_pages * PAGE_SIZE // CMP_BLOCK
    assert n_pages % PAGES_PER_GROUP == 0
    perm = jnp.asarray(_block_row_permutation(), BF16)
    tokens = pl.pallas_call(
        functools.partial(_compress_sample_body, layer=layer, n_pages=n_pages, nc=nc),
        grid_spec=pltpu.PrefetchScalarGridSpec(
            num_scalar_prefetch=1,
            grid=(bs,),
            in_specs=[pl.BlockSpec(memory_space=pl.ANY),
                      pl.BlockSpec(perm.shape, lambda i, pt: (0, 0)),
                      pl.BlockSpec(pe.shape, lambda i, pt: (0, 0, 0)),
                      pl.BlockSpec(w1_bf.shape, lambda i, pt: (0, 0, 0, 0)),
                      pl.BlockSpec(w2_bf.shape, lambda i, pt: (0, 0, 0))],
            out_specs=pl.BlockSpec((None, 2, nc, gd), lambda i, pt: (i, 0, 0, 0)),
            scratch_shapes=[pltpu.VMEM((n_pages, 2 * gd, PAGE_SIZE), F32),
                            pltpu.VMEM((2 * gd // LANE, n_pages // PAGES_PER_GROUP, CMP_BLOCK, ROW_GROUP, LANE),
                                       F32),
                            pltpu.SemaphoreType.DMA(())]),
        out_shape=jax.ShapeDtypeStruct((bs, 2, nc, gd), F32),
        compiler_params=_params("arbitrary"),
        name="nsa_compress_sample",
    )(page_table, cache_t, perm, pe, w1_bf, w2_bf)
    return tokens.reshape(bs, 2, nc // 2, 2, gd).transpose(0, 1, 3, 2, 4).reshape(bs, 2, nc, gd)


def _nsa_prompt_body(slope_ref, alibi_ref, q_ref, gate_ref, kc_ref, vc_ref, ks_ref, vs_ref, kw_ref, vw_ref,
                     o_ref, ns_sc, own_sc, tail_sc, m_sc, l_sc, acc_sc, o_sc, *, tq, tk, wk):
    i = pl.program_id(1)
    g_n, r_n, hd = NSA_KV_HEADS, NSA_GROUP, HEAD_DIM
    n_h = g_n * r_n
    rows = r_n * tq
    nc = kc_ref.shape[0]
    ns = nc // 2
    t_q = i * tq + lax.broadcasted_iota(jnp.int32, (1, tq), 1)
    t_row = _tile_lanes(t_q, r_n)
    cur = t_q // SEL_BLOCK
    blk = lax.broadcasted_iota(jnp.int32, (ns, 1), 0)
    q_t = q_ref[...].T * (QK_SCALE * LOG2E)
    gate_t = jax.nn.sigmoid(gate_ref[...]).T
    vc_t = vc_ref[...].T
    c_end, c_mid = _cmp_positions(nc, 0)
    key_i = lax.broadcasted_iota(jnp.int32, (tq, 1), 0)
    causal = _tile_lanes(jnp.where(key_i <= lax.broadcasted_iota(jnp.int32, (1, tq), 1), 0.0, NEG_INF), r_n)
    key_d = pl.multiple_of(i * tq, tq)
    blk_d = pl.multiple_of((i * tq // SEL_BLOCK) // 8 * 8, 8)
    n_past = (i * tq + tk - 1) // tk
    w0 = pl.multiple_of(jnp.maximum((i + 1) * tq - wk, 0), tq)
    w_dist = t_q - (w0 + lax.broadcasted_iota(jnp.int32, (wk, 1), 0))
    in_win = _tile_lanes(jnp.where((w_dist >= 0) & (w_dist < WINDOW), 0.0, NEG_INF), r_n)
    no_sel = jnp.zeros((AUG_OH, rows), F32)

    for g in range(g_n):
        gsl = slice(g * hd, (g + 1) * hd)
        asl = slice(g * AUG, (g + 1) * AUG)
        qg = _group_q_t(q_t, g, r_n)
        slope = slope_ref[g]

        s = jnp.dot(kc_ref[:, gsl], qg, preferred_element_type=F32, precision=HIGHEST)
        s = s - slope * jnp.abs(t_row.astype(F32) - c_mid)
        valid = c_end <= t_row
        s = jnp.where(valid, s, NEG_INF)
        p = jnp.where(valid, jnp.exp2(s - jnp.max(s, axis=0, keepdims=True)), 0.0)
        l = jnp.sum(p, axis=0, keepdims=True)
        p = p / jnp.where(l > 0.0, l, 1.0)
        o_cmp = jnp.dot(vc_t[gsl].astype(BF16), p.astype(BF16), preferred_element_type=F32)
        imp = p[:, 0:tq]
        for r in range(1, r_n):
            imp = imp + p[:, r * tq:(r + 1) * tq]
        imp = imp[:ns] + imp[ns:]
        sel = _topk_mask(imp, blk < cur, min(N_SEL - 1, ns), 0)
        ns_sc[g] = _tile_lanes(1.0 - jnp.where(blk < i * tq // SEL_BLOCK, sel, 0.0), r_n)
        own_sc[g] = _tile_lanes(1.0 - jnp.where(blk == cur, 1.0, sel), r_n)

        tail_sc[g] = _aug_tail(qg, alibi_ref[g])
        for r in range(r_n):
            h = g * r_n + r
            o_sc[h * hd:(h + 1) * hd, :] = gate_t[h:h + 1] * o_cmp[:, r * tq:(r + 1) * tq]

    def own_tile(g):
        return (lambda: jnp.dot(ks_ref[pl.ds(key_d, tq), g * AUG:(g + 1) * AUG],
                                _aug_rhs(own_sc[g, pl.ds(blk_d, 8), :], tail_sc[g]),
                                preferred_element_type=F32) + causal,
                lambda: vs_ref[g * hd:(g + 1) * hd, pl.ds(key_d, tq)])

    def window_tile(g):
        return (lambda: jnp.dot(kw_ref[pl.ds(w0, wk), g * AUG:(g + 1) * AUG], _aug_rhs(no_sel, tail_sc[g]),
                                preferred_element_type=F32) + in_win,
                lambda: vw_ref[g * hd:(g + 1) * hd, pl.ds(w0, wk)])

    states = _soft_first_tiles([job(g) for g in range(g_n) for job in (own_tile, window_tile)])
    for g in range(g_n):
        m_sc[g], l_sc[g], acc_sc[g] = states[2 * g]
        o_win = _soft_done(states[2 * g + 1])
        for r in range(r_n):
            h = g * r_n + r
            o_sc[h * hd:(h + 1) * hd, :] += gate_t[2 * n_h + h:2 * n_h + h + 1] * o_win[:, r * tq:(r + 1) * tq]

    def past(j, carry):
        key0 = pl.multiple_of(j * tk, tk)
        blk0 = pl.multiple_of(j * 8, 8)
        _soft_more_groups(m_sc, l_sc, acc_sc, ks_ref, vs_ref, key0, tk,
                          lambda g: _aug_rhs(ns_sc[g, pl.ds(blk0, 8), :], tail_sc[g]))
        return carry

    lax.fori_loop(0, n_past, past, 0)

    for g in range(g_n):
        o_slc = acc_sc[g] / l_sc[g]
        for r in range(r_n):
            h = g * r_n + r
            o_sc[h * hd:(h + 1) * hd, :] += gate_t[n_h + h:n_h + h + 1] * o_slc[:, r * tq:(r + 1) * tq]
    o_ref[...] = o_sc[...].T


def _nsa_prompt_attn(proj3, k_slc, v_slc_t, k_win, v_win_t, kcvc, slope_rows, alibi_rows, tq):
    b, t, _ = proj3.shape
    g_n = NSA_KV_HEADS
    gd = g_n * HEAD_DIM
    qd = g_n * NSA_GROUP * HEAD_DIM
    nc = kcvc.shape[2]
    rows = NSA_GROUP * tq
    tk = NSA_PAST_TILE
    wk = WINDOW + tq
    assert t % tk == 0 and t >= wk and (nc // 2) % 8 == 0
    k_spec = pl.BlockSpec((None, t, g_n * AUG), lambda bi, i: (bi, 0, 0))
    v_spec = pl.BlockSpec((None, gd, t), lambda bi, i: (bi, 0, 0))
    return pl.pallas_call(
        functools.partial(_nsa_prompt_body, tq=tq, tk=tk, wk=wk),
        grid=(b, t // tq),
        in_specs=[pl.BlockSpec((g_n, 1, rows), lambda bi, i: (0, 0, 0)),
                  pl.BlockSpec((g_n, AUG_AL, rows), lambda bi, i: (0, 0, 0)),
                  pl.BlockSpec((None, tq, qd), lambda bi, i: (bi, i, 0)),
                  pl.BlockSpec((None, tq, LANE), lambda bi, i: (bi, i, (qd + 6 * gd) // LANE)),
                  pl.BlockSpec((None, None, nc, gd), lambda bi, i: (0, bi, 0, 0)),
                  pl.BlockSpec((None, None, nc, gd), lambda bi, i: (1, bi, 0, 0)),
                  k_spec, v_spec, k_spec, v_spec],
        out_specs=pl.BlockSpec((None, tq, qd), lambda bi, i: (bi, i, 0)),
        out_shape=jax.ShapeDtypeStruct((b, t, qd), F32),
        scratch_shapes=[pltpu.VMEM((g_n, nc // 2, rows), F32), pltpu.VMEM((g_n, nc // 2, rows), F32),
                        pltpu.VMEM((g_n, AUG - AUG_OH, rows), BF16),
                        pltpu.VMEM((g_n, 1, rows), F32), pltpu.VMEM((g_n, 1, rows), F32),
                        pltpu.VMEM((g_n, HEAD_DIM, rows), F32), pltpu.VMEM((qd, tq), F32)],
        compiler_params=_params("parallel", "arbitrary"),
        name="nsa_prompt_attn",
    )(slope_rows, alibi_rows, proj3, proj3, kcvc, kcvc, k_slc, v_slc_t, k_win, v_win_t)


def _page_tile(buf, slot, j, tk, lo, hi):
    pages = tk // PAGE_SIZE
    return jnp.concatenate([buf[slot, j * pages + u, lo:hi, :] for u in range(pages)], axis=1).astype(BF16)


def _block_diag_q(q_ref, g_n, r_n):
    hd = HEAD_DIM
    ts = q_ref.shape[0]
    out = []
    for g in range(g_n):
        qg = jnp.concatenate([q_ref[:, (g * r_n + r) * hd:(g * r_n + r + 1) * hd] for r in range(r_n)], axis=0)
        parts = [jnp.zeros((r_n * ts, hd), F32)] * g_n
        parts[g] = qg
        out.append(jnp.concatenate(parts, axis=-1))
    return jnp.concatenate(out, axis=0)


def _nsa_sample_body(pt_ref, slope_ref, q_ref, gate_ref, kcvc_ref, snew_ref, wnew_ref, win_ref, cache_hbm,
                     o_ref, buf, sem, *, layer, n_pages, tk):
    g_n, r_n, hd = NSA_KV_HEADS, NSA_GROUP, HEAD_DIM
    gd = g_n * hd
    ts = q_ref.shape[0]
    rows = g_n * r_n * ts
    past = n_pages * PAGE_SIZE
    b = pl.program_id(0)
    slot = b % 2
    gather = lambda seq, sl: _gather_pages(pt_ref, seq, cache_hbm, layer, 2 * gd, 2 * gd, buf.at[sl],
                                           sem.at[sl], n_pages)

    @pl.when(b == 0)
    def _():
        gather(b, slot)[0]()

    @pl.when(b + 1 < pl.num_programs(0))
    def _():
        gather(b + 1, 1 - slot)[0]()

    qf = _block_diag_q(q_ref, g_n, r_n) * QK_SCALE
    qb = qf.astype(BF16)
    slope = slope_ref[...]
    t = past + lax.broadcasted_iota(jnp.int32, (rows, 1), 0) % ts
    new_pos = past + lax.broadcasted_iota(jnp.int32, (1, ts), 1)
    causal = lambda dist: dist >= 0

    nc = kcvc_ref.shape[1]
    ns = nc // 2
    p_cmp, o_cmp = _cmp_attention(qf, kcvc_ref[0], kcvc_ref[1], t, slope)
    imp = []
    for g in range(g_n):
        base = g * r_n * ts
        acc = p_cmp[base:base + ts]
        for r in range(1, r_n):
            acc = acc + p_cmp[base + r * ts:base + (r + 1) * ts]
        imp.append(acc)
    imp = jnp.concatenate(imp, axis=0)
    imp = imp[:, :ns] + imp[:, ns:]
    sel = _topk_mask(imp, jnp.full(imp.shape, True), min(N_SEL - 1, ns), 1).astype(BF16)
    sel = jnp.concatenate([sel[g * ts:(g + 1) * ts] for g in range(g_n) for _ in range(r_n)], axis=0)

    wp = win_ref.shape[1]
    win_pos = (past - wp) + lax.broadcasted_iota(jnp.int32, (1, wp), 1)
    in_win = lambda dist: (dist >= 0) & (dist < WINDOW)
    carry = _softmax_step(_softmax_init(rows, gd), qb, win_ref[:gd, :].astype(BF16),
                          win_ref[gd:, :].astype(BF16), t, slope, win_pos, in_win, True)
    carry = _softmax_step(carry, qb, wnew_ref[:, :gd].astype(BF16), wnew_ref[:, gd:].astype(BF16),
                          t, slope, new_pos, in_win, False)
    o_win = _softmax_done(carry)

    gather(b, slot)[1]()

    def slc_step(jj, carry):
        return _paged_tile_pair(carry, jj, qb, buf, slot, tk, gd, t, slope, sel, SEL_BLOCK.bit_length() - 1)

    carry = lax.fori_loop(0, past // (2 * tk), slc_step, _softmax_init(rows, gd))
    carry = _softmax_step(carry, qb, snew_ref[:, :gd].astype(BF16), snew_ref[:, gd:].astype(BF16),
                          t, slope, new_pos, causal, False)
    o_slc = _softmax_done(carry)

    gate = jax.nn.sigmoid(gate_ref[...])
    n_h = g_n * r_n
    for h in range(n_h):
        g = h // r_n
        rs = slice(h * ts, (h + 1) * ts)
        cs = slice(g * hd, (g + 1) * hd)
        o_ref[:, h * hd:(h + 1) * hd] = (gate[:, h:h + 1] * o_cmp[rs, cs]
                                         + gate[:, n_h + h:n_h + h + 1] * o_slc[rs, cs]
                                         + gate[:, 2 * n_h + h:2 * n_h + h + 1] * o_win[rs, cs])


def _nsa_sample_attn(proj3, kcvc, win_t, cache_t, layer, page_table, slopes, tk=512):
    bs, ts, _ = proj3.shape
    n_pages = page_table.shape[1]
    gd = NSA_KV_HEADS * HEAD_DIM
    qd = NSA_KV_HEADS * NSA_GROUP * HEAD_DIM
    nc = kcvc.shape[2]
    wp = win_t.shape[3]
    slope_rows = jnp.repeat(slopes, ts).reshape(-1, 1)
    assert (n_pages * PAGE_SIZE) % (2 * tk) == 0
    return pl.pallas_call(
        functools.partial(_nsa_sample_body, layer=layer, n_pages=n_pages, tk=tk),
        grid_spec=pltpu.PrefetchScalarGridSpec(
            num_scalar_prefetch=1,
            grid=(bs,),
            in_specs=[pl.BlockSpec((slope_rows.shape[0], 1), lambda i, pt: (0, 0)),
                      pl.BlockSpec((None, ts, qd), lambda i, pt: (i, 0, 0)),
                      pl.BlockSpec((None, ts, LANE), lambda i, pt: (i, 0, (qd + 6 * gd) // LANE)),
                      pl.BlockSpec((None, 2, nc, gd), lambda i, pt: (i, 0, 0, 0)),
                      pl.BlockSpec((None, ts, 2 * gd), lambda i, pt: (i, 0, (qd + 2 * gd) // (2 * gd))),
                      pl.BlockSpec((None, ts, 2 * gd), lambda i, pt: (i, 0, (qd + 4 * gd) // (2 * gd))),
                      pl.BlockSpec((None, None, 2 * gd, wp), lambda i, pt: (layer, i, 0, 0)),
                      pl.BlockSpec(memory_space=pl.ANY)],
            out_specs=pl.BlockSpec((None, ts, qd), lambda i, pt: (i, 0, 0)),
            scratch_shapes=[pltpu.VMEM((2, n_pages, 2 * gd, PAGE_SIZE), F32), pltpu.SemaphoreType.DMA((2,))]),
        out_shape=jax.ShapeDtypeStruct((bs, ts, qd), F32),
        compiler_params=_params("arbitrary"),
        name="nsa_sample_attn",
    )(page_table, slope_rows, proj3, proj3, kcvc, proj3, proj3, win_t, cache_t)


def _moba_prompt_body(alibi_ref, q_ref, kf_ref, k_ref, v_ref, o_ref, kmean_sc, rhs_sc, m_sc, l_sc, acc_sc, o_sc,
                      *, tq, tk):
    i = pl.program_id(1)
    g_n, r_n, hd = MOBA_KV_HEADS, MOBA_GROUP, HEAD_DIM
    rows = r_n * tq
    nb = kmean_sc.shape[0]

    @pl.when(i == 0)
    def _():
        kmean_sc[...] = jnp.concatenate(
            [jnp.mean(kf_ref[n * MOBA_BLOCK:(n + 1) * MOBA_BLOCK, :], axis=0, keepdims=True) for n in range(nb)],
            axis=0)

    t_q = i * tq + lax.broadcasted_iota(jnp.int32, (1, tq), 1)
    cur = i * tq // MOBA_BLOCK
    blk = lax.broadcasted_iota(jnp.int32, (nb, 1), 0)
    q_t = q_ref[...].T
    own0 = pl.multiple_of(cur * MOBA_BLOCK, MOBA_BLOCK)
    own_pos = own0 + lax.broadcasted_iota(jnp.int32, (MOBA_BLOCK, 1), 0)
    causal = _tile_lanes(jnp.where(own_pos <= t_q, 0.0, NEG_INF), r_n)
    n_past = (cur * MOBA_BLOCK + tk - 1) // tk
    no_sel = jnp.zeros((AUG_OH, rows), F32)

    for g in range(g_n):
        gsl = slice(g * hd, (g + 1) * hd)
        qg = _group_q_t(q_t, g, r_n)
        gs = jnp.dot(kmean_sc[:, gsl], qg, preferred_element_type=F32, precision=HIGHEST)
        sel = _topk_mask(gs, blk < cur, min(MOBA_TOPK, nb), 0)
        rhs_sc[g] = _aug_rhs(1.0 - sel, _aug_tail(qg * (QK_SCALE * LOG2E), alibi_ref[g]))

    def own_tile(g):
        def logits():
            tail = jnp.concatenate([rhs_sc[g, :hd], rhs_sc[g, hd + AUG_OH:]], axis=0)
            return jnp.dot(k_ref[pl.ds(own0, MOBA_BLOCK), g * AUG:(g + 1) * AUG], _aug_rhs(no_sel, tail),
                           preferred_element_type=F32) + causal
        return logits, lambda: v_ref[g * hd:(g + 1) * hd, pl.ds(own0, MOBA_BLOCK)]

    for g, state in enumerate(_soft_first_tiles([own_tile(g) for g in range(g_n)])):
        m_sc[g], l_sc[g], acc_sc[g] = state

    def past(j, carry):
        key0 = pl.multiple_of(j * tk, tk)
        _soft_more_groups(m_sc, l_sc, acc_sc, k_ref, v_ref, key0, tk, lambda g: rhs_sc[g])
        return carry

    lax.fori_loop(0, n_past, past, 0)

    for g in range(g_n):
        o = acc_sc[g] / l_sc[g]
        for r in range(r_n):
            h = g * r_n + r
            o_sc[h * hd:(h + 1) * hd, :] = o[:, r * tq:(r + 1) * tq]
    o_ref[...] = o_sc[...].T


def _moba_prompt_attn(proj3, k_aug, v_t, alibi_rows, tq, tk=512):
    b, t, _ = proj3.shape
    g_n = MOBA_KV_HEADS
    gd = g_n * HEAD_DIM
    qd = g_n * MOBA_GROUP * HEAD_DIM
    rows = MOBA_GROUP * tq
    assert t % tk == 0 and tk % MOBA_BLOCK == 0 and MOBA_BLOCK % tq == 0 and t // MOBA_BLOCK <= AUG_OH
    return pl.pallas_call(
        functools.partial(_moba_prompt_body, tq=tq, tk=tk),
        grid=(b, t // tq),
        in_specs=[pl.BlockSpec((g_n, AUG_AL, rows), lambda bi, i: (0, 0, 0)),
                  pl.BlockSpec((None, tq, qd), lambda bi, i: (bi, i, 0)),
                  pl.BlockSpec((None, t, gd), lambda bi, i: (bi, 0, qd // gd)),
                  pl.BlockSpec((None, t, g_n * AUG), lambda bi, i: (bi, 0, 0)),
                  pl.BlockSpec((None, gd, t), lambda bi, i: (bi, 0, 0))],
        out_specs=pl.BlockSpec((None, tq, qd), lambda bi, i: (bi, i, 0)),
        out_shape=jax.ShapeDtypeStruct((b, t, qd), F32),
        scratch_shapes=[pltpu.VMEM((t // MOBA_BLOCK, gd), F32), pltpu.VMEM((g_n, AUG, rows), BF16),
                        pltpu.VMEM((g_n, 1, rows), F32), pltpu.VMEM((g_n, 1, rows), F32),
                        pltpu.VMEM((g_n, HEAD_DIM, rows), F32), pltpu.VMEM((qd, tq), F32)],
        compiler_params=_params("parallel", "arbitrary"),
        name="moba_prompt_attn",
    )(alibi_rows, proj3, proj3, k_aug, v_t)


def _moba_sample_body(pt_ref, slope_ref, q_ref, new_ref, cache_hbm, o_ref, buf, sem, *, layer, n_pages, tk):
    g_n, r_n, hd = MOBA_KV_HEADS, MOBA_GROUP, HEAD_DIM
    gd = g_n * hd
    ts = q_ref.shape[0]
    rows = g_n * r_n * ts
    past = n_pages * PAGE_SIZE
    nb = past // MOBA_BLOCK
    b = pl.program_id(0)
    slot = b % 2
    gather = lambda seq, sl: _gather_pages(pt_ref, seq, cache_hbm, layer, 0, 2 * gd, buf.at[sl], sem.at[sl],
                                           n_pages)

    @pl.when(b == 0)
    def _():
        gather(b, slot)[0]()

    @pl.when(b + 1 < pl.num_programs(0))
    def _():
        gather(b + 1, 1 - slot)[0]()

    qf = _block_diag_q(q_ref, g_n, r_n)
    qb = (qf * QK_SCALE).astype(BF16)
    slope = slope_ref[...]
    t = past + lax.broadcasted_iota(jnp.int32, (rows, 1), 0) % ts
    new_pos = past + lax.broadcasted_iota(jnp.int32, (1, ts), 1)
    gather(b, slot)[1]()

    blk = lax.broadcasted_iota(jnp.int32, (gd, nb), 1)
    kmean_t = jnp.zeros((gd, nb), F32)
    pages = MOBA_BLOCK // PAGE_SIZE
    for n in range(nb):
        col = sum(jnp.sum(buf[slot, n * pages + u, :gd, :], axis=1, keepdims=True) for u in range(pages))
        kmean_t = jnp.where(blk == n, col / MOBA_BLOCK, kmean_t)
    gs = jnp.dot(qf, kmean_t, preferred_element_type=F32, precision=HIGHEST)
    sel = _topk_mask(gs, jnp.full(gs.shape, True), min(MOBA_TOPK, nb), 1).astype(BF16)

    def step(jj, carry):
        return _paged_tile_pair(carry, jj, qb, buf, slot, tk, gd, t, slope, sel, MOBA_BLOCK.bit_length() - 1)

    carry = lax.fori_loop(0, past // (2 * tk), step, _softmax_init(rows, gd))
    carry = _softmax_step(carry, qb, new_ref[:, :gd].astype(BF16), new_ref[:, gd:].astype(BF16),
                          t, slope, new_pos, lambda dist: dist >= 0, False)
    o = _softmax_done(carry)
    for h in range(g_n * r_n):
        g = h // r_n
        o_ref[:, h * hd:(h + 1) * hd] = o[h * ts:(h + 1) * ts, g * hd:(g + 1) * hd]


def _moba_sample_attn(proj3, cache_t, layer, page_table, slopes, tk=512):
    bs, ts, _ = proj3.shape
    n_pages = page_table.shape[1]
    gd = MOBA_KV_HEADS * HEAD_DIM
    qd = MOBA_KV_HEADS * MOBA_GROUP * HEAD_DIM
    slope_rows = jnp.repeat(slopes, ts).reshape(-1, 1)
    assert (n_pages * PAGE_SIZE) % (2 * tk) == 0
    return pl.pallas_call(
        functools.partial(_moba_sample_body, layer=layer, n_pages=n_pages, tk=tk),
        grid_spec=pltpu.PrefetchScalarGridSpec(
            num_scalar_prefetch=1,
            grid=(bs,),
            in_specs=[pl.BlockSpec((slope_rows.shape[0], 1), lambda i, pt: (0, 0)),
                      pl.BlockSpec((None, ts, qd), lambda i, pt: (i, 0, 0)),
                      pl.BlockSpec((None, ts, 2 * gd), lambda i, pt: (i, 0, qd // (2 * gd))),
                      pl.BlockSpec(memory_space=pl.ANY)],
            out_specs=pl.BlockSpec((None, ts, qd), lambda i, pt: (i, 0, 0)),
            scratch_shapes=[pltpu.VMEM((2, n_pages, 2 * gd, PAGE_SIZE), F32), pltpu.SemaphoreType.DMA((2,))]),
        out_shape=jax.ShapeDtypeStruct((bs, ts, qd), F32),
        compiler_params=_params("arbitrary"),
        name="moba_sample_attn",
    )(page_table, slope_rows, proj3, proj3, cache_t)


PROMPT_Q_TILE = 128


def _alibi_slopes(n_heads):
    return jnp.exp2(-8.0 * jnp.arange(1, n_heads + 1, dtype=F32) / n_heads)


def _alibi_query_rows(slopes, g_n, r_n, tq):
    s2 = slopes * LOG2E
    d0 = s2.astype(BF16).astype(F32)
    d1 = (s2 - d0).astype(BF16).astype(F32)
    d2 = (s2 - d0 - d1).astype(BF16).astype(F32)
    zero = jnp.zeros_like(s2)
    digits = jnp.stack([d0, d1, d2, POS_DIGIT * d0, POS_DIGIT * d1, POS_DIGIT * d2, zero, zero], axis=0)
    lanes = lambda a: jnp.repeat(a.reshape(-1, g_n, r_n), tq, axis=2).transpose(1, 0, 2)
    return lanes(s2[None]), lanes(digits)


def _positions_minor(a, n_lead):
    nd = a.ndim
    a = jnp.transpose(a, tuple(range(n_lead)) + tuple(range(n_lead + 1, nd)) + (n_lead,))
    return a.reshape(a.shape[:n_lead] + (-1, a.shape[-1]))


def _nsa_layer(xp, xs, bp, bs, cache_t, win_t, layer, win_state, page_table, w_in, w_out, pe, w1, w2, g, b,
               alpha):
    tp, ts = xp.shape[0] // bp, xs.shape[0] // bs
    g_n, hd = NSA_KV_HEADS, HEAD_DIM
    gd = g_n * hd
    qd = g_n * NSA_GROUP * hd
    n_in = w_in.shape[1]
    n_pad = -(-n_in // LANE) * LANE
    w_in_bf = jnp.pad(w_in, ((0, 0), (0, n_pad - n_in))).astype(BF16)
    pe, w1_bf, w2_bf = _compress_weights(pe, w1, w2)
    w_out_bf = w_out.astype(BF16)
    slopes = _alibi_slopes(g_n * NSA_GROUP)
    past = page_table.shape[1] * PAGE_SIZE
    assert PAGE_SIZE == LANE and past % (2 * CMP_BLOCK) == 0 and ts <= CMP_BLOCK

    assert tp <= POS_DIGIT * 256
    proj_p, k_slc, k_win, v_slc_t, v_win_t = _proj(
        xp, w_in_bf, tp, keys=((qd + 2 * gd, SEL_BLOCK, NSA_PAST_TILE // SEL_BLOCK), (qd + 4 * gd, SEL_BLOCK, 0)),
        values=(qd + 3 * gd, qd + 5 * gd), g_n=g_n)
    (proj_s,) = _proj(xs, w_in_bf)
    proj_p3 = proj_p.reshape(bp, tp, n_pad)
    proj_s3 = proj_s.reshape(bs, ts, n_pad)
    kcvc_p = _compress_prompt(proj_p3, qd, pe, w1_bf, w2_bf)
    slope_rows, alibi_rows = _alibi_query_rows(slopes, g_n, NSA_GROUP, PROMPT_Q_TILE)
    o_p = _nsa_prompt_attn(proj_p3, k_slc.reshape(bp, tp, g_n * AUG), v_slc_t, k_win.reshape(bp, tp, g_n * AUG),
                           v_win_t, kcvc_p, slope_rows, alibi_rows, PROMPT_Q_TILE)
    kcvc_s = _compress_sample(cache_t, layer, page_table, pe, w1_bf, w2_bf)
    o_s = _nsa_sample_attn(proj_s3, kcvc_s, win_t, cache_t, layer, page_table, slopes)
    xp = _out_ln(o_p.reshape(bp * tp, qd), xp, w_out_bf, g, b, alpha)
    xs = _out_ln(o_s.reshape(bs * ts, qd), xs, w_out_bf, g, b, alpha)

    kv_shape = (4, g_n, hd)
    win_shape = (2, g_n, hd)
    kv_p = proj_p3[:, :, qd:qd + 4 * gd].reshape((bp, tp) + kv_shape)
    kv_s = proj_s3[:, :, qd:qd + 4 * gd].reshape((bs, ts) + kv_shape)
    win_p = proj_p3[:, tp - min(WINDOW, tp):, qd + 4 * gd:qd + 6 * gd].reshape((bp, min(WINDOW, tp)) + win_shape)
    win_s = jnp.concatenate([win_state, proj_s3[:, :, qd + 4 * gd:qd + 6 * gd].reshape((bs, ts) + win_shape)], axis=1)
    win_s = win_s[:, win_s.shape[1] - min(WINDOW, win_s.shape[1]):]
    return xp, xs, kv_p, kv_s, win_p, win_s


def _pool_layer(xp, xs, bp, bs, state, past, w, scale, g, b, alpha):
    d = xp.shape[1]
    tp, ts = xp.shape[0] // bp, xs.shape[0] // bs
    assert POOL_HALO % ts == 0 and tp % POOL_HALO == 0
    w_bf = w.astype(BF16)
    scale = scale.reshape(1, d)
    xs3 = xs.reshape(bs, ts, d)
    xe_s = jnp.concatenate([jnp.zeros((bs, POOL_HALO - POOL_PAST, d), F32), state, xs3], axis=1)
    pool_p = xp.reshape(bp, tp, d)[:, tp - POOL_PAST:]
    pool_s = xe_s[:, xe_s.shape[1] - POOL_PAST:]
    xp = _pool_ln_prompt(xp, tp, w_bf, scale, g, b, alpha)
    xs = _pool_ln_sample(xe_s, ts, past, w_bf, scale, g, b, alpha)
    return xp, xs, pool_p, pool_s


def _moba_layer(xp, xs, bp, bs, cache_t, layer, page_table, w_in, w_out, g, b, alpha):
    tp, ts = xp.shape[0] // bp, xs.shape[0] // bs
    g_n, hd = MOBA_KV_HEADS, HEAD_DIM
    gd = g_n * hd
    qd = g_n * MOBA_GROUP * hd
    n_in = w_in.shape[1]
    w_in_bf, w_out_bf = w_in.astype(BF16), w_out.astype(BF16)
    slopes = _alibi_slopes(g_n * MOBA_GROUP)
    past = page_table.shape[1] * PAGE_SIZE
    assert PAGE_SIZE == LANE and past % MOBA_BLOCK == 0 and ts <= MOBA_BLOCK

    assert tp <= POS_DIGIT * 256
    proj_p, k_aug, v_t = _proj(xp, w_in_bf, tp, keys=((qd, MOBA_BLOCK, AUG_OH),), values=(qd + gd,), g_n=g_n)
    (proj_s,) = _proj(xs, w_in_bf)
    proj_p3 = proj_p.reshape(bp, tp, n_in)
    proj_s3 = proj_s.reshape(bs, ts, n_in)
    _, alibi_rows = _alibi_query_rows(slopes, g_n, MOBA_GROUP, PROMPT_Q_TILE)
    o_p = _moba_prompt_attn(proj_p3, k_aug.reshape(bp, tp, g_n * AUG), v_t, alibi_rows, PROMPT_Q_TILE)
    o_s = _moba_sample_attn(proj_s3, cache_t, layer, page_table, slopes)
    xp = _out_ln(o_p.reshape(bp * tp, qd), xp, w_out_bf, g, b, alpha)
    xs = _out_ln(o_s.reshape(bs * ts, qd), xs, w_out_bf, g, b, alpha)
    kv_shape = (2, g_n, hd)
    kv_p = proj_p3[:, :, qd:].reshape((bp, tp) + kv_shape)
    kv_s = proj_s3[:, :, qd:].reshape((bs, ts) + kv_shape)
    return xp, xs, kv_p, kv_s


def kernel(x_prompt, x_sample, cache_nsa_kv, state_nsa_win, state_pool, cache_moba, page_table, ln_g, ln_b, mlp_w1, mlp_w2, nsa_w_in, nsa_w_out, nsa_cmp_pe, nsa_cmp_w1, nsa_cmp_w2, pool_w, pool_scale, moba_w_in, moba_w_out):
    bp, tp, d = x_prompt.shape
    bs, ts, _ = x_sample.shape
    depth = ln_g.shape[0]
    alpha = (2 * depth) ** 0.25
    past = page_table.shape[1] * PAGE_SIZE
    xp = x_prompt.reshape(bp * tp, d)
    xs = x_sample.reshape(bs * ts, d)
    nsa_cache_t = _positions_minor(cache_nsa_kv, 2)
    nsa_win_t = _positions_minor(state_nsa_win, 2)
    moba_cache_t = _positions_minor(cache_moba, 2)
    outs = {k: [] for k in ("nsa_kv_p", "nsa_kv_s", "nsa_win_p", "nsa_win_s", "pool_p", "pool_s", "moba_p", "moba_s")}
    for i in range(depth):
        kind, j = i % N_MIXERS, i // N_MIXERS
        g0, b0 = ln_g[i, 0].reshape(1, d), ln_b[i, 0].reshape(1, d)
        g1, b1 = ln_g[i, 1].reshape(1, d), ln_b[i, 1].reshape(1, d)
        if kind == 0:
            xp, xs, kv_p, kv_s, win_p, win_s = _nsa_layer(
                xp, xs, bp, bs, nsa_cache_t, nsa_win_t, j, state_nsa_win[j], page_table, nsa_w_in[j], nsa_w_out[j],
                nsa_cmp_pe[j], nsa_cmp_w1[j], nsa_cmp_w2[j], g0, b0, alpha)
            outs["nsa_kv_p"].append(kv_p); outs["nsa_kv_s"].append(kv_s)
            outs["nsa_win_p"].append(win_p); outs["nsa_win_s"].append(win_s)
        elif kind == 1:
            xp, xs, pool_p, pool_s = _pool_layer(xp, xs, bp, bs, state_pool[j], past, pool_w[j], pool_scale[j],
                                                 g0, b0, alpha)
            outs["pool_p"].append(pool_p); outs["pool_s"].append(pool_s)
        else:
            xp, xs, kv_p, kv_s = _moba_layer(xp, xs, bp, bs, moba_cache_t, j, page_table, moba_w_in[j],
                                             moba_w_out[j], g0, b0, alpha)
            outs["moba_p"].append(kv_p); outs["moba_s"].append(kv_s)
        w1_bf, w2_bf = mlp_w1[i].astype(BF16), mlp_w2[i].astype(BF16)
        xp = _mlp_ln(xp, w1_bf, w2_bf, g1, b1, alpha)
        xs = _mlp_ln(xs, w1_bf, w2_bf, g1, b1, alpha)
    return (xp.reshape(bp, tp, d), xs.reshape(bs, ts, d),
            jnp.stack(outs["nsa_kv_p"]), jnp.stack(outs["nsa_kv_s"]),
            jnp.stack(outs["nsa_win_p"]), jnp.stack(outs["nsa_win_s"]),
            jnp.stack(outs["pool_p"]), jnp.stack(outs["pool_s"]),
            jnp.stack(outs["moba_p"]), jnp.stack(outs["moba_s"]))
```
